```python
import math
import jax, jax.numpy as jnp
from jax import lax
import numpy as np

D_MODEL = 1024
BATCH = 8
SEQ = 4096
DEPTH = 4

GRID_W = 64
CTX_LEN = 256
HEAD_DIM = 64
HY_WIDTH = D_MODEL // 4
HY_GROUP_DIM = 64
HY_ORDER = 2
HY_BANDS = 16
HY_POS_DIM = 1 + 2 * HY_BANDS
HY_FFN = 64
HY_DECAY_TARGET = 1e-2
HY_FAST_DECAY = 0.3
HY_SLOW_DECAY = 1.5
DA_WIDTH = D_MODEL // 2
DA_VDIM = 2 * HEAD_DIM
DA_HEADS = DA_WIDTH // DA_VDIM
Q_BLOCK = 128
ML_WIDTH = D_MODEL // 4
ML_HEADS = ML_WIDTH // HEAD_DIM
ML_CHUNK = 64
MIX_WIDTH = HY_WIDTH + DA_WIDTH + ML_WIDTH
D_FF = ((8 * D_MODEL // 3) + 127) // 128 * 128
SHORT_CONV = 3
ROPE_BASE = 10000.0
EPS = 1e-6
COL_SIZES = ((HY_ORDER + 1) * HY_WIDTH, 2 * DA_HEADS * HEAD_DIM, 2 * DA_HEADS * HEAD_DIM, DA_WIDTH,
             ML_WIDTH, ML_WIDTH, ML_WIDTH, ML_WIDTH, 4 * ML_HEADS)
IN_WIDTH = sum(COL_SIZES)
COL_SPLITS = tuple(sum(COL_SIZES[:i + 1]) for i in range(len(COL_SIZES) - 1))

kernel_name = 'hybrid_hyena_diffattn_mlstm_dit'


def rms_norm(x):
    xf = x.astype(jnp.float32)
    return (xf * lax.rsqrt(jnp.mean(xf * xf, axis=-1, keepdims=True) + EPS)).astype(x.dtype)


def group_rms_norm(x, group):
    shape = x.shape
    return rms_norm(x.reshape(shape[:-1] + (shape[-1] // group, group))).reshape(shape)


def modulate(h, shift, scale):
    return h * (1 + scale) + shift


def short_conv(u, w, b):
    L = u.shape[1]
    pad = SHORT_CONV // 2
    up = jnp.pad(u, ((0, 0), (pad, pad), (0, 0)))
    return sum(up[:, j:j + L] * w[j] for j in range(SHORT_CONV)) + b


def heads_merge(a):
    B, H, L, d = a.shape
    return a.transpose(0, 2, 1, 3).reshape(B, L, H * d)


def axial_rope_tables(n_tokens):
    rows_n = n_tokens // GRID_W
    rows = jnp.repeat(jnp.arange(rows_n, dtype=jnp.float32), GRID_W)
    cols = jnp.tile(jnp.arange(GRID_W, dtype=jnp.float32), rows_n)
    nf = HEAD_DIM // 4
    inv = ROPE_BASE ** (-jnp.arange(nf, dtype=jnp.float32) / nf)
    ang = jnp.concatenate([rows[:, None] * inv, cols[:, None] * inv], axis=-1)
    return jnp.cos(ang), jnp.sin(ang)


def apply_axial_rope(x, cos, sin):
    L = x.shape[1]
    nf = HEAD_DIM // 4
    xs = x.astype(jnp.float32).reshape(x.shape[:-1] + (2, 2, nf))
    x1, x2 = xs[..., 0, :], xs[..., 1, :]
    c = cos.reshape(L, 1, 1, 2, nf)
    s = sin.reshape(L, 1, 1, 2, nf)
    out = jnp.stack([x1 * c - x2 * s, x1 * s + x2 * c], axis=-2)
    return out.reshape(x.shape).astype(x.dtype)


def hyena_filter_freq(L, w1, b1, w2, b2, w3, b3):
    t = jnp.linspace(0.0, 1.0, L, dtype=jnp.float32)
    pos = jnp.arange(L, dtype=jnp.float32)
    f = jnp.linspace(1e-4, HY_BANDS - 1, HY_BANDS, dtype=jnp.float32)
    ang = (2.0 * math.pi / L) * pos[:, None] * f
    z = jnp.concatenate([t[:, None], jnp.cos(ang), jnp.sin(ang)], axis=-1)
    h = jnp.sin(z @ w1 + b1)
    h = jnp.sin(h @ w2 + b2)
    h = (h @ w3 + b3).astype(jnp.float32).reshape(L, HY_ORDER, 2, HY_WIDTH)
    deltas = jnp.abs(jnp.linspace(math.log(HY_DECAY_TARGET) / HY_FAST_DECAY,
                                  math.log(HY_DECAY_TARGET) / HY_SLOW_DECAY, HY_WIDTH, dtype=jnp.float32))
    h = h * jnp.exp(-t[:, None, None, None] * deltas)
    h_fwd, h_bwd = h[:, :, 0], h[:, :, 1]
    taps = jnp.concatenate([h_fwd, jnp.zeros((1, HY_ORDER, HY_WIDTH), jnp.float32),
                            jnp.flip(h_bwd[1:], axis=0)], axis=0)
    taps = taps / jnp.sum(jnp.abs(taps), axis=0, keepdims=True)
    return jnp.fft.rfft(taps, axis=0)


def fft_long_conv(u, filt_f):
    L = u.shape[1]
    uf = jnp.fft.rfft(u.astype(jnp.float32), n=2 * L, axis=1)
    return jnp.fft.irfft(uf * filt_f, n=2 * L, axis=1)[:, :L].astype(u.dtype)


def hyena_mixer(p, conv_w, conv_b, w1, b1, w2, b2, w3, b3, skip):
    L = p.shape[1]
    v, g1, g2 = jnp.split(short_conv(p, conv_w, conv_b), HY_ORDER + 1, axis=-1)
    filt_f = hyena_filter_freq(L, w1, b1, w2, b2, w3, b3)
    z = v
    for o, gate in enumerate((g1, g2)):
        z = gate * (fft_long_conv(z, filt_f[:, o]) + skip[o] * z)
    return z


def diff_attention(q, k, v, lam):
    s = jnp.einsum('bhmqd,bhmkd->bhmqk', q, k).astype(jnp.float32) * (HEAD_DIM ** -0.5)
    p = jax.nn.softmax(s, axis=-1)
    w = p[:, :, 0] - lam * p[:, :, 1]
    return jnp.einsum('bhqk,bhkv->bhqv', w.astype(v.dtype), v)


def diff_attention_blocked(q, k, v, lam):
    B, H, _, L, d = q.shape
    nb = L // Q_BLOCK
    qb = jnp.moveaxis(q.reshape(B, H, 2, nb, Q_BLOCK, d), 3, 0)
    out = lax.map(lambda blk: diff_attention(blk, k, v, lam), qb)
    return jnp.moveaxis(out, 0, 2).reshape(B, H, L, v.shape[-1])


def to_qk(a):
    return a.transpose(0, 2, 3, 1, 4)


def to_v(a):
    return a.transpose(0, 2, 1, 3)


def mlstm_zero_state(B):
    return (jnp.zeros((B, ML_HEADS, HEAD_DIM, HEAD_DIM), jnp.float32),
            jnp.zeros((B, ML_HEADS, HEAD_DIM), jnp.float32),
            jnp.zeros((B, ML_HEADS), jnp.float32))


def mlstm_scan(q, k, v, ig, lf, state):
    B, H, L, d = q.shape
    nc = L // ML_CHUNK

    def chunks(a):
        return jnp.moveaxis(a.reshape(a.shape[:2] + (nc, ML_CHUNK) + a.shape[3:]), 2, 0)

    tril = jnp.tril(jnp.ones((ML_CHUNK, ML_CHUNK), dtype=bool))

    def step(carry, xs):
        C, n, m = carry
        qc, kc, vc, ic, fc = xs
        b = jnp.cumsum(fc, axis=-1)
        log_d = jnp.where(tril, b[..., :, None] - b[..., None, :] + ic[..., None, :], -jnp.inf)
        log_inter = b + m[..., None]
        m_t = jnp.maximum(log_inter, jnp.max(log_d, axis=-1))
        d_w = jnp.exp(log_d - m_t[..., None])
        inter_w = jnp.exp(log_inter - m_t)
        s = jnp.einsum('bhjk,bhik->bhji', qc, kc) * d_w
        num = jnp.einsum('bhji,bhie->bhje', s, vc) + inter_w[..., None] * jnp.einsum('bhek,bhjk->bhje', C, qc)
        den = jnp.sum(s, axis=-1) + inter_w * jnp.einsum('bhk,bhjk->bhj', n, qc)
        h = num / jnp.maximum(jnp.abs(den), jnp.exp(-m_t))[..., None]
        m_new = m_t[..., -1]
        w = jnp.exp(b[..., -1:] - b + ic - m_new[..., None])
        decay = jnp.exp(b[..., -1] + m - m_new)
        C_new = decay[..., None, None] * C + jnp.einsum('bhi,bhie,bhik->bhek', w, vc, kc)
        n_new = decay[..., None] * n + jnp.einsum('bhi,bhik->bhk', w, kc)
        return (C_new, n_new, m_new), h

    state, h = lax.scan(step, state, (chunks(q), chunks(k), chunks(v), chunks(ig), chunks(lf)))
    return jnp.moveaxis(h, 0, 2).reshape(B, H, L, d), state


def mlstm_bidir(q, k, v, i_f, f_f, i_b, f_b, state_f, state_b):
    h_f, st_f = mlstm_scan(q, k, v, i_f, f_f, state_f)
    flip = lambda a: jnp.flip(a, axis=2)
    h_b, st_b = mlstm_scan(flip(q), flip(k), flip(v), flip(i_b), flip(f_b), state_b)
    return h_f + flip(h_b), st_f, st_b


def mlstm_inputs(mq, mk, mv, mg, conv_w, conv_b, gate_b):
    B, L, _ = mq.shape
    qk = jax.nn.silu(short_conv(jnp.concatenate([mq, mk], axis=-1), conv_w, conv_b))
    heads = lambda a: a.reshape(B, L, ML_HEADS, HEAD_DIM).transpose(0, 2, 1, 3).astype(jnp.float32)
    q = heads(qk[..., :ML_WIDTH])
    k = heads(qk[..., ML_WIDTH:]) * (HEAD_DIM ** -0.5)
    v = heads(mv)
    g = (mg + gate_b).astype(jnp.float32).reshape(B, L, 4, ML_HEADS).transpose(2, 0, 3, 1)
    return (q, k, v, g[0], jax.nn.log_sigmoid(g[1]), g[2], jax.nn.log_sigmoid(g[3]))


def merge_groups(y_hy, y_da, y_ml, gain, lam_init):
    return jnp.concatenate([group_rms_norm(y_hy, HY_GROUP_DIM),
                            group_rms_norm(y_da, DA_VDIM) * (1.0 - lam_init),
                            group_rms_norm(y_ml, HEAD_DIM)], axis=-1) * gain


def conv_ffn(h, up, cw, cb, down):
    u = short_conv(h @ up, cw, cb)
    a, g = jnp.split(u, 2, axis=-1)
    return (jax.nn.silu(g) * a) @ down


def layer(x, ctx, mod_x, mod_c, lam_init, rope_cos, rope_sin, w_in, w_out, hy_conv_w, hy_conv_b,
          hy_w1, hy_b1, hy_w2, hy_b2, hy_w3, hy_b3, hy_skip, da_lambda, ml_conv_w, ml_conv_b,
          ml_gate_b, mix_norm_w, ffn_up, ffn_conv_w, ffn_conv_b, ffn_down, update_ctx):
    B, L, _ = x.shape
    Lc = ctx.shape[1]
    sh1, sc1, g1, sh2, sc2, g2 = mod_x
    csh1, csc1, cg1, csh2, csc2, cg2 = mod_c
    px = jnp.split(modulate(rms_norm(x), sh1, sc1) @ w_in, COL_SPLITS, axis=-1)
    pc = jnp.split(modulate(rms_norm(ctx), csh1, csc1) @ w_in, COL_SPLITS, axis=-1)
    hy_params = (hy_conv_w, hy_conv_b, hy_w1, hy_b1, hy_w2, hy_b2, hy_w3, hy_b3, hy_skip)

    y_hy_x = hyena_mixer(px[0], *hy_params)

    lam = (jnp.exp(jnp.sum(da_lambda[0] * da_lambda[1])) -
           jnp.exp(jnp.sum(da_lambda[2] * da_lambda[3]))).astype(jnp.float32) + lam_init
    q_x = apply_axial_rope(px[1].reshape(B, L, DA_HEADS, 2, HEAD_DIM), rope_cos, rope_sin)
    k_x = apply_axial_rope(px[2].reshape(B, L, DA_HEADS, 2, HEAD_DIM), rope_cos, rope_sin)
    v_x = px[3].reshape(B, L, DA_HEADS, DA_VDIM)
    k_c = pc[2].reshape(B, Lc, DA_HEADS, 2, HEAD_DIM)
    v_c = pc[3].reshape(B, Lc, DA_HEADS, DA_VDIM)
    k_all = to_qk(jnp.concatenate([k_c, k_x], axis=1))
    v_all = to_v(jnp.concatenate([v_c, v_x], axis=1))
    y_da_x = heads_merge(diff_attention_blocked(to_qk(q_x), k_all, v_all, lam))

    ml_c = mlstm_inputs(pc[4], pc[5], pc[6], pc[8], ml_conv_w, ml_conv_b, ml_gate_b)
    ml_x = mlstm_inputs(px[4], px[5], px[6], px[8], ml_conv_w, ml_conv_b, ml_gate_b)
    h_c, st_f, st_b = mlstm_bidir(*ml_c, mlstm_zero_state(B), mlstm_zero_state(B))
    h_x, _, _ = mlstm_bidir(*ml_x, st_f, st_b)
    y_ml_x = heads_merge(h_x).astype(x.dtype) * jax.nn.sigmoid(px[7])

    x = x + g1 * (merge_groups(y_hy_x, y_da_x, y_ml_x, mix_norm_w, lam_init) @ w_out)
    x = x + g2 * conv_ffn(modulate(rms_norm(x), sh2, sc2), ffn_up, ffn_conv_w, ffn_conv_b, ffn_down)

    if update_ctx:
        y_hy_c = hyena_mixer(pc[0], *hy_params)
        q_c = to_qk(pc[1].reshape(B, Lc, DA_HEADS, 2, HEAD_DIM))
        y_da_c = heads_merge(diff_attention(q_c, to_qk(k_c), to_v(v_c), lam))
        y_ml_c = heads_merge(h_c).astype(ctx.dtype) * jax.nn.sigmoid(pc[7])
        ctx = ctx + cg1 * (merge_groups(y_hy_c, y_da_c, y_ml_c, mix_norm_w, lam_init) @ w_out)
        ctx = ctx + cg2 * conv_ffn(modulate(rms_norm(ctx), csh2, csc2), ffn_up, ffn_conv_w, ffn_conv_b, ffn_down)
    return x, ctx


def setup_inputs(seed: int = 0) -> dict:
    key = jax.random.key(seed)
    ks = iter(jax.random.split(key, 40))
    nrm = lambda shape, scale: jax.random.normal(next(ks), shape, jnp.float32) * scale
    fgate = jnp.linspace(3.0, 6.0, ML_HEADS, dtype=jnp.float32)
    ml_gate_b = jnp.concatenate([nrm((DEPTH, ML_HEADS), 0.1), fgate + nrm((DEPTH, ML_HEADS), 0.1),
                                 nrm((DEPTH, ML_HEADS), 0.1), fgate + nrm((DEPTH, ML_HEADS), 0.1)], axis=-1)
    return {
        'x': nrm((BATCH, SEQ, D_MODEL), 1.0),
        'c': nrm((BATCH, D_MODEL), 1.0),
        'ctx': nrm((BATCH, CTX_LEN, D_MODEL), 1.0),
        'c_ctx': nrm((D_MODEL,), 1.0),
        'ada_w': nrm((DEPTH, D_MODEL, 6 * D_MODEL), 0.5 * D_MODEL ** -0.5),
        'ada_b': nrm((DEPTH, 6 * D_MODEL), 0.02),
        'w_in': nrm((DEPTH, D_MODEL, IN_WIDTH), D_MODEL ** -0.5),
        'w_out': nrm((DEPTH, MIX_WIDTH, D_MODEL), MIX_WIDTH ** -0.5),
        'hy_conv_w': nrm((DEPTH, SHORT_CONV, (HY_ORDER + 1) * HY_WIDTH), SHORT_CONV ** -0.5),
        'hy_conv_b': nrm((DEPTH, (HY_ORDER + 1) * HY_WIDTH), 0.02),
        'hy_w1': nrm((DEPTH, HY_POS_DIM, HY_FFN), 1.0),
        'hy_b1': nrm((DEPTH, HY_FFN), 0.1),
        'hy_w2': nrm((DEPTH, HY_FFN, HY_FFN), HY_FFN ** -0.5),
        'hy_b2': nrm((DEPTH, HY_FFN), 0.1),
        'hy_w3': nrm((DEPTH, HY_FFN, HY_ORDER * 2 * HY_WIDTH), HY_FFN ** -0.5),
        'hy_b3': nrm((DEPTH, HY_ORDER * 2 * HY_WIDTH), 0.02),
        'hy_skip': nrm((DEPTH, HY_ORDER, HY_WIDTH), 0.5),
        'da_lambda': nrm((DEPTH, 4, HEAD_DIM), 0.1),
        'ml_conv_w': nrm((DEPTH, SHORT_CONV, 2 * ML_WIDTH), SHORT_CONV ** -0.5),
        'ml_conv_b': nrm((DEPTH, 2 * ML_WIDTH), 0.02),
        'ml_gate_b': ml_gate_b,
        'mix_norm_w': 1.0 + nrm((DEPTH, MIX_WIDTH), 0.02),
        'ffn_up': nrm((DEPTH, D_MODEL, 2 * D_FF), D_MODEL ** -0.5),
        'ffn_conv_w': nrm((DEPTH, SHORT_CONV, 2 * D_FF), SHORT_CONV ** -0.5),
        'ffn_conv_b': nrm((DEPTH, 2 * D_FF), 0.02),
        'ffn_down': nrm((DEPTH, D_FF, D_MODEL), D_FF ** -0.5),
        'final_norm_w': 1.0 + nrm((D_MODEL,), 0.02),
    }


def reference(x, c, ctx, c_ctx, ada_w, ada_b, w_in, w_out, hy_conv_w, hy_conv_b, hy_w1, hy_b1,
              hy_w2, hy_b2, hy_w3, hy_b3, hy_skip, da_lambda, ml_conv_w, ml_conv_b, ml_gate_b,
              mix_norm_w, ffn_up, ffn_conv_w, ffn_conv_b, ffn_down, final_norm_w):
    rope_cos, rope_sin = axial_rope_tables(x.shape[1])
    for l in range(DEPTH):
        mod_x = [m[:, None, :] for m in jnp.split(jax.nn.silu(c) @ ada_w[l] + ada_b[l], 6, axis=-1)]
        mod_c = jnp.split(jax.nn.silu(c_ctx) @ ada_w[l] + ada_b[l], 6, axis=-1)
        lam_init = 0.8 - 0.6 * math.exp(-0.3 * l)
        x, ctx = layer(x, ctx, mod_x, mod_c, lam_init, rope_cos, rope_sin, w_in[l], w_out[l],
                       hy_conv_w[l], hy_conv_b[l], hy_w1[l], hy_b1[l], hy_w2[l], hy_b2[l], hy_w3[l],
                       hy_b3[l], hy_skip[l], da_lambda[l], ml_conv_w[l], ml_conv_b[l], ml_gate_b[l],
                       mix_norm_w[l], ffn_up[l], ffn_conv_w[l], ffn_conv_b[l], ffn_down[l],
                       update_ctx=(l < DEPTH - 1))
    return rms_norm(x) * final_norm_w
```

```python
import functools
import math

import numpy as np
import jax
import jax.numpy as jnp
from jax import lax
from jax.experimental import pallas as pl
from jax.experimental.pallas import tpu as pltpu

F32 = jnp.float32
BF16 = jnp.bfloat16
HI = lax.Precision.HIGHEST

EPS = 1e-6
HEAD_DIM = 64
GRID_W = 64
ROPE_BASE = 10000.0
HY_BANDS = 16
HY_POS_PAD = 64
HY_DECAY_TARGET = 1e-2
HY_FAST_DECAY = 0.3
HY_SLOW_DECAY = 1.5
LANES = 128
SUBLANES = 8
VMEM_LIMIT = 56 * 1024 * 1024

FFT_N1 = 64
FFT_N2 = 128
ML_T = 256
TQ = 256
TK = 512
TM = 512
FFN_TN = 256


def _cparams(*sem):
    return pltpu.CompilerParams(dimension_semantics=sem, vmem_limit_bytes=VMEM_LIMIT)


def _const_spec(shape):
    n = len(shape)
    return pl.BlockSpec(shape, lambda *_: (0,) * n, pipeline_mode=pl.Buffered(1))


def _silu(x):
    return x / (1.0 + jnp.exp(-x))


def _sigmoid(x):
    return 1.0 / (1.0 + jnp.exp(-x))


def _log_sigmoid(x):
    return jnp.minimum(x, 0.0) - jnp.log(1.0 + jnp.exp(-jnp.abs(x)))


def _rms(x):
    return x * lax.rsqrt(jnp.mean(x * x, axis=-1, keepdims=True) + EPS)


def _group_rms_2x64(y):
    lo = lax.broadcasted_iota(jnp.int32, y.shape, 1) < 64
    y2 = y * y
    s_lo = jnp.sum(jnp.where(lo, y2, 0.0), axis=-1, keepdims=True)
    s_hi = jnp.sum(jnp.where(lo, 0.0, y2), axis=-1, keepdims=True)
    r = jnp.where(lo, lax.rsqrt(s_lo * (1.0 / 64) + EPS), lax.rsqrt(s_hi * (1.0 / 64) + EPS))
    return y * r


def _conv3(x, w, b):
    L = x.shape[0]
    row = lax.broadcasted_iota(jnp.int32, x.shape, 0)
    xm = jnp.where(row == 0, 0.0, pltpu.roll(x, 1, axis=0))
    xp = jnp.where(row == L - 1, 0.0, pltpu.roll(x, L - 1, axis=0))
    return xm * w[0:1] + x * w[1:2] + xp * w[2:3] + b


def _adaln_kernel(c_ref, w_ref, b_ref, o_ref):
    a = _silu(c_ref[...])
    o_ref[0] = jnp.dot(a, w_ref[0], precision=HI, preferred_element_type=F32) + b_ref[0]


def _adaln(cc, ada_w, ada_b):
    depth, d, n6 = ada_w.shape
    tn = 1536
    return pl.pallas_call(
        _adaln_kernel,
        grid=(depth, n6 // tn),
        in_specs=[pl.BlockSpec(cc.shape, lambda l, j: (0, 0)),
                  pl.BlockSpec((1, d, tn), lambda l, j: (l, 0, j)),
                  pl.BlockSpec((1, 1, tn), lambda l, j: (l, 0, j))],
        out_specs=pl.BlockSpec((1, cc.shape[0], tn), lambda l, j: (l, 0, j)),
        out_shape=jax.ShapeDtypeStruct((depth, cc.shape[0], n6), F32),
        compiler_params=_cparams("parallel", "parallel"),
        name="adaln",
    )(cc, ada_w, ada_b.reshape(depth, 1, n6))


_C_HY, _C_Q, _C_QS, _C_K, _C_KS, _C_V, _C_MQK, _C_MV, _C_MO, _C_G, _C_END = (
    0, 768, 1280, 1792, 2304, 2816, 3328, 3840, 4096, 4352, 4608)


def _inproj_kernel(x_ref, sh_ref, sc_ref, w_ref, cos_ref, sin_ref,
                   hy_ref, q_ref, k_ref, v_ref, mqk_ref, mv_ref, mo_ref, g_ref):
    h = (_rms(x_ref[0]) * (1.0 + sc_ref[0]) + sh_ref[0]).astype(BF16)

    def proj(lo, hi):
        return jnp.dot(h, w_ref[:, lo:hi], preferred_element_type=F32)

    hy_ref[0] = proj(_C_HY, _C_Q)
    cos = cos_ref[...]
    sin = sin_ref[...]
    for j in range(4):
        o = LANES * j
        q = proj(_C_Q + o, _C_Q + o + LANES) * cos + proj(_C_QS + o, _C_QS + o + LANES) * sin
        q_ref[0, :, o:o + LANES] = (q * (HEAD_DIM ** -0.5)).astype(BF16)
        k = proj(_C_K + o, _C_K + o + LANES) * cos + proj(_C_KS + o, _C_KS + o + LANES) * sin
        k_ref[0, :, o:o + LANES] = k.astype(BF16)
    v_ref[0] = proj(_C_V, _C_MQK).astype(BF16)
    mqk_ref[0] = proj(_C_MQK, _C_MV)
    mv_ref[0] = proj(_C_MV, _C_MO)
    mo_ref[0] = proj(_C_MO, _C_G)
    g_ref[0] = proj(_C_G, _C_END)


def _inproj(x, shift, scale, w_ext, cos, sin):
    B, L, D = x.shape
    tm = min(TM, L)
    tok = lambda w: pl.BlockSpec((1, tm, w), lambda b, i: (b, i, 0))
    mod = pl.BlockSpec((1, 1, D), lambda b, i: (b, 0, 0))
    tab = pl.BlockSpec((tm, LANES), lambda b, i: (i, 0))
    widths = (768, 512, 512, 512, 512, 256, 256, 256)
    dtypes = (F32, BF16, BF16, BF16, F32, F32, F32, F32)
    return pl.pallas_call(
        _inproj_kernel,
        grid=(B, L // tm),
        in_specs=[tok(D), mod, mod, _const_spec(w_ext.shape), tab, tab],
        out_specs=[tok(w) for w in widths],
        out_shape=[jax.ShapeDtypeStruct((B, L, w), dt) for w, dt in zip(widths, dtypes)],
        compiler_params=_cparams("parallel", "parallel"),
        name="inproj",
    )(x, shift, scale, w_ext, cos, sin)


def _dft_tables(L):
    n1, n2, g = FFT_N1, FFT_N2, SUBLANES
    N = n1 * n2
    assert N == 2 * L
    k1 = np.arange(n1)[:, None]
    m1 = np.arange(n1 // 2)[None, :]
    th = 2 * np.pi * ((k1 * m1) % n1) / n1
    f = np.stack([np.cos(th), -np.sin(th)], axis=1)
    eye = np.eye(g)
    m1f = np.einsum('krn,gh->krgnh', f, eye).reshape(2 * n1 * g, (n1 // 2) * g)
    m1i = np.einsum('krn,gh->ngkrh', f, eye).reshape((n1 // 2) * g, 2 * n1 * g)
    a = np.arange(n2)
    th2 = 2 * np.pi * ((a[:, None] * a[None, :]) % n2) / n2
    fr, fi = np.cos(th2), -np.sin(th2)
    tw = 2 * np.pi * ((np.arange(n1)[:, None] * a[None, :]) % N) / N
    tr, ti = np.cos(tw)[:, None, :], -np.sin(tw)[:, None, :]
    gr = fr[None] * tr - fi[None] * ti
    gi = fr[None] * ti + fi[None] * tr
    gf = np.concatenate([np.concatenate([gr, -gi], axis=2),
                         np.concatenate([gi, gr], axis=2)], axis=1)
    gb = np.transpose(gf, (0, 2, 1))
    cvt = lambda m: jnp.asarray(m.astype(np.float32)).astype(BF16)
    return cvt(m1f), cvt(gf), cvt(gb), cvt(m1i)


def _fft_block_fwd(src3_ref, a_ref, m1f_ref):
    nh, n2, c = src3_ref.shape
    m1f = m1f_ref[...]
    for g in range(n2 // SUBLANES):
        sl = slice(SUBLANES * g, SUBLANES * (g + 1))
        xg = src3_ref[:, sl, :].reshape(nh * SUBLANES, c).astype(BF16)
        out = jnp.dot(m1f, xg, preferred_element_type=F32)
        a_ref[:, sl, :] = out.reshape(4 * nh, SUBLANES, c)


def _fft_block_inv(a_ref, dst3_ref, m1i_ref):
    nh, n2, c = dst3_ref.shape
    m1i = m1i_ref[...]
    for g in range(n2 // SUBLANES):
        sl = slice(SUBLANES * g, SUBLANES * (g + 1))
        ag = a_ref[:, sl, :].reshape(4 * nh * SUBLANES, c).astype(BF16)
        out = jnp.dot(m1i, ag, preferred_element_type=F32)
        dst3_ref[:, sl, :] = out.reshape(nh, SUBLANES, c)


def _hy_mlp(z_ref, w1_ref, b1_ref, w2_ref, b2_ref):
    h = jnp.sin(jnp.dot(z_ref[...], w1_ref[...], precision=HI, preferred_element_type=F32) + b1_ref[...])
    return jnp.sin(jnp.dot(h, w2_ref[...], precision=HI, preferred_element_type=F32) + b2_ref[...])


def _hy_taps(h2, w3_ref, b3_ref, delta_ref, zero_first):
    L = h2.shape[0]
    h = jnp.dot(h2, w3_ref[...], precision=HI, preferred_element_type=F32) + b3_ref[...]
    row = lax.broadcasted_iota(jnp.int32, h.shape, 0)
    t = row.astype(F32) * (1.0 / (L - 1))
    h = h * jnp.exp(-t * delta_ref[...])
    if zero_first:
        h = jnp.where(row == 0, 0.0, h)
    return h


def _hy_filter_kernel(z_ref, w1_ref, b1_ref, w2_ref, b2_ref, w3f_ref, b3f_ref, w3b_ref, b3b_ref,
                      delta_ref, m1f_ref, gf_ref, kf_ref, src_ref, af_ref, ab_ref):
    h2 = _hy_mlp(z_ref, w1_ref, b1_ref, w2_ref, b2_ref)
    hf = _hy_taps(h2, w3f_ref, b3f_ref, delta_ref, False)
    hb = _hy_taps(h2, w3b_ref, b3b_ref, delta_ref, True)
    L, c = hf.shape
    n_total = 2.0 * L
    inv = 1.0 / ((jnp.sum(jnp.abs(hf), axis=0, keepdims=True)
                  + jnp.sum(jnp.abs(hb), axis=0, keepdims=True)) * n_total)
    nh, n2 = src_ref.shape[0], src_ref.shape[1]
    src_ref[...] = (hf * inv).reshape(nh, n2, c)
    _fft_block_fwd(src_ref, af_ref, m1f_ref)
    src_ref[...] = (hb * inv).reshape(nh, n2, c)
    _fft_block_fwd(src_ref, ab_ref, m1f_ref)

    def body(k1, carry):
        sl = pl.ds(pl.multiple_of(2 * k1, 2), 2)
        g = gf_ref[k1]
        xf = jnp.dot(g, af_ref[sl].reshape(2 * n2, c).astype(BF16), preferred_element_type=F32)
        xb = jnp.dot(g, ab_ref[sl].reshape(2 * n2, c).astype(BF16), preferred_element_type=F32)
        kr = xf[:n2] + xb[:n2]
        ki = xf[n2:] - xb[n2:]
        kf_ref[0, sl] = jnp.concatenate([kr, ki], axis=0).reshape(2, n2, c).astype(BF16)
        return carry

    lax.fori_loop(0, gf_ref.shape[0], body, 0)


def _hy_filter(zfeat, w1, b1, w2, b2, w3, b3, delta, m1f, gf):
    L = zfeat.shape[0]
    C = delta.shape[1]
    cb = LANES
    ncb = C // cb
    nh, n2 = FFT_N1 // 2, FFT_N2
    full = lambda a: pl.BlockSpec(a.shape, lambda o, j: (0,) * a.ndim)
    return pl.pallas_call(
        _hy_filter_kernel,
        grid=(2, ncb),
        in_specs=[full(zfeat), full(w1), full(b1), full(w2), full(b2),
                  pl.BlockSpec((w3.shape[0], cb), lambda o, j: (0, o * 2 * ncb + j)),
                  pl.BlockSpec((1, cb), lambda o, j: (0, o * 2 * ncb + j)),
                  pl.BlockSpec((w3.shape[0], cb), lambda o, j: (0, o * 2 * ncb + ncb + j)),
                  pl.BlockSpec((1, cb), lambda o, j: (0, o * 2 * ncb + ncb + j)),
                  pl.BlockSpec((1, cb), lambda o, j: (0, j)),
                  _const_spec(m1f.shape), _const_spec(gf.shape)],
        out_specs=pl.BlockSpec((1, 4 * nh, n2, cb), lambda o, j: (o, 0, 0, j)),
        out_shape=jax.ShapeDtypeStruct((2, 4 * nh, n2, C), BF16),
        scratch_shapes=[pltpu.VMEM((nh, n2, cb), F32), pltpu.VMEM((4 * nh, n2, cb), F32),
                        pltpu.VMEM((4 * nh, n2, cb), F32)],
        compiler_params=_cparams("arbitrary", "arbitrary"),
        name="hy_filter",
    )(zfeat, w1, b1, w2, b2, w3, b3, w3, b3, delta, m1f, gf)


def _hy_conv_kernel(conv_z, norm_out, z_ref, g_ref, cwz_ref, cbz_ref, cwg_ref, cbg_ref, skip_ref,
                    kf_ref, m1f_ref, gf_ref, gb_ref, m1i_ref, o_ref, zs_ref, ys_ref, a_ref):
    nh, n2, c = zs_ref.shape
    z = z_ref[0]
    if conv_z:
        z = _conv3(z, cwz_ref[...], cbz_ref[...])
    zs_ref[...] = z.reshape(nh, n2, c)
    _fft_block_fwd(zs_ref, a_ref, m1f_ref)

    def body(k1, carry):
        sl = pl.ds(pl.multiple_of(2 * k1, 2), 2)
        x = jnp.dot(gf_ref[k1], a_ref[sl].reshape(2 * n2, c).astype(BF16), preferred_element_type=F32)
        kf = kf_ref[0, sl].astype(F32)
        xr, xi, kr, ki = x[:n2], x[n2:], kf[0], kf[1]
        y = jnp.concatenate([xr * kr - xi * ki, xr * ki + xi * kr], axis=0).astype(BF16)
        a_ref[sl] = jnp.dot(gb_ref[k1], y, preferred_element_type=F32).reshape(2, n2, c)
        return carry

    lax.fori_loop(0, gf_ref.shape[0], body, 0)
    _fft_block_inv(a_ref, ys_ref, m1i_ref)
    y = ys_ref[...].reshape(nh * n2, c)
    z = zs_ref[...].reshape(nh * n2, c)
    gate = _conv3(g_ref[0], cwg_ref[...], cbg_ref[...])
    out = gate * (y + skip_ref[...] * z)
    if norm_out:
        out = _group_rms_2x64(out)
    o_ref[0] = out


def _hy_conv(order, z, zcol, p_hy, conv_w, conv_b, skip, kf, tables):
    m1f, gf, gb, m1i = tables
    B, L, _ = p_hy.shape
    C = skip.shape[1]
    cb = LANES
    ncb = C // cb
    nh, n2 = FFT_N1 // 2, FFT_N2
    gcol = (order + 1) * ncb
    cw = lambda base: pl.BlockSpec((3, cb), lambda j, b: (0, base + j))
    cbs = lambda base: pl.BlockSpec((1, cb), lambda j, b: (0, base + j))
    return pl.pallas_call(
        functools.partial(_hy_conv_kernel, order == 0, order == 1),
        grid=(ncb, B),
        in_specs=[pl.BlockSpec((1, L, cb), lambda j, b: (b, 0, zcol + j)),
                  pl.BlockSpec((1, L, cb), lambda j, b: (b, 0, gcol + j)),
                  cw(0), cbs(0), cw(gcol), cbs(gcol),
                  pl.BlockSpec((1, cb), lambda j, b: (0, j)),
                  pl.BlockSpec((1, 4 * nh, n2, cb), lambda j, b: (order, 0, 0, j)),
                  _const_spec(m1f.shape), _const_spec(gf.shape), _const_spec(gb.shape),
                  _const_spec(m1i.shape)],
        out_specs=pl.BlockSpec((1, L, cb), lambda j, b: (b, 0, j)),
        out_shape=jax.ShapeDtypeStruct((B, L, C), F32),
        scratch_shapes=[pltpu.VMEM((nh, n2, cb), F32), pltpu.VMEM((nh, n2, cb), F32),
                        pltpu.VMEM((4 * nh, n2, cb), F32)],
        compiler_params=_cparams("arbitrary", "arbitrary"),
        name=f"hy_conv{order}",
    )(z, p_hy, conv_w, conv_b, conv_w, conv_b, skip[order:order + 1], kf, m1f, gf, gb, m1i)


def _hy_ctx_kernel(p_ref, z_ref, w1_ref, b1_ref, w2_ref, b2_ref, w3_ref, b3_ref, delta_ref,
                   cw_ref, cb_ref, skip_ref, fd_ref, fi_ref, o_ref):
    lc = p_ref.shape[1]
    C = skip_ref.shape[1]
    h2 = _hy_mlp(z_ref, w1_ref, b1_ref, w2_ref, b2_ref)
    u = _conv3(p_ref[0], cw_ref[...], cb_ref[...])
    z = u[:, 0:C]
    fd = fd_ref[...]
    fi = fi_ref[...]
    nf = fd.shape[0] // 2
    dot = lambda a, b: jnp.dot(a, b, precision=HI, preferred_element_type=F32)
    for o in range(2):
        base = 2 * o * C
        hf = _hy_taps(h2, w3_ref.at[:, base:base + C], b3_ref.at[:, base:base + C], delta_ref, False)
        hb = _hy_taps(h2, w3_ref.at[:, base + C:base + 2 * C], b3_ref.at[:, base + C:base + 2 * C],
                      delta_ref, True)
        inv = 1.0 / ((jnp.sum(jnp.abs(hf), axis=0, keepdims=True)
                      + jnp.sum(jnp.abs(hb), axis=0, keepdims=True)) * (2.0 * lc))
        xf = dot(fd, hf * inv)
        xb = dot(fd, hb * inv)
        kr = xf[:nf] + xb[:nf]
        ki = xf[nf:] - xb[nf:]
        x = dot(fd, z)
        xr, xi = x[:nf], x[nf:]
        y = dot(fi, jnp.concatenate([xr * kr - xi * ki, xr * ki + xi * kr], axis=0))
        z = u[:, (o + 1) * C:(o + 2) * C] * (y + skip_ref[o:o + 1, :] * z)
    for j in range(C // LANES):
        o_ref[0, :, j * LANES:(j + 1) * LANES] = _group_rms_2x64(z[:, j * LANES:(j + 1) * LANES])


def _dense_dft_tables(lc):
    n = 2 * lc
    k = np.arange(n)[:, None]
    t = np.arange(lc)[None, :]
    th = 2 * np.pi * ((k * t) % n) / n
    fd = np.concatenate([np.cos(th), -np.sin(th)], axis=0)
    fi = np.concatenate([np.cos(th).T, -np.sin(th).T], axis=1)
    return jnp.asarray(fd.astype(np.float32)), jnp.asarray(fi.astype(np.float32))


def _hy_ctx(p_hy, zfeat, w1, b1, w2, b2, w3, b3, delta, conv_w, conv_b, skip, fd, fi):
    B, lc, w = p_hy.shape
    C = skip.shape[1]
    full = lambda a: pl.BlockSpec(a.shape, lambda b: (0,) * a.ndim)
    args = (zfeat, w1, b1, w2, b2, w3, b3, delta, conv_w, conv_b, skip, fd, fi)
    return pl.pallas_call(
        _hy_ctx_kernel,
        grid=(B,),
        in_specs=[pl.BlockSpec((1, lc, w), lambda b: (b, 0, 0))] + [full(a) for a in args],
        out_specs=pl.BlockSpec((1, lc, C), lambda b: (b, 0, 0)),
        out_shape=jax.ShapeDtypeStruct((B, lc, C), F32),
        compiler_params=_cparams("parallel"),
        name="hy_ctx",
    )(p_hy, *args)


def _attn_kernel(seg_lens, lam_init, q_ref, lam_ref, *refs):
    nseg = len(seg_lens)
    k_refs, v_refs = refs[:nseg], refs[nseg:2 * nseg]
    o_ref = refs[2 * nseg]
    p_ref = refs[2 * nseg + 1]
    tq = q_ref.shape[1]
    q = q_ref[0]
    lane = lax.broadcasted_iota(jnp.int32, q.shape, 1)
    qm = (jnp.where(lane < HEAD_DIM, q, jnp.zeros_like(q)), jnp.where(lane < HEAD_DIM, jnp.zeros_like(q), q))
    dl = lam_ref[...]
    lam = (jnp.exp(jnp.sum(dl[0:1] * dl[1:2], keepdims=True))
           - jnp.exp(jnp.sum(dl[2:3] * dl[3:4], keepdims=True))) + lam_init

    chunks = []
    off = 0
    for s, n in enumerate(seg_lens):
        for st in range(0, n, TK):
            w = min(TK, n - st)
            chunks.append((s, st, off, w))
            off += w

    m_run = [jnp.full((tq, 1), -jnp.inf, F32) for _ in range(2)]
    l_run = [jnp.zeros((tq, 1), F32) for _ in range(2)]
    m_chunk = [[], []]
    for (s, st, o, w) in chunks:
        k = k_refs[s][0, st:st + w, :]
        for m in range(2):
            sc = lax.dot_general(qm[m], k, (((1,), (1,)), ((), ())), preferred_element_type=F32)
            m_new = jnp.maximum(m_run[m], jnp.max(sc, axis=-1, keepdims=True))
            p = jnp.exp(sc - m_new)
            l_run[m] = l_run[m] * jnp.exp(m_run[m] - m_new) + jnp.sum(p, axis=-1, keepdims=True)
            m_run[m] = m_new
            m_chunk[m].append(m_new)
            p_ref[m, :, o:o + w] = p

    acc = jnp.zeros((tq, v_refs[0].shape[2]), F32)
    for ci, (s, st, o, w) in enumerate(chunks):
        a0 = jnp.exp(m_chunk[0][ci] - m_run[0]) / l_run[0]
        a1 = jnp.exp(m_chunk[1][ci] - m_run[1]) / l_run[1] * lam
        wgt = p_ref[0, :, o:o + w] * a0 - p_ref[1, :, o:o + w] * a1
        acc = acc + jnp.dot(wgt.astype(BF16), v_refs[s][0, st:st + w, :], preferred_element_type=F32)
    o_ref[0] = _rms(acc) * (1.0 - lam_init)


def _diff_attn(q, ks, vs, da_lambda, lam_init):
    B, lq, w = q.shape
    H = w // LANES
    tq = min(TQ, lq)
    seg_lens = tuple(k.shape[1] for k in ks)
    seg = lambda n: pl.BlockSpec((1, n, LANES), lambda b, h, i: (b, 0, h))
    return pl.pallas_call(
        functools.partial(_attn_kernel, seg_lens, lam_init),
        grid=(B, H, lq // tq),
        in_specs=[pl.BlockSpec((1, tq, LANES), lambda b, h, i: (b, i, h)),
                  pl.BlockSpec(da_lambda.shape, lambda b, h, i: (0, 0))]
                 + [seg(n) for n in seg_lens] + [seg(n) for n in seg_lens],
        out_specs=pl.BlockSpec((1, tq, LANES), lambda b, h, i: (b, i, h)),
        out_shape=jax.ShapeDtypeStruct((B, lq, w), F32),
        scratch_shapes=[pltpu.VMEM((2, tq, sum(seg_lens)), F32)],
        compiler_params=_cparams("parallel", "parallel", "arbitrary"),
        name="diff_attn",
    )(q, da_lambda, *ks, *vs)


def _mlstm_chunk(rev, q, k, v, gc, gr, tri_lo, tri_up, ct_ref, n_ref, m_ref):
    T = q.shape[0]
    d = 2 if not rev else 6
    ii = 0 if not rev else 4
    lane = lax.broadcasted_iota(jnp.int32, (T, LANES), 1)
    lo = lane < HEAD_DIM
    lo_row = lax.broadcasted_iota(jnp.int32, (1, LANES), 1) < HEAD_DIM
    csum_c = jnp.dot(tri_up if rev else tri_lo, _log_sigmoid(gc), precision=HI, preferred_element_type=F32)
    csum_r = jnp.dot(_log_sigmoid(gr), tri_lo if rev else tri_up, precision=HI, preferred_element_type=F32)
    r_i = lax.broadcasted_iota(jnp.int32, (T, T), 0)
    c_i = lax.broadcasted_iota(jnp.int32, (T, T), 1)
    mask = (c_i >= r_i) if rev else (c_i <= r_i)
    last = 0 if rev else T - 1
    qb = q.astype(BF16)
    kb = k.astype(BF16)
    vb = v.astype(BF16)
    zero = jnp.zeros_like(qb)
    sv, den, hden, wcol, decay, interw = [], [], [], [], [], []
    m_new = []
    for j in range(2):
        bcol = csum_c[:, d + j:d + j + 1]
        brow = csum_r[d + j:d + j + 1, :]
        igrow = gr[ii + j:ii + j + 1, :]
        igcol = gc[:, ii + j:ii + j + 1]
        m_prev = m_ref[j:j + 1, 0:1]
        logd = jnp.where(mask, bcol - brow + igrow, -jnp.inf)
        mrow = jnp.maximum(bcol + m_prev, jnp.max(logd, axis=-1, keepdims=True))
        dw = jnp.exp(logd - mrow)
        iw = jnp.exp(bcol + m_prev - mrow)
        qj = jnp.where(lo, qb, zero) if j == 0 else jnp.where(lo, zero, qb)
        s = lax.dot_general(qj, kb, (((1,), (1,)), ((), ())), preferred_element_type=F32) * dw
        sv.append(jnp.dot(s.astype(BF16), vb, preferred_element_type=F32))
        qn = jnp.sum(qj.astype(F32) * n_ref[...], axis=-1, keepdims=True)
        dn = jnp.sum(s, axis=-1, keepdims=True) + iw * qn
        den.append(dn)
        hden.append(jnp.maximum(jnp.abs(dn), jnp.exp(-mrow)))
        interw.append(iw)
        mn = mrow[last:last + 1, :]
        btot = bcol[last:last + 1, :]
        wcol.append(jnp.exp(btot - bcol + igcol - mn))
        decay.append(jnp.exp(btot + m_prev - mn))
        m_new.append(mn)
    ct = ct_ref[...]
    inter = jnp.dot(qb, ct.astype(BF16), preferred_element_type=F32)
    num = jnp.where(lo, sv[0], sv[1]) + jnp.where(lo, interw[0], interw[1]) * inter
    h = num / jnp.where(lo, hden[0], hden[1])
    wpair = jnp.where(lo, wcol[0], wcol[1])
    dpair = jnp.where(lo_row, decay[0], decay[1])
    upd = jnp.dot(k.T.astype(BF16), (v * wpair).astype(BF16), preferred_element_type=F32)
    rr = lax.broadcasted_iota(jnp.int32, (LANES, LANES), 0) < HEAD_DIM
    cc = lax.broadcasted_iota(jnp.int32, (LANES, LANES), 1) < HEAD_DIM
    ct_ref[...] = dpair * ct + jnp.where(rr == cc, upd, 0.0)
    n_ref[...] = dpair * n_ref[...] + jnp.sum(k * wpair, axis=0, keepdims=True)
    m_ref[0:1, :] = jnp.broadcast_to(m_new[0], (1, LANES))
    m_ref[1:2, :] = jnp.broadcast_to(m_new[1], (1, LANES))
    return h


def _mlstm_kernel(qc_ref, kc_ref, vc_ref, oc_ref, gcc_ref, grc_ref,
                  qx_ref, kx_ref, vx_ref, ox_ref, gcx_ref, grx_ref,
                  cwq_ref, cbq_ref, cwk_ref, cbk_ref, gbc_ref, gbr_ref,
                  yc_ref, yx_ref,
                  qs_ref, ks_ref, hf_ref, hb_ref, ct_ref, n_ref, m_ref):
    T = ML_T
    lx = qx_ref.shape[1]
    nx = lx // T
    r_i = lax.broadcasted_iota(jnp.int32, (T, T), 0)
    c_i = lax.broadcasted_iota(jnp.int32, (T, T), 1)
    tri_lo = (c_i <= r_i).astype(F32)
    tri_up = (c_i >= r_i).astype(F32)
    ct_ref[...] = jnp.zeros_like(ct_ref)
    n_ref[...] = jnp.zeros_like(n_ref)
    m_ref[...] = jnp.zeros_like(m_ref)
    scale = HEAD_DIM ** -0.5
    cwq, cbq, cwk, cbk = cwq_ref[...], cbq_ref[...], cwk_ref[...], cbk_ref[...]
    gbc, gbr = gbc_ref[...], gbr_ref[...]

    qc = _silu(_conv3(qc_ref[0], cwq, cbq))
    kc = _silu(_conv3(kc_ref[0], cwk, cbk)) * scale
    vc = vc_ref[0]
    gcc = gcc_ref[0] + gbc
    grc = grc_ref[0] + gbr
    h_c = None
    for rev in (False, True):
        di = int(rev)
        h = _mlstm_chunk(rev, qc, kc, vc, gcc, grc, tri_lo, tri_up,
                         ct_ref.at[di], n_ref.at[di], m_ref.at[di])
        h_c = h if h_c is None else h_c + h
    yc_ref[0] = _group_rms_2x64(h_c * _sigmoid(oc_ref[0]))

    qs_ref[...] = _silu(_conv3(qx_ref[0], cwq, cbq))
    ks_ref[...] = _silu(_conv3(kx_ref[0], cwk, cbk)) * scale

    def body(c, carry):
        for rev in (False, True):
            di = int(rev)
            cidx = (nx - 1 - c) if rev else c
            r0 = pl.multiple_of(cidx * T, T)
            rows = pl.ds(r0, T)
            h = _mlstm_chunk(rev, qs_ref[rows, :], ks_ref[rows, :], vx_ref[0, rows, :],
                             gcx_ref[0, rows, :] + gbc, grx_ref[0, :, rows] + gbr,
                             tri_lo, tri_up, ct_ref.at[di], n_ref.at[di], m_ref.at[di])
            if rev:
                hb_ref[rows, :] = h
            else:
                hf_ref[rows, :] = h
        return carry

    lax.fori_loop(0, nx, body, 0)
    yx_ref[0] = _group_rms_2x64((hf_ref[...] + hb_ref[...]) * _sigmoid(ox_ref[0]))


def _mlstm(seg_c, seg_x, conv_w, conv_b, gate_bc, gate_br):
    B, lx, _ = seg_x[1].shape
    lc = seg_c[1].shape[1]
    assert lc == ML_T and lx % ML_T == 0

    def seg_specs(L):
        col = lambda off: pl.BlockSpec((1, L, LANES), lambda b, p, off=off: (b, 0, off + p))
        return [col(0), col(2), col(0), col(0), col(0),
                pl.BlockSpec((1, SUBLANES, L), lambda b, p: (b, p, 0))]

    def seg_args(s):
        qk, v, o, g, gt = s
        return [qk, qk, v, o, g, gt]

    wspec = lambda rows, off: pl.BlockSpec((rows, LANES), lambda b, p, off=off: (0, off + p))
    return pl.pallas_call(
        _mlstm_kernel,
        grid=(B, 2),
        in_specs=seg_specs(lc) + seg_specs(lx)
                 + [wspec(3, 0), wspec(1, 0), wspec(3, 2), wspec(1, 2), wspec(1, 0),
                    pl.BlockSpec((SUBLANES, 1), lambda b, p: (p, 0))],
        out_specs=[pl.BlockSpec((1, lc, LANES), lambda b, p: (b, 0, p)),
                   pl.BlockSpec((1, lx, LANES), lambda b, p: (b, 0, p))],
        out_shape=[jax.ShapeDtypeStruct((B, lc, 2 * LANES), F32),
                   jax.ShapeDtypeStruct((B, lx, 2 * LANES), F32)],
        scratch_shapes=[pltpu.VMEM((lx, LANES), F32), pltpu.VMEM((lx, LANES), F32),
                        pltpu.VMEM((lx, LANES), F32), pltpu.VMEM((lx, LANES), F32),
                        pltpu.VMEM((2, LANES, LANES), F32), pltpu.VMEM((2, 1, LANES), F32),
                        pltpu.VMEM((2, 2, LANES), F32)],
        compiler_params=_cparams("parallel", "arbitrary"),
        name="mlstm",
    )(*seg_args(seg_c), *seg_args(seg_x), conv_w, conv_b, conv_w, conv_b, gate_bc, gate_br)


def _outproj_kernel(x_ref, g_ref, hy_ref, da_ref, ml_ref, gain_ref, w_ref, o_ref):
    c_hy = hy_ref.shape[2]
    c_da = da_ref.shape[2]
    gain = gain_ref[...]
    acc = jnp.dot((hy_ref[0] * gain[:, 0:c_hy]).astype(BF16), w_ref[0:c_hy, :], preferred_element_type=F32)
    acc += jnp.dot((da_ref[0] * gain[:, c_hy:c_hy + c_da]).astype(BF16), w_ref[c_hy:c_hy + c_da, :],
                   preferred_element_type=F32)
    acc += jnp.dot((ml_ref[0] * gain[:, c_hy + c_da:]).astype(BF16), w_ref[c_hy + c_da:, :],
                   preferred_element_type=F32)
    o_ref[0] = x_ref[0] + g_ref[0] * acc


def _outproj(x, gate, y_hy, y_da, y_ml, gain, w_out):
    B, L, D = x.shape
    tm = min(TM, L)
    tok = lambda w: pl.BlockSpec((1, tm, w), lambda b, i: (b, i, 0))
    return pl.pallas_call(
        _outproj_kernel,
        grid=(B, L // tm),
        in_specs=[tok(D), pl.BlockSpec((1, 1, D), lambda b, i: (b, 0, 0)),
                  tok(y_hy.shape[2]), tok(y_da.shape[2]), tok(y_ml.shape[2]),
                  pl.BlockSpec(gain.shape, lambda b, i: (0, 0)), _const_spec(w_out.shape)],
        out_specs=tok(D),
        out_shape=jax.ShapeDtypeStruct((B, L, D), F32),
        compiler_params=_cparams("parallel", "parallel"),
        name="outproj",
    )(x, gate, y_hy, y_da, y_ml, gain, w_out)


def _ffn_kernel(final, x_ref, prev_ref, next_ref, sh_ref, sc_ref, g_ref, up_ref, cw_ref, cb_ref,
                down_ref, fw_ref, o_ref):
    i = pl.program_id(1)
    last = pl.num_programs(1) - 1
    tm = x_ref.shape[1]
    d_ff = down_ref.shape[0]
    x = x_ref[0]
    mod = lambda a: _rms(a) * (1.0 + sc_ref[0]) + sh_ref[0]
    hp = jnp.where(i == 0, 0.0, mod(prev_ref[0]))
    hn = jnp.where(i == last, 0.0, mod(next_ref[0]))
    h = jnp.concatenate([hp, mod(x), hn], axis=0).astype(BF16)
    ext = tm + 2 * SUBLANES

    def conv_cols(lo, hi):
        u = jnp.dot(h, up_ref[:, lo:hi], preferred_element_type=F32)
        w = cw_ref[:, lo:hi]
        c = (pltpu.roll(u, 1, axis=0) * w[0:1] + u * w[1:2] + pltpu.roll(u, ext - 1, axis=0) * w[2:3]
             + cb_ref[:, lo:hi])
        return c[SUBLANES:SUBLANES + tm]

    acc = jnp.zeros((tm, x.shape[1]), F32)
    for j in range(d_ff // FFN_TN):
        lo = j * FFN_TN
        a = conv_cols(lo, lo + FFN_TN)
        g = conv_cols(d_ff + lo, d_ff + lo + FFN_TN)
        acc = acc + jnp.dot((_silu(g) * a).astype(BF16), down_ref[lo:lo + FFN_TN, :],
                            preferred_element_type=F32)
    y = x + g_ref[0] * acc
    if final:
        y = _rms(y) * fw_ref[...]
    o_ref[0] = y


def _ffn(x, shift, scale, gate, up, conv_w, conv_b, down, final_w, final):
    B, L, D = x.shape
    tm = min(TM, L)
    nb = tm // SUBLANES
    nrow = L // SUBLANES
    tok = pl.BlockSpec((1, tm, D), lambda b, i: (b, i, 0))
    mod = pl.BlockSpec((1, 1, D), lambda b, i: (b, 0, 0))
    full = lambda a: pl.BlockSpec(a.shape, lambda b, i: (0,) * a.ndim)
    return pl.pallas_call(
        functools.partial(_ffn_kernel, final),
        grid=(B, L // tm),
        in_specs=[tok,
                  pl.BlockSpec((1, SUBLANES, D), lambda b, i: (b, jnp.maximum(i * nb - 1, 0), 0)),
                  pl.BlockSpec((1, SUBLANES, D), lambda b, i: (b, jnp.minimum((i + 1) * nb, nrow - 1), 0)),
                  mod, mod, mod, _const_spec(up.shape), full(conv_w), full(conv_b),
                  _const_spec(down.shape), full(final_w)],
        out_specs=tok,
        out_shape=jax.ShapeDtypeStruct((B, L, D), F32),
        compiler_params=_cparams("parallel", "parallel"),
        name="ffn_final" if final else "ffn",
    )(x, x, x, shift, scale, gate, up, conv_w, conv_b, down, final_w)


def _rope_tables(L):
    rows_n = L // GRID_W
    rows = jnp.repeat(jnp.arange(rows_n, dtype=F32), GRID_W)
    cols = jnp.tile(jnp.arange(GRID_W, dtype=F32), rows_n)
    nf = HEAD_DIM // 4
    inv = ROPE_BASE ** (-jnp.arange(nf, dtype=F32) / nf)
    cr, sr = jnp.cos(rows[:, None] * inv), jnp.sin(rows[:, None] * inv)
    cc, sc = jnp.cos(cols[:, None] * inv), jnp.sin(cols[:, None] * inv)
    cos64 = jnp.concatenate([cr, cr, cc, cc], axis=-1)
    sin64 = jnp.concatenate([-sr, sr, -sc, sc], axis=-1)
    return jnp.tile(cos64, (1, 2)), jnp.tile(sin64, (1, 2))


def _hy_features(L):
    t = jnp.linspace(0.0, 1.0, L, dtype=F32)
    pos = jnp.arange(L, dtype=F32)
    f = jnp.linspace(1e-4, HY_BANDS - 1, HY_BANDS, dtype=F32)
    ang = (2.0 * math.pi / L) * pos[:, None] * f
    z = jnp.concatenate([t[:, None], jnp.cos(ang), jnp.sin(ang)], axis=-1)
    return jnp.pad(z, ((0, 0), (0, HY_POS_PAD - z.shape[1])))


def _gate_layout(a):
    g = a.reshape(a.shape[:-1] + (4, 2, 2))
    g = jnp.moveaxis(g, -2, -3).reshape(a.shape[:-1] + (2, 8))
    g = jnp.pad(g, [(0, 0)] * (g.ndim - 1) + [(0, LANES - 8)])
    return g.reshape(a.shape[:-1] + (2 * LANES,))


def _gates_t(g):
    return jnp.swapaxes(jnp.concatenate([g[..., 0:8], g[..., LANES:LANES + 8]], axis=-1), 1, 2)


def kernel(x, c, ctx, c_ctx, ada_w, ada_b, w_in, w_out, hy_conv_w, hy_conv_b, hy_w1, hy_b1, hy_w2, hy_b2,
           hy_w3, hy_b3, hy_skip, da_lambda, ml_conv_w, ml_conv_b, ml_gate_b, mix_norm_w, ffn_up,
           ffn_conv_w, ffn_conv_b, ffn_down, final_norm_w):
    B, L, D = x.shape
    lc = ctx.shape[1]
    depth = ada_w.shape[0]
    hy_w = hy_skip.shape[2]
    n_hy = 3 * hy_w
    da_w = 2 * hy_w
    ml_w = hy_w

    rows = ((B + 1 + SUBLANES - 1) // SUBLANES) * SUBLANES
    cc = jnp.zeros((rows, D), F32).at[:B].set(c).at[B].set(c_ctx)
    mods = _adaln(cc, ada_w, ada_b)

    cos_x, sin_x = _rope_tables(L)
    cos_c, sin_c = jnp.ones((lc, LANES), F32), jnp.zeros((lc, LANES), F32)
    tables = _dft_tables(L)
    fd_c, fi_c = _dense_dft_tables(lc)
    zf_x, zf_c = _hy_features(L), _hy_features(lc)
    delta = jnp.abs(jnp.linspace(math.log(HY_DECAY_TARGET) / HY_FAST_DECAY,
                                 math.log(HY_DECAY_TARGET) / HY_SLOW_DECAY, hy_w, dtype=F32))[None, :]
    swap = np.arange(da_w) ^ (HEAD_DIM // 4)
    fw = final_norm_w[None, :]

    for l in range(depth):
        lam_init = 0.8 - 0.6 * math.exp(-0.3 * l)
        update_ctx = l < depth - 1
        mx = [m[:, None, :] for m in jnp.split(mods[l, :B], 6, axis=-1)]
        mc = [jnp.broadcast_to(m[None], (B, 1, D)) for m in jnp.split(mods[l, B:B + 1], 6, axis=-1)]

        w = w_in[l]
        o = n_hy
        wq, wk, wv = w[:, o:o + da_w], w[:, o + da_w:o + 2 * da_w], w[:, o + 2 * da_w:o + 3 * da_w]
        o += 3 * da_w
        wmqk, wmv, wmo = w[:, o:o + 2 * ml_w], w[:, o + 2 * ml_w:o + 3 * ml_w], w[:, o + 3 * ml_w:o + 4 * ml_w]
        wg = _gate_layout(w[:, o + 4 * ml_w:])
        w_ext = jnp.concatenate([w[:, :n_hy], wq, wq[:, swap], wk, wk[:, swap], wv, wmqk, wmv, wmo, wg],
                                axis=1).astype(BF16)
        gate_b = _gate_layout(ml_gate_b[l])
        gate_bc = gate_b[None, :]
        gate_br = _gates_t(gate_b[None, None, :])[0]

        px = _inproj(x, mx[0], mx[1], w_ext, cos_x, sin_x)
        pc = _inproj(ctx, mc[0], mc[1], w_ext, cos_c, sin_c)
        hy_x, q_x, k_x, v_x, mqk_x, mv_x, mo_x, g_x = px
        hy_c, q_c, k_c, v_c, mqk_c, mv_c, mo_c, g_c = pc

        w1p = jnp.pad(hy_w1[l], ((0, HY_POS_PAD - hy_w1.shape[1]), (0, 0)))
        b1, b2, b3 = hy_b1[l][None, :], hy_b2[l][None, :], hy_b3[l][None, :]
        cbias = hy_conv_b[l][None, :]
        kf = _hy_filter(zf_x, w1p, b1, hy_w2[l], b2, hy_w3[l], b3, delta, tables[0], tables[1])
        z1 = _hy_conv(0, hy_x, 0, hy_x, hy_conv_w[l], cbias, hy_skip[l], kf, tables)
        y_hy_x = _hy_conv(1, z1, 0, hy_x, hy_conv_w[l], cbias, hy_skip[l], kf, tables)

        y_da_x = _diff_attn(q_x, [k_c, k_x], [v_c, v_x], da_lambda[l], lam_init)

        seg_c = (mqk_c, mv_c, mo_c, g_c, _gates_t(g_c))
        seg_x = (mqk_x, mv_x, mo_x, g_x, _gates_t(g_x))
        y_ml_c, y_ml_x = _mlstm(seg_c, seg_x, ml_conv_w[l], ml_conv_b[l][None, :], gate_bc, gate_br)

        gain = mix_norm_w[l][None, :]
        wo = w_out[l].astype(BF16)
        up = ffn_up[l].astype(BF16)
        down = ffn_down[l].astype(BF16)
        fcb = ffn_conv_b[l][None, :]
        x = _outproj(x, mx[2], y_hy_x, y_da_x, y_ml_x, gain, wo)
        x = _ffn(x, mx[3], mx[4], mx[5], up, ffn_conv_w[l], fcb, down, fw, final=not update_ctx)

        if update_ctx:
            y_hy_c = _hy_ctx(hy_c, zf_c, w1p, b1, hy_w2[l], b2, hy_w3[l], b3, delta,
                             hy_conv_w[l], cbias, hy_skip[l], fd_c, fi_c)
            y_da_c = _diff_attn(q_c, [k_c], [v_c], da_lambda[l], lam_init)
            ctx = _outproj(ctx, mc[2], y_hy_c, y_da_c, y_ml_c, gain, wo)
            ctx = _ffn(ctx, mc[3], mc[4], mc[5], up, ffn_conv_w[l], fcb, down, fw, final=False)
    return x
```

```python
import functools
import math

import numpy as np
import jax
import jax.numpy as jnp
from jax import lax
from jax.experimental import pallas as pl
from jax.experimental.pallas import tpu as pltpu

F32 = jnp.float32
BF16 = jnp.bfloat16
HI = lax.Precision.HIGHEST

EPS = 1e-6
HEAD_DIM = 64
GRID_W = 64
ROPE_BASE = 10000.0
HY_BANDS = 16
HY_POS_PAD = 64
HY_DECAY_TARGET = 1e-2
HY_FAST_DECAY = 0.3
HY_SLOW_DECAY = 1.5
LANES = 128
SUBLANES = 8
VMEM_LIMIT = 56 * 1024 * 1024

FFT_N1 = 64
FFT_N2 = 128
FFT_UNROLL = 4
ML_T = 256
TQ = 512
TK = 512
TM = 512
FFN_TN = 256


def _cparams(*sem):
    return pltpu.CompilerParams(dimension_semantics=sem, vmem_limit_bytes=VMEM_LIMIT)


def _const_spec(shape):
    n = len(shape)
    return pl.BlockSpec(shape, lambda *_: (0,) * n, pipeline_mode=pl.Buffered(1))


def _silu(x):
    return x / (1.0 + jnp.exp(-x))


def _sigmoid(x):
    return 1.0 / (1.0 + jnp.exp(-x))


def _log_sigmoid(x):
    return jnp.minimum(x, 0.0) - jnp.log(1.0 + jnp.exp(-jnp.abs(x)))


def _rms(x):
    return x * lax.rsqrt(jnp.mean(x * x, axis=-1, keepdims=True) + EPS)


def _group_rms_2x64(y):
    lo = lax.broadcasted_iota(jnp.int32, y.shape, 1) < 64
    y2 = y * y
    s_lo = jnp.sum(jnp.where(lo, y2, 0.0), axis=-1, keepdims=True)
    s_hi = jnp.sum(jnp.where(lo, 0.0, y2), axis=-1, keepdims=True)
    r = jnp.where(lo, lax.rsqrt(s_lo * (1.0 / 64) + EPS), lax.rsqrt(s_hi * (1.0 / 64) + EPS))
    return y * r


def _conv3(x, w, b):
    L = x.shape[0]
    row = lax.broadcasted_iota(jnp.int32, x.shape, 0)
    xm = jnp.where(row == 0, 0.0, pltpu.roll(x, 1, axis=0))
    xp = jnp.where(row == L - 1, 0.0, pltpu.roll(x, L - 1, axis=0))
    return xm * w[0:1] + x * w[1:2] + xp * w[2:3] + b


def _adaln_kernel(c_ref, w_ref, b_ref, o_ref):
    a = _silu(c_ref[...])
    o_ref[0] = jnp.dot(a, w_ref[0], precision=HI, preferred_element_type=F32) + b_ref[0]


def _adaln(cc, ada_w, ada_b):
    depth, d, n6 = ada_w.shape
    tn = 1536
    return pl.pallas_call(
        _adaln_kernel,
        grid=(depth, n6 // tn),
        in_specs=[pl.BlockSpec(cc.shape, lambda l, j: (0, 0)),
                  pl.BlockSpec((1, d, tn), lambda l, j: (l, 0, j)),
                  pl.BlockSpec((1, 1, tn), lambda l, j: (l, 0, j))],
        out_specs=pl.BlockSpec((1, cc.shape[0], tn), lambda l, j: (l, 0, j)),
        out_shape=jax.ShapeDtypeStruct((depth, cc.shape[0], n6), F32),
        compiler_params=_cparams("parallel", "parallel"),
        name="adaln",
    )(cc, ada_w, ada_b.reshape(depth, 1, n6))


_C_HY, _C_Q, _C_QS, _C_K, _C_KS, _C_V, _C_MQK, _C_MV, _C_MO, _C_G, _C_END = (
    0, 768, 1280, 1792, 2304, 2816, 3328, 3840, 4096, 4352, 4608)


def _inproj_kernel(x_ref, sh_ref, sc_ref, w_ref, cos_ref, sin_ref,
                   hy_ref, q_ref, k_ref, v_ref, mqk_ref, mv_ref, mo_ref, g_ref):
    h = (_rms(x_ref[0]) * (1.0 + sc_ref[0]) + sh_ref[0]).astype(BF16)

    def proj(lo, hi):
        return jnp.dot(h, w_ref[:, lo:hi], preferred_element_type=F32)

    hy_ref[0] = proj(_C_HY, _C_Q)
    cos = cos_ref[...]
    sin = sin_ref[...]

    def rope(c0, c1, out_ref, scale):
        a = proj(c0, c1)
        asw = proj(c1, 2 * c1 - c0)
        for j in range((c1 - c0) // LANES):
            sl = slice(LANES * j, LANES * (j + 1))
            out_ref[0, :, sl] = ((a[:, sl] * cos + asw[:, sl] * sin) * scale).astype(BF16)

    rope(_C_Q, _C_QS, q_ref, (HEAD_DIM ** -0.5) * math.log2(math.e))
    rope(_C_K, _C_KS, k_ref, 1.0)
    v_ref[0] = proj(_C_V, _C_MQK).astype(BF16)
    mqk_ref[0] = proj(_C_MQK, _C_MV)
    mv_ref[0] = proj(_C_MV, _C_MO)
    mo_ref[0] = proj(_C_MO, _C_G)
    g_ref[0] = proj(_C_G, _C_END)


def _inproj(x, shift, scale, w_ext, cos, sin):
    B, L, D = x.shape
    tm = min(TM, L)
    tok = lambda w: pl.BlockSpec((1, tm, w), lambda b, i: (b, i, 0))
    mod = pl.BlockSpec((1, 1, D), lambda b, i: (b, 0, 0))
    tab = pl.BlockSpec((tm, LANES), lambda b, i: (i, 0))
    widths = (768, 512, 512, 512, 512, 256, 256, 256)
    dtypes = (F32, BF16, BF16, BF16, F32, F32, F32, F32)
    return pl.pallas_call(
        _inproj_kernel,
        grid=(B, L // tm),
        in_specs=[tok(D), mod, mod, _const_spec(w_ext.shape), tab, tab],
        out_specs=[tok(w) for w in widths],
        out_shape=[jax.ShapeDtypeStruct((B, L, w), dt) for w, dt in zip(widths, dtypes)],
        compiler_params=_cparams("parallel", "parallel"),
        name="inproj",
    )(x, shift, scale, w_ext, cos, sin)


def _dft_tables(L):
    n1, n2, g = FFT_N1, FFT_N2, SUBLANES
    N = n1 * n2
    assert N == 2 * L
    k1 = np.arange(n1)[:, None]
    m1 = np.arange(n1 // 2)[None, :]
    th = 2 * np.pi * ((k1 * m1) % n1) / n1
    f = np.stack([np.cos(th), -np.sin(th)], axis=1)
    eye = np.eye(g)
    m1f = np.einsum('krn,gh->krgnh', f, eye).reshape(2 * n1 * g, (n1 // 2) * g)
    m1i = np.einsum('krn,gh->ngkrh', f, eye).reshape((n1 // 2) * g, 2 * n1 * g)
    a = np.arange(n2)
    th2 = 2 * np.pi * ((a[:, None] * a[None, :]) % n2) / n2
    fr, fi = np.cos(th2), -np.sin(th2)
    tw = 2 * np.pi * ((np.arange(n1)[:, None] * a[None, :]) % N) / N
    tr, ti = np.cos(tw)[:, None, :], -np.sin(tw)[:, None, :]
    gr = fr[None] * tr - fi[None] * ti
    gi = fr[None] * ti + fi[None] * tr
    gf = np.concatenate([np.concatenate([gr, -gi], axis=2),
                         np.concatenate([gi, gr], axis=2)], axis=1)
    gb = np.transpose(gf, (0, 2, 1))
    cvt = lambda m: jnp.asarray(m.astype(np.float32)).astype(BF16)
    return cvt(m1f), cvt(gf), cvt(gb), cvt(m1i)


def _fft_block_fwd(src3_ref, a_ref, m1f_ref):
    nh, n2, c = src3_ref.shape
    m1f = m1f_ref[...]
    group = lambda g: slice(SUBLANES * g, SUBLANES * (g + 1))
    for g in range(0, n2 // SUBLANES, 2):
        xg = jnp.concatenate([src3_ref[:, group(g + i), :].reshape(nh * SUBLANES, c) for i in range(2)],
                             axis=1).astype(BF16)
        out = jnp.dot(m1f, xg, preferred_element_type=F32)
        for i in range(2):
            a_ref[:, group(g + i), :] = out[:, c * i:c * (i + 1)].reshape(4 * nh, SUBLANES, c)


def _fft_block_inv(a_ref, dst3_ref, m1i_ref):
    nh, n2, c = dst3_ref.shape
    m1i = m1i_ref[...]
    group = lambda g: slice(SUBLANES * g, SUBLANES * (g + 1))
    for g in range(0, n2 // SUBLANES, 2):
        ag = jnp.concatenate([a_ref[:, group(g + i), :].reshape(4 * nh * SUBLANES, c) for i in range(2)],
                             axis=1).astype(BF16)
        out = jnp.dot(m1i, ag, preferred_element_type=F32)
        for i in range(2):
            dst3_ref[:, group(g + i), :] = out[:, c * i:c * (i + 1)].reshape(nh, SUBLANES, c)


def _hy_mlp(z_ref, w1_ref, b1_ref, w2_ref, b2_ref):
    h = jnp.sin(jnp.dot(z_ref[...], w1_ref[...], precision=HI, preferred_element_type=F32) + b1_ref[...])
    return jnp.sin(jnp.dot(h, w2_ref[...], precision=HI, preferred_element_type=F32) + b2_ref[...])


def _hy_taps(h2, w3_ref, b3_ref, delta_ref, zero_first):
    L = h2.shape[0]
    h = jnp.dot(h2, w3_ref[...], precision=HI, preferred_element_type=F32) + b3_ref[...]
    row = lax.broadcasted_iota(jnp.int32, h.shape, 0)
    t = row.astype(F32) * (1.0 / (L - 1))
    h = h * jnp.exp(-t * delta_ref[...])
    if zero_first:
        h = jnp.where(row == 0, 0.0, h)
    return h


def _hy_filter_kernel(z_ref, w1_ref, b1_ref, w2_ref, b2_ref, w3f_ref, b3f_ref, w3b_ref, b3b_ref,
                      delta_ref, m1f_ref, gf_ref, kf_ref, src_ref, af_ref, ab_ref):
    h2 = _hy_mlp(z_ref, w1_ref, b1_ref, w2_ref, b2_ref)
    hf = _hy_taps(h2, w3f_ref, b3f_ref, delta_ref, False)
    hb = _hy_taps(h2, w3b_ref, b3b_ref, delta_ref, True)
    L, c = hf.shape
    n_total = 2.0 * L
    inv = 1.0 / ((jnp.sum(jnp.abs(hf), axis=0, keepdims=True)
                  + jnp.sum(jnp.abs(hb), axis=0, keepdims=True)) * n_total)
    nh, n2 = src_ref.shape[0], src_ref.shape[1]
    src_ref[...] = (hf * inv).reshape(nh, n2, c)
    _fft_block_fwd(src_ref, af_ref, m1f_ref)
    src_ref[...] = (hb * inv).reshape(nh, n2, c)
    _fft_block_fwd(src_ref, ab_ref, m1f_ref)

    def body(k1, carry):
        sl = pl.ds(pl.multiple_of(2 * k1, 2), 2)
        g = gf_ref[k1]
        xf = jnp.dot(g, af_ref[sl].reshape(2 * n2, c).astype(BF16), preferred_element_type=F32)
        xb = jnp.dot(g, ab_ref[sl].reshape(2 * n2, c).astype(BF16), preferred_element_type=F32)
        kr = xf[:n2] + xb[:n2]
        ki = xf[n2:] - xb[n2:]
        kf_ref[0, sl] = jnp.concatenate([kr, ki], axis=0).reshape(2, n2, c).astype(BF16)
        return carry

    lax.fori_loop(0, gf_ref.shape[0], body, 0, unroll=FFT_UNROLL)


def _hy_filter(zfeat, w1, b1, w2, b2, w3, b3, delta, m1f, gf):
    L = zfeat.shape[0]
    C = delta.shape[1]
    cb = LANES
    ncb = C // cb
    nh, n2 = FFT_N1 // 2, FFT_N2
    full = lambda a: pl.BlockSpec(a.shape, lambda o, j: (0,) * a.ndim)
    return pl.pallas_call(
        _hy_filter_kernel,
        grid=(2, ncb),
        in_specs=[full(zfeat), full(w1), full(b1), full(w2), full(b2),
                  pl.BlockSpec((w3.shape[0], cb), lambda o, j: (0, o * 2 * ncb + j)),
                  pl.BlockSpec((1, cb), lambda o, j: (0, o * 2 * ncb + j)),
                  pl.BlockSpec((w3.shape[0], cb), lambda o, j: (0, o * 2 * ncb + ncb + j)),
                  pl.BlockSpec((1, cb), lambda o, j: (0, o * 2 * ncb + ncb + j)),
                  pl.BlockSpec((1, cb), lambda o, j: (0, j)),
                  _const_spec(m1f.shape), _const_spec(gf.shape)],
        out_specs=pl.BlockSpec((1, 4 * nh, n2, cb), lambda o, j: (o, 0, 0, j)),
        out_shape=jax.ShapeDtypeStruct((2, 4 * nh, n2, C), BF16),
        scratch_shapes=[pltpu.VMEM((nh, n2, cb), F32), pltpu.VMEM((4 * nh, n2, cb), F32),
                        pltpu.VMEM((4 * nh, n2, cb), F32)],
        compiler_params=_cparams("arbitrary", "arbitrary"),
        name="hy_filter",
    )(zfeat, w1, b1, w2, b2, w3, b3, w3, b3, delta, m1f, gf)


def _hy_conv_kernel(conv_z, norm_out, z_ref, g_ref, cwz_ref, cbz_ref, cwg_ref, cbg_ref, skip_ref,
                    kf_ref, m1f_ref, gf_ref, gb_ref, m1i_ref, o_ref, zs_ref, ys_ref, a_ref):
    nh, n2, c = zs_ref.shape
    z = z_ref[0]
    if conv_z:
        z = _conv3(z, cwz_ref[...], cbz_ref[...])
    zs_ref[...] = z.reshape(nh, n2, c)
    _fft_block_fwd(zs_ref, a_ref, m1f_ref)

    def body(k1, carry):
        sl = pl.ds(pl.multiple_of(2 * k1, 2), 2)
        x = jnp.dot(gf_ref[k1], a_ref[sl].reshape(2 * n2, c).astype(BF16), preferred_element_type=F32)
        kf = kf_ref[0, sl].astype(F32)
        xr, xi, kr, ki = x[:n2], x[n2:], kf[0], kf[1]
        y = jnp.concatenate([xr * kr - xi * ki, xr * ki + xi * kr], axis=0).astype(BF16)
        a_ref[sl] = jnp.dot(gb_ref[k1], y, preferred_element_type=F32).reshape(2, n2, c)
        return carry

    lax.fori_loop(0, gf_ref.shape[0], body, 0, unroll=FFT_UNROLL)
    _fft_block_inv(a_ref, ys_ref, m1i_ref)
    y = ys_ref[...].reshape(nh * n2, c)
    z = zs_ref[...].reshape(nh * n2, c)
    gate = _conv3(g_ref[0], cwg_ref[...], cbg_ref[...])
    out = gate * (y + skip_ref[...] * z)
    if norm_out:
        out = _group_rms_2x64(out)
    o_ref[0] = out


def _hy_conv(order, z, zcol, p_hy, conv_w, conv_b, skip, kf, tables):
    m1f, gf, gb, m1i = tables
    B, L, _ = p_hy.shape
    C = skip.shape[1]
    cb = LANES
    ncb = C // cb
    nh, n2 = FFT_N1 // 2, FFT_N2
    gcol = (order + 1) * ncb
    cw = lambda base: pl.BlockSpec((3, cb), lambda j, b: (0, base + j))
    cbs = lambda base: pl.BlockSpec((1, cb), lambda j, b: (0, base + j))
    return pl.pallas_call(
        functools.partial(_hy_conv_kernel, order == 0, order == 1),
        grid=(ncb, B),
        in_specs=[pl.BlockSpec((1, L, cb), lambda j, b: (b, 0, zcol + j)),
                  pl.BlockSpec((1, L, cb), lambda j, b: (b, 0, gcol + j)),
                  cw(0), cbs(0), cw(gcol), cbs(gcol),
                  pl.BlockSpec((1, cb), lambda j, b: (0, j)),
                  pl.BlockSpec((1, 4 * nh, n2, cb), lambda j, b: (order, 0, 0, j)),
                  _const_spec(m1f.shape), _const_spec(gf.shape), _const_spec(gb.shape),
                  _const_spec(m1i.shape)],
        out_specs=pl.BlockSpec((1, L, cb), lambda j, b: (b, 0, j)),
        out_shape=jax.ShapeDtypeStruct((B, L, C), F32),
        scratch_shapes=[pltpu.VMEM((nh, n2, cb), F32), pltpu.VMEM((nh, n2, cb), F32),
                        pltpu.VMEM((4 * nh, n2, cb), F32)],
        compiler_params=_cparams("arbitrary", "arbitrary"),
        name=f"hy_conv{order}",
    )(z, p_hy, conv_w, conv_b, conv_w, conv_b, skip[order:order + 1], kf, m1f, gf, gb, m1i)


def _hy_ctx_kernel(p_ref, z_ref, w1_ref, b1_ref, w2_ref, b2_ref, w3_ref, b3_ref, delta_ref,
                   cw_ref, cb_ref, skip_ref, fd_ref, fi_ref, o_ref):
    lc = p_ref.shape[1]
    C = skip_ref.shape[1]
    h2 = _hy_mlp(z_ref, w1_ref, b1_ref, w2_ref, b2_ref)
    u = _conv3(p_ref[0], cw_ref[...], cb_ref[...])
    z = u[:, 0:C]
    fd = fd_ref[...]
    fi = fi_ref[...]
    nf = fd.shape[0] // 2
    dot = lambda a, b: jnp.dot(a, b, precision=HI, preferred_element_type=F32)
    for o in range(2):
        base = 2 * o * C
        hf = _hy_taps(h2, w3_ref.at[:, base:base + C], b3_ref.at[:, base:base + C], delta_ref, False)
        hb = _hy_taps(h2, w3_ref.at[:, base + C:base + 2 * C], b3_ref.at[:, base + C:base + 2 * C],
                      delta_ref, True)
        inv = 1.0 / ((jnp.sum(jnp.abs(hf), axis=0, keepdims=True)
                      + jnp.sum(jnp.abs(hb), axis=0, keepdims=True)) * (2.0 * lc))
        xf = dot(fd, hf * inv)
        xb = dot(fd, hb * inv)
        kr = xf[:nf] + xb[:nf]
        ki = xf[nf:] - xb[nf:]
        x = dot(fd, z)
        xr, xi = x[:nf], x[nf:]
        y = dot(fi, jnp.concatenate([xr * kr - xi * ki, xr * ki + xi * kr], axis=0))
        z = u[:, (o + 1) * C:(o + 2) * C] * (y + skip_ref[o:o + 1, :] * z)
    for j in range(C // LANES):
        o_ref[0, :, j * LANES:(j + 1) * LANES] = _group_rms_2x64(z[:, j * LANES:(j + 1) * LANES])


def _dense_dft_tables(lc):
    n = 2 * lc
    k = np.arange(n)[:, None]
    t = np.arange(lc)[None, :]
    th = 2 * np.pi * ((k * t) % n) / n
    fd = np.concatenate([np.cos(th), -np.sin(th)], axis=0)
    fi = np.concatenate([np.cos(th).T, -np.sin(th).T], axis=1)
    return jnp.asarray(fd.astype(np.float32)), jnp.asarray(fi.astype(np.float32))


def _hy_ctx(p_hy, zfeat, w1, b1, w2, b2, w3, b3, delta, conv_w, conv_b, skip, fd, fi):
    B, lc, w = p_hy.shape
    C = skip.shape[1]
    full = lambda a: pl.BlockSpec(a.shape, lambda b: (0,) * a.ndim)
    args = (zfeat, w1, b1, w2, b2, w3, b3, delta, conv_w, conv_b, skip, fd, fi)
    return pl.pallas_call(
        _hy_ctx_kernel,
        grid=(B,),
        in_specs=[pl.BlockSpec((1, lc, w), lambda b: (b, 0, 0))] + [full(a) for a in args],
        out_specs=pl.BlockSpec((1, lc, C), lambda b: (b, 0, 0)),
        out_shape=jax.ShapeDtypeStruct((B, lc, C), F32),
        compiler_params=_cparams("parallel"),
        name="hy_ctx",
    )(p_hy, *args)


def _attn_kernel(seg_lens, lam_init, q_ref, lam_ref, *refs):
    nseg = len(seg_lens)
    k_refs, v_refs = refs[:nseg], refs[nseg:2 * nseg]
    o_ref = refs[2 * nseg]
    s_ref = refs[2 * nseg + 1]
    p_ref = refs[2 * nseg + 2]
    tq = q_ref.shape[1]
    th = tq // 2
    dl = lam_ref[...]
    lam = (jnp.exp(jnp.sum(dl[0:1] * dl[1:2], keepdims=True))
           - jnp.exp(jnp.sum(dl[2:3] * dl[3:4], keepdims=True))) + lam_init

    chunks = []
    off = 0
    for s, n in enumerate(seg_lens):
        for st in range(0, n, TK):
            w = min(TK, n - st)
            chunks.append((s, st, off, w))
            off += w

    def lane_tiles(a):
        return [a[:, t * LANES:(t + 1) * LANES] for t in range(a.shape[1] // LANES)]

    def scores(h, m):
        rows = slice(h * th, (h + 1) * th)
        q = q_ref[0, rows, :]
        lane = lax.broadcasted_iota(jnp.int32, q.shape, 1)
        qm = jnp.where((lane < HEAD_DIM) == (m == 0), q, jnp.zeros_like(q))
        mxw = jnp.full((th, LANES), -jnp.inf, F32)
        for (s, st, o, w) in chunks:
            sc = lax.dot_general(qm, k_refs[s][0, st:st + w, :], (((1,), (1,)), ((), ())),
                                 preferred_element_type=F32)
            s_ref[m, rows, o:o + w] = sc
            for t in lane_tiles(sc):
                mxw = jnp.maximum(mxw, t)
        return jnp.max(mxw, axis=-1, keepdims=True)

    def probs(h, m, mx):
        rows = slice(h * th, (h + 1) * th)
        lw = jnp.zeros((th, LANES), F32)
        for (s, st, o, w) in chunks:
            p = jnp.exp2(s_ref[m, rows, o:o + w] - mx)
            for t in lane_tiles(p):
                lw = lw + t
            p_ref[m, rows, o:o + w] = p.astype(BF16)
        return 1.0 / jnp.sum(lw, axis=-1, keepdims=True)

    def output(h, inv_l0, inv_l1):
        rows = slice(h * th, (h + 1) * th)
        r = lam * inv_l1 / inv_l0
        acc = jnp.zeros((th, v_refs[0].shape[2]), F32)
        for (s, st, o, w) in chunks:
            rb = jnp.broadcast_to(r, (th, w)).astype(BF16)
            wgt = p_ref[0, rows, o:o + w] - p_ref[1, rows, o:o + w] * rb
            acc = acc + jnp.dot(wgt, v_refs[s][0, st:st + w, :], preferred_element_type=F32)
        o_ref[0, rows, :] = _rms(acc * inv_l0) * (1.0 - lam_init)

    mx00 = scores(0, 0)
    mx01 = scores(0, 1)
    il00 = probs(0, 0, mx00)
    mx10 = scores(1, 0)
    il01 = probs(0, 1, mx01)
    mx11 = scores(1, 1)
    output(0, il00, il01)
    il10 = probs(1, 0, mx10)
    il11 = probs(1, 1, mx11)
    output(1, il10, il11)


def _diff_attn(q, ks, vs, da_lambda, lam_init):
    B, lq, w = q.shape
    H = w // LANES
    tq = min(TQ, lq)
    seg_lens = tuple(k.shape[1] for k in ks)
    seg = lambda n: pl.BlockSpec((1, n, LANES), lambda b, h, i: (b, 0, h))
    return pl.pallas_call(
        functools.partial(_attn_kernel, seg_lens, lam_init),
        grid=(B, H, lq // tq),
        in_specs=[pl.BlockSpec((1, tq, LANES), lambda b, h, i: (b, i, h)),
                  pl.BlockSpec(da_lambda.shape, lambda b, h, i: (0, 0))]
                 + [seg(n) for n in seg_lens] + [seg(n) for n in seg_lens],
        out_specs=pl.BlockSpec((1, tq, LANES), lambda b, h, i: (b, i, h)),
        out_shape=jax.ShapeDtypeStruct((B, lq, w), F32),
        scratch_shapes=[pltpu.VMEM((2, tq, sum(seg_lens)), F32), pltpu.VMEM((2, tq, sum(seg_lens)), BF16)],
        compiler_params=_cparams("parallel", "parallel", "arbitrary"),
        name="diff_attn",
    )(q, da_lambda, *ks, *vs)


def _mlstm_chunk(rev, q, k, v, gc, gr, tri_lo, tri_up, ct_ref, n_ref, m_ref):
    T = q.shape[0]
    d = 2 if not rev else 6
    ii = 0 if not rev else 4
    lane = lax.broadcasted_iota(jnp.int32, (T, LANES), 1)
    lo = lane < HEAD_DIM
    lo_row = lax.broadcasted_iota(jnp.int32, (1, LANES), 1) < HEAD_DIM
    csum_c = jnp.dot(tri_up if rev else tri_lo, _log_sigmoid(gc), precision=HI, preferred_element_type=F32)
    csum_r = jnp.dot(_log_sigmoid(gr), tri_lo if rev else tri_up, precision=HI, preferred_element_type=F32)
    r_i = lax.broadcasted_iota(jnp.int32, (T, T), 0)
    c_i = lax.broadcasted_iota(jnp.int32, (T, T), 1)
    mask = (c_i >= r_i) if rev else (c_i <= r_i)
    last = 0 if rev else T - 1
    qb = q.astype(BF16)
    kb = k.astype(BF16)
    vb = v.astype(BF16)
    zero = jnp.zeros_like(qb)
    sv, den, hden, wcol, decay, interw = [], [], [], [], [], []
    m_new = []
    for j in range(2):
        bcol = csum_c[:, d + j:d + j + 1]
        brow = csum_r[d + j:d + j + 1, :]
        igrow = gr[ii + j:ii + j + 1, :]
        igcol = gc[:, ii + j:ii + j + 1]
        m_prev = m_ref[j:j + 1, 0:1]
        logd = jnp.where(mask, bcol - brow + igrow, -jnp.inf)
        mrow = jnp.maximum(bcol + m_prev, jnp.max(logd, axis=-1, keepdims=True))
        dw = jnp.exp(logd - mrow)
        iw = jnp.exp(bcol + m_prev - mrow)
        qj = jnp.where(lo, qb, zero) if j == 0 else jnp.where(lo, zero, qb)
        s = lax.dot_general(qj, kb, (((1,), (1,)), ((), ())), preferred_element_type=F32) * dw
        sv.append(jnp.dot(s.astype(BF16), vb, preferred_element_type=F32))
        qn = jnp.sum(qj.astype(F32) * n_ref[...], axis=-1, keepdims=True)
        dn = jnp.sum(s, axis=-1, keepdims=True) + iw * qn
        den.append(dn)
        hden.append(jnp.maximum(jnp.abs(dn), jnp.exp(-mrow)))
        interw.append(iw)
        mn = mrow[last:last + 1, :]
        btot = bcol[last:last + 1, :]
        wcol.append(jnp.exp(btot - bcol + igcol - mn))
        decay.append(jnp.exp(btot + m_prev - mn))
        m_new.append(mn)
    ct = ct_ref[...]
    inter = jnp.dot(qb, ct.astype(BF16), preferred_element_type=F32)
    num = jnp.where(lo, sv[0], sv[1]) + jnp.where(lo, interw[0], interw[1]) * inter
    h = num / jnp.where(lo, hden[0], hden[1])
    wpair = jnp.where(lo, wcol[0], wcol[1])
    dpair = jnp.where(lo_row, decay[0], decay[1])
    upd = jnp.dot(k.T.astype(BF16), (v * wpair).astype(BF16), preferred_element_type=F32)
    rr = lax.broadcasted_iota(jnp.int32, (LANES, LANES), 0) < HEAD_DIM
    cc = lax.broadcasted_iota(jnp.int32, (LANES, LANES), 1) < HEAD_DIM
    ct_ref[...] = dpair * ct + jnp.where(rr == cc, upd, 0.0)
    n_ref[...] = dpair * n_ref[...] + jnp.sum(k * wpair, axis=0, keepdims=True)
    m_ref[0:1, :] = jnp.broadcast_to(m_new[0], (1, LANES))
    m_ref[1:2, :] = jnp.broadcast_to(m_new[1], (1, LANES))
    return h


def _mlstm_kernel(qc_ref, kc_ref, vc_ref, oc_ref, gcc_ref, grc_ref,
                  qx_ref, kx_ref, vx_ref, ox_ref, gcx_ref, grx_ref,
                  cwq_ref, cbq_ref, cwk_ref, cbk_ref, gbc_ref, gbr_ref,
                  yc_ref, yx_ref,
                  qs_ref, ks_ref, hf_ref, hb_ref, ct_ref, n_ref, m_ref):
    T = ML_T
    lx = qx_ref.shape[1]
    nx = lx // T
    r_i = lax.broadcasted_iota(jnp.int32, (T, T), 0)
    c_i = lax.broadcasted_iota(jnp.int32, (T, T), 1)
    tri_lo = (c_i <= r_i).astype(F32)
    tri_up = (c_i >= r_i).astype(F32)
    ct_ref[...] = jnp.zeros_like(ct_ref)
    n_ref[...] = jnp.zeros_like(n_ref)
    m_ref[...] = jnp.zeros_like(m_ref)
    scale = HEAD_DIM ** -0.5
    cwq, cbq, cwk, cbk = cwq_ref[...], cbq_ref[...], cwk_ref[...], cbk_ref[...]
    gbc, gbr = gbc_ref[...], gbr_ref[...]

    qc = _silu(_conv3(qc_ref[0], cwq, cbq))
    kc = _silu(_conv3(kc_ref[0], cwk, cbk)) * scale
    vc = vc_ref[0]
    gcc = gcc_ref[0] + gbc
    grc = grc_ref[0] + gbr
    h_c = None
    for rev in (False, True):
        di = int(rev)
        h = _mlstm_chunk(rev, qc, kc, vc, gcc, grc, tri_lo, tri_up,
                         ct_ref.at[di], n_ref.at[di], m_ref.at[di])
        h_c = h if h_c is None else h_c + h
    yc_ref[0] = _group_rms_2x64(h_c * _sigmoid(oc_ref[0]))

    qs_ref[...] = _silu(_conv3(qx_ref[0], cwq, cbq))
    ks_ref[...] = _silu(_conv3(kx_ref[0], cwk, cbk)) * scale

    def body(c, carry):
        for rev in (False, True):
            di = int(rev)
            cidx = (nx - 1 - c) if rev else c
            r0 = pl.multiple_of(cidx * T, T)
            rows = pl.ds(r0, T)
            h = _mlstm_chunk(rev, qs_ref[rows, :], ks_ref[rows, :], vx_ref[0, rows, :],
                             gcx_ref[0, rows, :] + gbc, grx_ref[0, :, rows] + gbr,
                             tri_lo, tri_up, ct_ref.at[di], n_ref.at[di], m_ref.at[di])
            if rev:
                hb_ref[rows, :] = h
            else:
                hf_ref[rows, :] = h
        return carry

    lax.fori_loop(0, nx, body, 0)
    yx_ref[0] = _group_rms_2x64((hf_ref[...] + hb_ref[...]) * _sigmoid(ox_ref[0]))


def _mlstm(seg_c, seg_x, conv_w, conv_b, gate_bc, gate_br):
    B, lx, _ = seg_x[1].shape
    lc = seg_c[1].shape[1]
    assert lc == ML_T and lx % ML_T == 0

    def seg_specs(L):
        col = lambda off: pl.BlockSpec((1, L, LANES), lambda b, p, off=off: (b, 0, off + p))
        return [col(0), col(2), col(0), col(0), col(0),
                pl.BlockSpec((1, SUBLANES, L), lambda b, p: (b, p, 0))]

    def seg_args(s):
        qk, v, o, g, gt = s
        return [qk, qk, v, o, g, gt]

    wspec = lambda rows, off: pl.BlockSpec((rows, LANES), lambda b, p, off=off: (0, off + p))
    return pl.pallas_call(
        _mlstm_kernel,
        grid=(B, 2),
        in_specs=seg_specs(lc) + seg_specs(lx)
                 + [wspec(3, 0), wspec(1, 0), wspec(3, 2), wspec(1, 2), wspec(1, 0),
                    pl.BlockSpec((SUBLANES, 1), lambda b, p: (p, 0))],
        out_specs=[pl.BlockSpec((1, lc, LANES), lambda b, p: (b, 0, p)),
                   pl.BlockSpec((1, lx, LANES), lambda b, p: (b, 0, p))],
        out_shape=[jax.ShapeDtypeStruct((B, lc, 2 * LANES), F32),
                   jax.ShapeDtypeStruct((B, lx, 2 * LANES), F32)],
        scratch_shapes=[pltpu.VMEM((lx, LANES), F32), pltpu.VMEM((lx, LANES), F32),
                        pltpu.VMEM((lx, LANES), F32), pltpu.VMEM((lx, LANES), F32),
                        pltpu.VMEM((2, LANES, LANES), F32), pltpu.VMEM((2, 1, LANES), F32),
                        pltpu.VMEM((2, 2, LANES), F32)],
        compiler_params=_cparams("parallel", "arbitrary"),
        name="mlstm",
    )(*seg_args(seg_c), *seg_args(seg_x), conv_w, conv_b, conv_w, conv_b, gate_bc, gate_br)


def _outproj_kernel(x_ref, g_ref, hy_ref, da_ref, ml_ref, gain_ref, w_ref, o_ref):
    c_hy = hy_ref.shape[2]
    c_da = da_ref.shape[2]
    gain = gain_ref[...]
    acc = jnp.dot((hy_ref[0] * gain[:, 0:c_hy]).astype(BF16), w_ref[0:c_hy, :], preferred_element_type=F32)
    acc += jnp.dot((da_ref[0] * gain[:, c_hy:c_hy + c_da]).astype(BF16), w_ref[c_hy:c_hy + c_da, :],
                   preferred_element_type=F32)
    acc += jnp.dot((ml_ref[0] * gain[:, c_hy + c_da:]).astype(BF16), w_ref[c_hy + c_da:, :],
                   preferred_element_type=F32)
    o_ref[0] = x_ref[0] + g_ref[0] * acc


def _outproj(x, gate, y_hy, y_da, y_ml, gain, w_out):
    B, L, D = x.shape
    tm = min(TM, L)
    tok = lambda w: pl.BlockSpec((1, tm, w), lambda b, i: (b, i, 0))
    return pl.pallas_call(
        _outproj_kernel,
        grid=(B, L // tm),
        in_specs=[tok(D), pl.BlockSpec((1, 1, D), lambda b, i: (b, 0, 0)),
                  tok(y_hy.shape[2]), tok(y_da.shape[2]), tok(y_ml.shape[2]),
                  pl.BlockSpec(gain.shape, lambda b, i: (0, 0)), _const_spec(w_out.shape)],
        out_specs=tok(D),
        out_shape=jax.ShapeDtypeStruct((B, L, D), F32),
        compiler_params=_cparams("parallel", "parallel"),
        name="outproj",
    )(x, gate, y_hy, y_da, y_ml, gain, w_out)


def _ffn_kernel(final, x_ref, prev_ref, next_ref, sh_ref, sc_ref, g_ref, up_ref, cw_ref, cb_ref,
                down_ref, fw_ref, o_ref, act_ref):
    i = pl.program_id(1)
    last = pl.num_programs(1) - 1
    tm = x_ref.shape[1]
    d_ff = down_ref.shape[0]
    x = x_ref[0]
    mod = lambda a: _rms(a) * (1.0 + sc_ref[0]) + sh_ref[0]
    hp = jnp.where(i == 0, 0.0, mod(prev_ref[0]))
    hn = jnp.where(i == last, 0.0, mod(next_ref[0]))
    h = jnp.concatenate([hp, mod(x), hn], axis=0).astype(BF16)
    ext = tm + 2 * SUBLANES

    def conv_cols(lo, hi):
        u = jnp.dot(h, up_ref[:, lo:hi], preferred_element_type=F32)
        w = cw_ref[:, lo:hi]
        c = (pltpu.roll(u, 1, axis=0) * w[0:1] + u * w[1:2] + pltpu.roll(u, ext - 1, axis=0) * w[2:3]
             + cb_ref[:, lo:hi])
        return c[SUBLANES:SUBLANES + tm]

    for j in range(d_ff // FFN_TN):
        lo = j * FFN_TN
        a = conv_cols(lo, lo + FFN_TN)
        g = conv_cols(d_ff + lo, d_ff + lo + FFN_TN)
        act_ref[:, lo:lo + FFN_TN] = (_silu(g) * a).astype(BF16)
    y = x + g_ref[0] * jnp.dot(act_ref[...], down_ref[...], preferred_element_type=F32)
    if final:
        y = _rms(y) * fw_ref[...]
    o_ref[0] = y


def _ffn(x, shift, scale, gate, up, conv_w, conv_b, down, final_w, final):
    B, L, D = x.shape
    tm = min(TM, L)
    nb = tm // SUBLANES
    nrow = L // SUBLANES
    tok = pl.BlockSpec((1, tm, D), lambda b, i: (b, i, 0))
    mod = pl.BlockSpec((1, 1, D), lambda b, i: (b, 0, 0))
    full = lambda a: pl.BlockSpec(a.shape, lambda b, i: (0,) * a.ndim)
    return pl.pallas_call(
        functools.partial(_ffn_kernel, final),
        grid=(B, L // tm),
        in_specs=[tok,
                  pl.BlockSpec((1, SUBLANES, D), lambda b, i: (b, jnp.maximum(i * nb - 1, 0), 0)),
                  pl.BlockSpec((1, SUBLANES, D), lambda b, i: (b, jnp.minimum((i + 1) * nb, nrow - 1), 0)),
                  mod, mod, mod, _const_spec(up.shape), full(conv_w), full(conv_b),
                  _const_spec(down.shape), full(final_w)],
        out_specs=tok,
        out_shape=jax.ShapeDtypeStruct((B, L, D), F32),
        scratch_shapes=[pltpu.VMEM((tm, down.shape[0]), BF16)],
        compiler_params=_cparams("parallel", "parallel"),
        name="ffn_final" if final else "ffn",
    )(x, x, x, shift, scale, gate, up, conv_w, conv_b, down, final_w)


def _rope_tables(L):
    rows_n = L // GRID_W
    rows = jnp.repeat(jnp.arange(rows_n, dtype=F32), GRID_W)
    cols = jnp.tile(jnp.arange(GRID_W, dtype=F32), rows_n)
    nf = HEAD_DIM // 4
    inv = ROPE_BASE ** (-jnp.arange(nf, dtype=F32) / nf)
    cr, sr = jnp.cos(rows[:, None] * inv), jnp.sin(rows[:, None] * inv)
    cc, sc = jnp.cos(cols[:, None] * inv), jnp.sin(cols[:, None] * inv)
    cos64 = jnp.concatenate([cr, cr, cc, cc], axis=-1)
    sin64 = jnp.concatenate([-sr, sr, -sc, sc], axis=-1)
    return jnp.tile(cos64, (1, 2)), jnp.tile(sin64, (1, 2))


def _hy_features(L):
    t = jnp.linspace(0.0, 1.0, L, dtype=F32)
    pos = jnp.arange(L, dtype=F32)
    f = jnp.linspace(1e-4, HY_BANDS - 1, HY_BANDS, dtype=F32)
    ang = (2.0 * math.pi / L) * pos[:, None] * f
    z = jnp.concatenate([t[:, None], jnp.cos(ang), jnp.sin(ang)], axis=-1)
    return jnp.pad(z, ((0, 0), (0, HY_POS_PAD - z.shape[1])))


def _gate_layout(a):
    g = a.reshape(a.shape[:-1] + (4, 2, 2))
    g = jnp.moveaxis(g, -2, -3).reshape(a.shape[:-1] + (2, 8))
    g = jnp.pad(g, [(0, 0)] * (g.ndim - 1) + [(0, LANES - 8)])
    return g.reshape(a.shape[:-1] + (2 * LANES,))


def _gates_t(g):
    return jnp.swapaxes(jnp.concatenate([g[..., 0:8], g[..., LANES:LANES + 8]], axis=-1), 1, 2)


def kernel(x, c, ctx, c_ctx, ada_w, ada_b, w_in, w_out, hy_conv_w, hy_conv_b, hy_w1, hy_b1, hy_w2, hy_b2,
           hy_w3, hy_b3, hy_skip, da_lambda, ml_conv_w, ml_conv_b, ml_gate_b, mix_norm_w, ffn_up,
           ffn_conv_w, ffn_conv_b, ffn_down, final_norm_w):
    B, L, D = x.shape
    lc = ctx.shape[1]
    depth = ada_w.shape[0]
    hy_w = hy_skip.shape[2]
    n_hy = 3 * hy_w
    da_w = 2 * hy_w
    ml_w = hy_w

    rows = ((B + 1 + SUBLANES - 1) // SUBLANES) * SUBLANES
    cc = jnp.zeros((rows, D), F32).at[:B].set(c).at[B].set(c_ctx)
    mods = _adaln(cc, ada_w, ada_b)

    cos_x, sin_x = _rope_tables(L)
    cos_c, sin_c = jnp.ones((lc, LANES), F32), jnp.zeros((lc, LANES), F32)
    tables = _dft_tables(L)
    fd_c, fi_c = _dense_dft_tables(lc)
    zf_x, zf_c = _hy_features(L), _hy_features(lc)
    delta = jnp.abs(jnp.linspace(math.log(HY_DECAY_TARGET) / HY_FAST_DECAY,
                                 math.log(HY_DECAY_TARGET) / HY_SLOW_DECAY, hy_w, dtype=F32))[None, :]
    swap = np.arange(da_w) ^ (HEAD_DIM // 4)
    fw = final_norm_w[None, :]

    for l in range(depth):
        lam_init = 0.8 - 0.6 * math.exp(-0.3 * l)
        update_ctx = l < depth - 1
        mx = [m[:, None, :] for m in jnp.split(mods[l, :B], 6, axis=-1)]
        mc = [jnp.broadcast_to(m[None], (B, 1, D)) for m in jnp.split(mods[l, B:B + 1], 6, axis=-1)]

        w = w_in[l]
        o = n_hy
        wq, wk, wv = w[:, o:o + da_w], w[:, o + da_w:o + 2 * da_w], w[:, o + 2 * da_w:o + 3 * da_w]
        o += 3 * da_w
        wmqk, wmv, wmo = w[:, o:o + 2 * ml_w], w[:, o + 2 * ml_w:o + 3 * ml_w], w[:, o + 3 * ml_w:o + 4 * ml_w]
        wg = _gate_layout(w[:, o + 4 * ml_w:])
        w_ext = jnp.concatenate([w[:, :n_hy], wq, wq[:, swap], wk, wk[:, swap], wv, wmqk, wmv, wmo, wg],
                                axis=1).astype(BF16)
        gate_b = _gate_layout(ml_gate_b[l])
        gate_bc = gate_b[None, :]
        gate_br = _gates_t(gate_b[None, None, :])[0]

        px = _inproj(x, mx[0], mx[1], w_ext, cos_x, sin_x)
        pc = _inproj(ctx, mc[0], mc[1], w_ext, cos_c, sin_c)
        hy_x, q_x, k_x, v_x, mqk_x, mv_x, mo_x, g_x = px
        hy_c, q_c, k_c, v_c, mqk_c, mv_c, mo_c, g_c = pc

        w1p = jnp.pad(hy_w1[l], ((0, HY_POS_PAD - hy_w1.shape[1]), (0, 0)))
        b1, b2, b3 = hy_b1[l][None, :], hy_b2[l][None, :], hy_b3[l][None, :]
        cbias = hy_conv_b[l][None, :]
        kf = _hy_filter(zf_x, w1p, b1, hy_w2[l], b2, hy_w3[l], b3, delta, tables[0], tables[1])
        z1 = _hy_conv(0, hy_x, 0, hy_x, hy_conv_w[l], cbias, hy_skip[l], kf, tables)
        y_hy_x = _hy_conv(1, z1, 0, hy_x, hy_conv_w[l], cbias, hy_skip[l], kf, tables)

        y_da_x = _diff_attn(q_x, [k_c, k_x], [v_c, v_x], da_lambda[l], lam_init)

        seg_c = (mqk_c, mv_c, mo_c, g_c, _gates_t(g_c))
        seg_x = (mqk_x, mv_x, mo_x, g_x, _gates_t(g_x))
        y_ml_c, y_ml_x = _mlstm(seg_c, seg_x, ml_conv_w[l], ml_conv_b[l][None, :], gate_bc, gate_br)

        gain = mix_norm_w[l][None, :]
        wo = w_out[l].astype(BF16)
        up = ffn_up[l].astype(BF16)
        down = ffn_down[l].astype(BF16)
        fcb = ffn_conv_b[l][None, :]
        x = _outproj(x, mx[2], y_hy_x, y_da_x, y_ml_x, gain, wo)
        x = _ffn(x, mx[3], mx[4], mx[5], up, ffn_conv_w[l], fcb, down, fw, final=not update_ctx)

        if update_ctx:
            y_hy_c = _hy_ctx(hy_c, zf_c, w1p, b1, hy_w2[l], b2, hy_w3[l], b3, delta,
                             hy_conv_w[l], cbias, hy_skip[l], fd_c, fi_c)
            y_da_c = _diff_attn(q_c, [k_c], [v_c], da_lambda[l], lam_init)
            ctx = _outproj(ctx, mc[2], y_hy_c, y_da_c, y_ml_c, gain, wo)
            ctx = _ffn(ctx, mc[3], mc[4], mc[5], up, ffn_conv_w[l], fcb, down, fw, final=False)
    return x
```

```python
import functools
import math

import numpy as np
import jax
import jax.numpy as jnp
from jax import lax
from jax.experimental import pallas as pl
from jax.experimental.pallas import tpu as pltpu

F32 = jnp.float32
BF16 = jnp.bfloat16
HI = lax.Precision.HIGHEST

EPS = 1e-6
HEAD_DIM = 64
GRID_W = 64
ROPE_BASE = 10000.0
HY_BANDS = 16
HY_POS_PAD = 64
HY_DECAY_TARGET = 1e-2
HY_FAST_DECAY = 0.3
HY_SLOW_DECAY = 1.5
LANES = 128
SUBLANES = 8
VMEM_LIMIT = 56 * 1024 * 1024

FFT_N1 = 64
FFT_N2 = 128
FFT_UNROLL = 4
ML_T = 256
TQ = 1024
TQ_SUB = 256
TK = 512
TM = 512
FFN_TN = 256


def _cparams(*sem):
    return pltpu.CompilerParams(dimension_semantics=sem, vmem_limit_bytes=VMEM_LIMIT)


def _const_spec(shape):
    n = len(shape)
    return pl.BlockSpec(shape, lambda *_: (0,) * n, pipeline_mode=pl.Buffered(1))


def _silu(x):
    return x / (1.0 + jnp.exp(-x))


def _sigmoid(x):
    return 1.0 / (1.0 + jnp.exp(-x))


def _log_sigmoid(x):
    return jnp.minimum(x, 0.0) - jnp.log(1.0 + jnp.exp(-jnp.abs(x)))


def _rms(x):
    return x * lax.rsqrt(jnp.mean(x * x, axis=-1, keepdims=True) + EPS)


def _group_rms_2x64(y):
    lo = lax.broadcasted_iota(jnp.int32, y.shape, 1) < 64
    y2 = y * y
    s_lo = jnp.sum(jnp.where(lo, y2, 0.0), axis=-1, keepdims=True)
    s_hi = jnp.sum(jnp.where(lo, 0.0, y2), axis=-1, keepdims=True)
    r = jnp.where(lo, lax.rsqrt(s_lo * (1.0 / 64) + EPS), lax.rsqrt(s_hi * (1.0 / 64) + EPS))
    return y * r


def _conv3(x, w, b):
    L = x.shape[0]
    row = lax.broadcasted_iota(jnp.int32, x.shape, 0)
    xm = jnp.where(row == 0, 0.0, pltpu.roll(x, 1, axis=0))
    xp = jnp.where(row == L - 1, 0.0, pltpu.roll(x, L - 1, axis=0))
    return xm * w[0:1] + x * w[1:2] + xp * w[2:3] + b


def _adaln_kernel(c_ref, w_ref, b_ref, o_ref):
    a = _silu(c_ref[...])
    o_ref[0] = jnp.dot(a, w_ref[0], precision=HI, preferred_element_type=F32) + b_ref[0]


def _adaln(cc, ada_w, ada_b):
    depth, d, n6 = ada_w.shape
    tn = 1536
    return pl.pallas_call(
        _adaln_kernel,
        grid=(depth, n6 // tn),
        in_specs=[pl.BlockSpec(cc.shape, lambda l, j: (0, 0)),
                  pl.BlockSpec((1, d, tn), lambda l, j: (l, 0, j)),
                  pl.BlockSpec((1, 1, tn), lambda l, j: (l, 0, j))],
        out_specs=pl.BlockSpec((1, cc.shape[0], tn), lambda l, j: (l, 0, j)),
        out_shape=jax.ShapeDtypeStruct((depth, cc.shape[0], n6), F32),
        compiler_params=_cparams("parallel", "parallel"),
        name="adaln",
    )(cc, ada_w, ada_b.reshape(depth, 1, n6))


_C_HY, _C_Q, _C_QS, _C_K, _C_KS, _C_V, _C_MQK, _C_MV, _C_MO, _C_G, _C_END = (
    0, 768, 1280, 1792, 2304, 2816, 3328, 3840, 4096, 4352, 4608)


def _inproj_kernel(x_ref, sh_ref, sc_ref, w_ref, cos_ref, sin_ref,
                   hy_ref, q_ref, k_ref, v_ref, mqk_ref, mv_ref, mo_ref, g_ref):
    h = (_rms(x_ref[0]) * (1.0 + sc_ref[0]) + sh_ref[0]).astype(BF16)

    def proj(lo, hi):
        return jnp.dot(h, w_ref[:, lo:hi], preferred_element_type=F32)

    hy_ref[0] = proj(_C_HY, _C_Q)
    cos = cos_ref[...]
    sin = sin_ref[...]

    def rope(c0, c1, out_ref, scale):
        a = proj(c0, c1)
        asw = proj(c1, 2 * c1 - c0)
        for j in range((c1 - c0) // LANES):
            sl = slice(LANES * j, LANES * (j + 1))
            out_ref[0, :, sl] = ((a[:, sl] * cos + asw[:, sl] * sin) * scale).astype(BF16)

    rope(_C_Q, _C_QS, q_ref, (HEAD_DIM ** -0.5) * math.log2(math.e))
    rope(_C_K, _C_KS, k_ref, 1.0)
    v_ref[0] = proj(_C_V, _C_MQK).astype(BF16)
    mqk_ref[0] = proj(_C_MQK, _C_MV)
    mv_ref[0] = proj(_C_MV, _C_MO)
    mo_ref[0] = proj(_C_MO, _C_G)
    g_ref[0] = proj(_C_G, _C_END)


def _inproj(x, shift, scale, w_ext, cos, sin):
    B, L, D = x.shape
    tm = min(TM, L)
    tok = lambda w: pl.BlockSpec((1, tm, w), lambda b, i: (b, i, 0))
    mod = pl.BlockSpec((1, 1, D), lambda b, i: (b, 0, 0))
    tab = pl.BlockSpec((tm, LANES), lambda b, i: (i, 0))
    widths = (768, 512, 512, 512, 512, 256, 256, 256)
    dtypes = (F32, BF16, BF16, BF16, F32, F32, F32, F32)
    return pl.pallas_call(
        _inproj_kernel,
        grid=(B, L // tm),
        in_specs=[tok(D), mod, mod, _const_spec(w_ext.shape), tab, tab],
        out_specs=[tok(w) for w in widths],
        out_shape=[jax.ShapeDtypeStruct((B, L, w), dt) for w, dt in zip(widths, dtypes)],
        compiler_params=_cparams("parallel", "parallel"),
        name="inproj",
    )(x, shift, scale, w_ext, cos, sin)


def _dft_tables(L):
    n1, n2, g = FFT_N1, FFT_N2, SUBLANES
    N = n1 * n2
    assert N == 2 * L
    k1 = np.arange(n1)[:, None]
    m1 = np.arange(n1 // 2)[None, :]
    th = 2 * np.pi * ((k1 * m1) % n1) / n1
    f = np.stack([np.cos(th), -np.sin(th)], axis=1)
    eye = np.eye(g)
    m1f = np.einsum('krn,gh->krgnh', f, eye).reshape(2 * n1 * g, (n1 // 2) * g)
    m1i = np.einsum('krn,gh->ngkrh', f, eye).reshape((n1 // 2) * g, 2 * n1 * g)
    a = np.arange(n2)
    th2 = 2 * np.pi * ((a[:, None] * a[None, :]) % n2) / n2
    fr, fi = np.cos(th2), -np.sin(th2)
    tw = 2 * np.pi * ((np.arange(n1)[:, None] * a[None, :]) % N) / N
    tr, ti = np.cos(tw)[:, None, :], -np.sin(tw)[:, None, :]
    gr = fr[None] * tr - fi[None] * ti
    gi = fr[None] * ti + fi[None] * tr
    gf = np.concatenate([np.concatenate([gr, -gi], axis=2),
                         np.concatenate([gi, gr], axis=2)], axis=1)
    gb = np.transpose(gf, (0, 2, 1))
    cvt = lambda m: jnp.asarray(m.astype(np.float32)).astype(BF16)
    return cvt(m1f), cvt(gf), cvt(gb), cvt(m1i)


def _fft_block_fwd(src3_ref, a_ref, m1f_ref):
    nh, n2, c = src3_ref.shape
    m1f = m1f_ref[...]
    group = lambda g: slice(SUBLANES * g, SUBLANES * (g + 1))
    for g in range(0, n2 // SUBLANES, 2):
        xg = jnp.concatenate([src3_ref[:, group(g + i), :].reshape(nh * SUBLANES, c) for i in range(2)],
                             axis=1).astype(BF16)
        out = jnp.dot(m1f, xg, preferred_element_type=F32)
        for i in range(2):
            a_ref[:, group(g + i), :] = out[:, c * i:c * (i + 1)].reshape(4 * nh, SUBLANES, c)


def _fft_block_inv(a_ref, dst3_ref, m1i_ref):
    nh, n2, c = dst3_ref.shape
    m1i = m1i_ref[...]
    group = lambda g: slice(SUBLANES * g, SUBLANES * (g + 1))
    for g in range(0, n2 // SUBLANES, 2):
        ag = jnp.concatenate([a_ref[:, group(g + i), :].reshape(4 * nh * SUBLANES, c) for i in range(2)],
                             axis=1).astype(BF16)
        out = jnp.dot(m1i, ag, preferred_element_type=F32)
        for i in range(2):
            dst3_ref[:, group(g + i), :] = out[:, c * i:c * (i + 1)].reshape(nh, SUBLANES, c)


def _hy_mlp(z_ref, w1_ref, b1_ref, w2_ref, b2_ref):
    h = jnp.sin(jnp.dot(z_ref[...], w1_ref[...], precision=HI, preferred_element_type=F32) + b1_ref[...])
    return jnp.sin(jnp.dot(h, w2_ref[...], precision=HI, preferred_element_type=F32) + b2_ref[...])


def _hy_taps(h2, w3_ref, b3_ref, delta_ref, zero_first):
    L = h2.shape[0]
    h = jnp.dot(h2, w3_ref[...], precision=HI, preferred_element_type=F32) + b3_ref[...]
    row = lax.broadcasted_iota(jnp.int32, h.shape, 0)
    t = row.astype(F32) * (1.0 / (L - 1))
    h = h * jnp.exp(-t * delta_ref[...])
    if zero_first:
        h = jnp.where(row == 0, 0.0, h)
    return h


def _hy_filter_kernel(z_ref, w1_ref, b1_ref, w2_ref, b2_ref, w3f_ref, b3f_ref, w3b_ref, b3b_ref,
                      delta_ref, m1f_ref, gf_ref, kf_ref, src_ref, af_ref, ab_ref):
    h2 = _hy_mlp(z_ref, w1_ref, b1_ref, w2_ref, b2_ref)
    hf = _hy_taps(h2, w3f_ref, b3f_ref, delta_ref, False)
    hb = _hy_taps(h2, w3b_ref, b3b_ref, delta_ref, True)
    L, c = hf.shape
    n_total = 2.0 * L
    inv = 1.0 / ((jnp.sum(jnp.abs(hf), axis=0, keepdims=True)
                  + jnp.sum(jnp.abs(hb), axis=0, keepdims=True)) * n_total)
    nh, n2 = src_ref.shape[0], src_ref.shape[1]
    src_ref[...] = (hf * inv).reshape(nh, n2, c)
    _fft_block_fwd(src_ref, af_ref, m1f_ref)
    src_ref[...] = (hb * inv).reshape(nh, n2, c)
    _fft_block_fwd(src_ref, ab_ref, m1f_ref)

    def body(k1, carry):
        sl = pl.ds(pl.multiple_of(2 * k1, 2), 2)
        g = gf_ref[k1]
        xf = jnp.dot(g, af_ref[sl].reshape(2 * n2, c).astype(BF16), preferred_element_type=F32)
        xb = jnp.dot(g, ab_ref[sl].reshape(2 * n2, c).astype(BF16), preferred_element_type=F32)
        kr = xf[:n2] + xb[:n2]
        ki = xf[n2:] - xb[n2:]
        kf_ref[0, sl] = jnp.concatenate([kr, ki], axis=0).reshape(2, n2, c).astype(BF16)
        return carry

    lax.fori_loop(0, gf_ref.shape[0], body, 0, unroll=FFT_UNROLL)


def _hy_filter(zfeat, w1, b1, w2, b2, w3, b3, delta, m1f, gf):
    L = zfeat.shape[0]
    C = delta.shape[1]
    cb = LANES
    ncb = C // cb
    nh, n2 = FFT_N1 // 2, FFT_N2
    full = lambda a: pl.BlockSpec(a.shape, lambda o, j: (0,) * a.ndim)
    return pl.pallas_call(
        _hy_filter_kernel,
        grid=(2, ncb),
        in_specs=[full(zfeat), full(w1), full(b1), full(w2), full(b2),
                  pl.BlockSpec((w3.shape[0], cb), lambda o, j: (0, o * 2 * ncb + j)),
                  pl.BlockSpec((1, cb), lambda o, j: (0, o * 2 * ncb + j)),
                  pl.BlockSpec((w3.shape[0], cb), lambda o, j: (0, o * 2 * ncb + ncb + j)),
                  pl.BlockSpec((1, cb), lambda o, j: (0, o * 2 * ncb + ncb + j)),
                  pl.BlockSpec((1, cb), lambda o, j: (0, j)),
                  _const_spec(m1f.shape), _const_spec(gf.shape)],
        out_specs=pl.BlockSpec((1, 4 * nh, n2, cb), lambda o, j: (o, 0, 0, j)),
        out_shape=jax.ShapeDtypeStruct((2, 4 * nh, n2, C), BF16),
        scratch_shapes=[pltpu.VMEM((nh, n2, cb), F32), pltpu.VMEM((4 * nh, n2, cb), F32),
                        pltpu.VMEM((4 * nh, n2, cb), F32)],
        compiler_params=_cparams("arbitrary", "arbitrary"),
        name="hy_filter",
    )(zfeat, w1, b1, w2, b2, w3, b3, w3, b3, delta, m1f, gf)


def _hy_conv_kernel(conv_z, norm_out, z_ref, g_ref, cwz_ref, cbz_ref, cwg_ref, cbg_ref, skip_ref,
                    kf_ref, m1f_ref, gf_ref, gb_ref, m1i_ref, o_ref, zs_ref, ys_ref, a_ref):
    nh, n2, c = zs_ref.shape
    z = z_ref[0]
    if conv_z:
        z = _conv3(z, cwz_ref[...], cbz_ref[...])
    zs_ref[...] = z.reshape(nh, n2, c)
    _fft_block_fwd(zs_ref, a_ref, m1f_ref)

    def body(k1, carry):
        sl = pl.ds(pl.multiple_of(2 * k1, 2), 2)
        x = jnp.dot(gf_ref[k1], a_ref[sl].reshape(2 * n2, c).astype(BF16), preferred_element_type=F32)
        kf = kf_ref[0, sl].astype(F32)
        xr, xi, kr, ki = x[:n2], x[n2:], kf[0], kf[1]
        y = jnp.concatenate([xr * kr - xi * ki, xr * ki + xi * kr], axis=0).astype(BF16)
        a_ref[sl] = jnp.dot(gb_ref[k1], y, preferred_element_type=F32).reshape(2, n2, c)
        return carry

    lax.fori_loop(0, gf_ref.shape[0], body, 0, unroll=FFT_UNROLL)
    _fft_block_inv(a_ref, ys_ref, m1i_ref)
    y = ys_ref[...].reshape(nh * n2, c)
    z = zs_ref[...].reshape(nh * n2, c)
    gate = _conv3(g_ref[0], cwg_ref[...], cbg_ref[...])
    out = gate * (y + skip_ref[...] * z)
    if norm_out:
        out = _group_rms_2x64(out)
    o_ref[0] = out


def _hy_conv(order, z, zcol, p_hy, conv_w, conv_b, skip, kf, tables):
    m1f, gf, gb, m1i = tables
    B, L, _ = p_hy.shape
    C = skip.shape[1]
    cb = LANES
    ncb = C // cb
    nh, n2 = FFT_N1 // 2, FFT_N2
    gcol = (order + 1) * ncb
    cw = lambda base: pl.BlockSpec((3, cb), lambda j, b: (0, base + j))
    cbs = lambda base: pl.BlockSpec((1, cb), lambda j, b: (0, base + j))
    return pl.pallas_call(
        functools.partial(_hy_conv_kernel, order == 0, order == 1),
        grid=(ncb, B),
        in_specs=[pl.BlockSpec((1, L, cb), lambda j, b: (b, 0, zcol + j)),
                  pl.BlockSpec((1, L, cb), lambda j, b: (b, 0, gcol + j)),
                  cw(0), cbs(0), cw(gcol), cbs(gcol),
                  pl.BlockSpec((1, cb), lambda j, b: (0, j)),
                  pl.BlockSpec((1, 4 * nh, n2, cb), lambda j, b: (order, 0, 0, j)),
                  _const_spec(m1f.shape), _const_spec(gf.shape), _const_spec(gb.shape),
                  _const_spec(m1i.shape)],
        out_specs=pl.BlockSpec((1, L, cb), lambda j, b: (b, 0, j)),
        out_shape=jax.ShapeDtypeStruct((B, L, C), F32),
        scratch_shapes=[pltpu.VMEM((nh, n2, cb), F32), pltpu.VMEM((nh, n2, cb), F32),
                        pltpu.VMEM((4 * nh, n2, cb), F32)],
        compiler_params=_cparams("arbitrary", "arbitrary"),
        name=f"hy_conv{order}",
    )(z, p_hy, conv_w, conv_b, conv_w, conv_b, skip[order:order + 1], kf, m1f, gf, gb, m1i)


def _hy_ctx_kernel(p_ref, z_ref, w1_ref, b1_ref, w2_ref, b2_ref, w3_ref, b3_ref, delta_ref,
                   cw_ref, cb_ref, skip_ref, fd_ref, fi_ref, o_ref):
    lc = p_ref.shape[1]
    C = skip_ref.shape[1]
    h2 = _hy_mlp(z_ref, w1_ref, b1_ref, w2_ref, b2_ref)
    u = _conv3(p_ref[0], cw_ref[...], cb_ref[...])
    z = u[:, 0:C]
    fd = fd_ref[...]
    fi = fi_ref[...]
    nf = fd.shape[0] // 2
    dot = lambda a, b: jnp.dot(a, b, precision=HI, preferred_element_type=F32)
    for o in range(2):
        base = 2 * o * C
        hf = _hy_taps(h2, w3_ref.at[:, base:base + C], b3_ref.at[:, base:base + C], delta_ref, False)
        hb = _hy_taps(h2, w3_ref.at[:, base + C:base + 2 * C], b3_ref.at[:, base + C:base + 2 * C],
                      delta_ref, True)
        inv = 1.0 / ((jnp.sum(jnp.abs(hf), axis=0, keepdims=True)
                      + jnp.sum(jnp.abs(hb), axis=0, keepdims=True)) * (2.0 * lc))
        xf = dot(fd, hf * inv)
        xb = dot(fd, hb * inv)
        kr = xf[:nf] + xb[:nf]
        ki = xf[nf:] - xb[nf:]
        x = dot(fd, z)
        xr, xi = x[:nf], x[nf:]
        y = dot(fi, jnp.concatenate([xr * kr - xi * ki, xr * ki + xi * kr], axis=0))
        z = u[:, (o + 1) * C:(o + 2) * C] * (y + skip_ref[o:o + 1, :] * z)
    for j in range(C // LANES):
        o_ref[0, :, j * LANES:(j + 1) * LANES] = _group_rms_2x64(z[:, j * LANES:(j + 1) * LANES])


def _dense_dft_tables(lc):
    n = 2 * lc
    k = np.arange(n)[:, None]
    t = np.arange(lc)[None, :]
    th = 2 * np.pi * ((k * t) % n) / n
    fd = np.concatenate([np.cos(th), -np.sin(th)], axis=0)
    fi = np.concatenate([np.cos(th).T, -np.sin(th).T], axis=1)
    return jnp.asarray(fd.astype(np.float32)), jnp.asarray(fi.astype(np.float32))


def _hy_ctx(p_hy, zfeat, w1, b1, w2, b2, w3, b3, delta, conv_w, conv_b, skip, fd, fi):
    B, lc, w = p_hy.shape
    C = skip.shape[1]
    full = lambda a: pl.BlockSpec(a.shape, lambda b: (0,) * a.ndim)
    args = (zfeat, w1, b1, w2, b2, w3, b3, delta, conv_w, conv_b, skip, fd, fi)
    return pl.pallas_call(
        _hy_ctx_kernel,
        grid=(B,),
        in_specs=[pl.BlockSpec((1, lc, w), lambda b: (b, 0, 0))] + [full(a) for a in args],
        out_specs=pl.BlockSpec((1, lc, C), lambda b: (b, 0, 0)),
        out_shape=jax.ShapeDtypeStruct((B, lc, C), F32),
        compiler_params=_cparams("parallel"),
        name="hy_ctx",
    )(p_hy, *args)


def _attn_kernel(seg_lens, lam_init, q_ref, lam_ref, *refs):
    nseg = len(seg_lens)
    k_refs, v_refs = refs[:nseg], refs[nseg:2 * nseg]
    o_ref = refs[2 * nseg]
    n_s, n_p = 3, 2
    s_refs = refs[2 * nseg + 1:2 * nseg + 1 + n_s]
    p_flat = refs[2 * nseg + 1 + n_s:2 * nseg + 1 + n_s + 2 * n_p]
    p_refs = [p_flat[2 * i:2 * i + 2] for i in range(n_p)]
    tq = q_ref.shape[1]
    th = s_refs[0].shape[0]
    dl = lam_ref[...]
    lam = (jnp.exp(jnp.sum(dl[0:1] * dl[1:2], keepdims=True))
           - jnp.exp(jnp.sum(dl[2:3] * dl[3:4], keepdims=True))) + lam_init

    chunks = []
    off = 0
    for s, n in enumerate(seg_lens):
        for st in range(0, n, TK):
            w = min(TK, n - st)
            chunks.append((s, st, off, w))
            off += w

    def lane_tiles(a):
        return [a[:, t * LANES:(t + 1) * LANES] for t in range(a.shape[1] // LANES)]

    nh = tq // th
    nck = len(chunks)
    val = {}

    def scores_chunk(h, m, ci):
        rows = slice(h * th, (h + 1) * th)
        if ci == 0:
            q = q_ref[0, rows, :]
            lane = lax.broadcasted_iota(jnp.int32, q.shape, 1)
            val['q', h, m] = jnp.where((lane < HEAD_DIM) == (m == 0), q, jnp.zeros_like(q))
            val['mx', h, m] = jnp.full((th, LANES), -jnp.inf, F32)
        s, st, o, w = chunks[ci]
        sc = lax.dot_general(val['q', h, m], k_refs[s][0, st:st + w, :], (((1,), (1,)), ((), ())),
                             preferred_element_type=F32)
        s_refs[(2 * h + m) % n_s][:, o:o + w] = sc
        for t in lane_tiles(sc):
            val['mx', h, m] = jnp.maximum(val['mx', h, m], t)
        if ci == nck - 1:
            val['mx', h, m] = jnp.max(val['mx', h, m], axis=-1, keepdims=True)

    def probs_chunk(h, m, ci):
        if ci == 0:
            val['l', h, m] = jnp.zeros((th, LANES), F32)
        s, st, o, w = chunks[ci]
        p = jnp.exp2(s_refs[(2 * h + m) % n_s][:, o:o + w] - val['mx', h, m])
        for t in lane_tiles(p):
            val['l', h, m] = val['l', h, m] + t
        p_refs[h % n_p][m][:, o:o + w] = p.astype(BF16)
        if ci == nck - 1:
            val['l', h, m] = 1.0 / jnp.sum(val['l', h, m], axis=-1, keepdims=True)

    def output_chunk(h, _, ci):
        if ci == 0:
            r = lam * val['l', h, 1] / val['l', h, 0]
            val['r', h] = jnp.broadcast_to(r, (th, LANES)).astype(BF16)
            val['acc', h] = jnp.zeros((th, v_refs[0].shape[2]), F32)
        s, st, o, w = chunks[ci]
        rb = jnp.concatenate([val['r', h]] * (w // LANES), axis=1)
        wgt = p_refs[h % n_p][0][:, o:o + w] - p_refs[h % n_p][1][:, o:o + w] * rb
        val['acc', h] = val['acc', h] + jnp.dot(wgt, v_refs[s][0, st:st + w, :], preferred_element_type=F32)
        if ci == nck - 1:
            o_ref[0, h * th:(h + 1) * th, :] = _rms(val['acc', h] * val['l', h, 0]) * (1.0 - lam_init)

    mxu_phases = []
    for h in range(nh):
        mxu_phases += [('S', h, 0), ('S', h, 1)]
        if h >= 1:
            mxu_phases.append(('O', h - 1, 0))
    mxu_phases.append(('O', nh - 1, 0))
    vpu_phases = [('P', h, m) for h in range(nh) for m in range(2)]
    fn = {'S': scores_chunk, 'P': probs_chunk, 'O': output_chunk}
    cost = {'S': 1.0, 'O': 1.0, 'P': 1.3}

    def deps(ph):
        kind, h, m = ph
        i = 2 * h + m
        if kind == 'S':
            return [('P', (i - n_s) // 2, (i - n_s) % 2)] if i >= n_s else []
        if kind == 'P':
            return [('S', h, m)] + ([('O', h - n_p, 0)] if h >= n_p else [])
        return [('P', h, 0), ('P', h, 1)]

    done, order = {}, []
    streams = [[mxu_phases, 0, 0, 0.0], [vpu_phases, 0, 0, 0.0]]
    while any(st[1] < len(st[0]) for st in streams):
        best = None
        for st in streams:
            if st[1] == len(st[0]):
                continue
            ph = st[0][st[1]]
            need = deps(ph) if st[2] == 0 else []
            if any(d not in done for d in need):
                continue
            start = max([st[3]] + [done[d] for d in need])
            if best is None or start < best[0]:
                best = (start, st)
        start, st = best
        ph = st[0][st[1]]
        order.append((ph, st[2]))
        st[3] = start + cost[ph[0]] * chunks[st[2]][3]
        st[2] += 1
        if st[2] == nck:
            done[ph] = st[3]
            st[1] += 1
            st[2] = 0
    for (kind, h, m), ci in order:
        fn[kind](h, m, ci)


def _diff_attn(q, ks, vs, da_lambda, lam_init):
    B, lq, w = q.shape
    H = w // LANES
    tq = min(TQ, lq)
    th = min(TQ_SUB, tq // 2)
    seg_lens = tuple(k.shape[1] for k in ks)
    seg = lambda n: pl.BlockSpec((1, n, LANES), lambda b, h, i: (b, 0, h))
    return pl.pallas_call(
        functools.partial(_attn_kernel, seg_lens, lam_init),
        grid=(B, H, lq // tq),
        in_specs=[pl.BlockSpec((1, tq, LANES), lambda b, h, i: (b, i, h)),
                  pl.BlockSpec(da_lambda.shape, lambda b, h, i: (0, 0))]
                 + [seg(n) for n in seg_lens] + [seg(n) for n in seg_lens],
        out_specs=pl.BlockSpec((1, tq, LANES), lambda b, h, i: (b, i, h)),
        out_shape=jax.ShapeDtypeStruct((B, lq, w), F32),
        scratch_shapes=[pltpu.VMEM((th, sum(seg_lens)), F32)] * 3
                       + [pltpu.VMEM((th, sum(seg_lens)), BF16)] * 4,
        compiler_params=_cparams("parallel", "parallel", "arbitrary"),
        name="diff_attn",
    )(q, da_lambda, *ks, *vs)


def _split3(x):
    hi = x.astype(BF16)
    r1 = x - hi.astype(F32)
    mid = r1.astype(BF16)
    return hi, mid, (r1 - mid.astype(F32)).astype(BF16)


def _mlstm_cumsums(gc, gr, tri_lo, tri_up):
    dot = lambda a, b: jnp.dot(a, b, preferred_element_type=F32)
    lc = _split3(_log_sigmoid(gc))
    lr = _split3(_log_sigmoid(gr))
    pre_c = sum(dot(tri_lo, t) for t in lc)
    suf_c = sum(dot(tri_up, t) for t in lc)
    pre_r = sum(dot(t, tri_up) for t in lr)
    suf_r = sum(dot(t, tri_lo) for t in lr)
    fwd_lane = lax.broadcasted_iota(jnp.int32, gc.shape, 1) < 4
    fwd_row = lax.broadcasted_iota(jnp.int32, gr.shape, 0) < 4
    return jnp.where(fwd_lane, pre_c, suf_c), jnp.where(fwd_row, pre_r, suf_r)


def _mlstm_chunk(rev, q, k, vt, gc, gr, csum_c, csum_r, c_ref, n_ref, m_ref):
    T = q.shape[0]
    d = 6 if rev else 2
    ii = 4 if rev else 0
    lo = lax.broadcasted_iota(jnp.int32, (T, LANES), 1) < HEAD_DIM
    lo_row = lax.broadcasted_iota(jnp.int32, (1, LANES), 1) < HEAD_DIM
    sub = lax.broadcasted_iota(jnp.int32, (LANES, T), 0) < HEAD_DIM
    key_i = lax.broadcasted_iota(jnp.int32, (T, T), 0)
    qry_i = lax.broadcasted_iota(jnp.int32, (T, T), 1)
    mask = (key_i >= qry_i) if rev else (key_i <= qry_i)
    last = 0 if rev else T - 1
    qb = q.astype(BF16)
    kb = k.astype(BF16)
    vtb = vt.astype(BF16)
    zero = jnp.zeros_like(kb)
    nt = (((1,), (1,)), ((), ()))
    brow, igrow, mloc, numt, dsum = [], [], [], [], []
    for j in range(2):
        brow.append(csum_r[d + j:d + j + 1, :])
        igrow.append(gr[ii + j:ii + j + 1, :])
        ccol = gc[:, ii + j:ii + j + 1] - csum_c[:, d + j:d + j + 1]
        logd = jnp.where(mask, brow[j] + ccol, -jnp.inf)
        mloc.append(jnp.max(logd, axis=0, keepdims=True))
        kj = jnp.where(lo, kb, zero) if j == 0 else jnp.where(lo, zero, kb)
        st = lax.dot_general(kj, qb, nt, preferred_element_type=F32) * jnp.exp(logd - mloc[j])
        numt.append(jnp.dot(vtb, st.astype(BF16), preferred_element_type=F32))
        dsum.append(jnp.sum(st, axis=0, keepdims=True))
    num_loc = jnp.where(sub, numt[0], numt[1])

    n = n_ref[...]
    row8 = lax.broadcasted_iota(jnp.int32, (SUBLANES, LANES), 0)
    lo8 = lax.broadcasted_iota(jnp.int32, (SUBLANES, LANES), 1) < HEAD_DIM
    n8 = jnp.where(row8 == jnp.where(lo8, 0, 1), jnp.broadcast_to(n, (SUBLANES, LANES)), 0.0)
    qn = lax.dot_general(n8.astype(BF16), qb, nt, preferred_element_type=F32)
    hden, wrow, decay, interw, aloc, m_new = [], [], [], [], [], []
    for j in range(2):
        m_prev = m_ref[j:j + 1, 0:1]
        mrow = jnp.maximum(brow[j] + m_prev, mloc[j])
        a = jnp.exp(mloc[j] - mrow)
        iw = jnp.exp(brow[j] + m_prev - mrow)
        dn = a * dsum[j] + iw * qn[j:j + 1, :]
        hden.append(jnp.maximum(jnp.abs(dn), jnp.exp(-mrow)))
        aloc.append(a)
        interw.append(iw)
        mn = mrow[:, last:last + 1]
        btot = brow[j][:, last:last + 1]
        wrow.append(jnp.exp(btot - brow[j] + igrow[j] - mn))
        decay.append(jnp.exp(btot + m_prev - mn))
        m_new.append(mn)
    c = c_ref[...]
    intert = lax.dot_general(c.astype(BF16), qb, nt, preferred_element_type=F32)
    num = jnp.where(sub, aloc[0], aloc[1]) * num_loc + jnp.where(sub, interw[0], interw[1]) * intert
    ht = num / jnp.where(sub, hden[0], hden[1])
    wvt = (vt * jnp.where(sub, wrow[0], wrow[1])).astype(BF16)
    upd = jnp.dot(wvt, kb, preferred_element_type=F32)
    rr = lax.broadcasted_iota(jnp.int32, (LANES, LANES), 0) < HEAD_DIM
    cc = lax.broadcasted_iota(jnp.int32, (LANES, LANES), 1) < HEAD_DIM
    dcol = jnp.where(lax.broadcasted_iota(jnp.int32, (LANES, 1), 0) < HEAD_DIM, decay[0], decay[1])
    c_ref[...] = dcol * c + jnp.where(rr == cc, upd, 0.0)
    rowt = lax.broadcasted_iota(jnp.int32, (SUBLANES, T), 0)
    w8 = jnp.where(rowt == 0, wrow[0], jnp.where(rowt == 1, wrow[1], 0.0))
    r8 = jnp.dot(w8.astype(BF16), kb, preferred_element_type=F32)
    n_ref[...] = jnp.where(lo_row, decay[0] * n + r8[0:1, :], decay[1] * n + r8[1:2, :])
    m_ref[0:1, :] = jnp.broadcast_to(m_new[0], (1, LANES))
    m_ref[1:2, :] = jnp.broadcast_to(m_new[1], (1, LANES))
    return ht


def _mlstm_kernel(qc_ref, kc_ref, vc_ref, oc_ref, gcc_ref, grc_ref,
                  qx_ref, kx_ref, vx_ref, ox_ref, gcx_ref, grx_ref,
                  cwq_ref, cbq_ref, cwk_ref, cbk_ref, gbc_ref, gbr_ref,
                  yc_ref, yx_ref,
                  qs_ref, ks_ref, vt_ref, hf_ref, hb_ref, csc_ref, csr_ref, ct_ref, n_ref, m_ref):
    T = ML_T
    lx = qx_ref.shape[1]
    nx = lx // T
    r_i = lax.broadcasted_iota(jnp.int32, (T, T), 0)
    c_i = lax.broadcasted_iota(jnp.int32, (T, T), 1)
    tri_lo = jnp.where(c_i <= r_i, 1.0, 0.0).astype(BF16)
    tri_up = jnp.where(c_i >= r_i, 1.0, 0.0).astype(BF16)
    ct_ref[...] = jnp.zeros_like(ct_ref)
    n_ref[...] = jnp.zeros_like(n_ref)
    m_ref[...] = jnp.zeros_like(m_ref)
    scale = HEAD_DIM ** -0.5
    cwq, cbq, cwk, cbk = cwq_ref[...], cbq_ref[...], cwk_ref[...], cbk_ref[...]
    gbc, gbr = gbc_ref[...], gbr_ref[...]

    qc = _silu(_conv3(qc_ref[0], cwq, cbq))
    kc = _silu(_conv3(kc_ref[0], cwk, cbk)) * scale
    vct = vc_ref[0].T
    gcc = gcc_ref[0] + gbc
    grc = grc_ref[0] + gbr
    h_ct = None
    csc, csr = _mlstm_cumsums(gcc, grc, tri_lo, tri_up)
    for rev in (False, True):
        di = int(rev)
        ht = _mlstm_chunk(rev, qc, kc, vct, gcc, grc, csc, csr,
                          ct_ref.at[di], n_ref.at[di], m_ref.at[di])
        h_ct = ht if h_ct is None else h_ct + ht
    yc_ref[0] = _group_rms_2x64(h_ct.T * _sigmoid(oc_ref[0]))

    qs_ref[...] = _silu(_conv3(qx_ref[0], cwq, cbq))
    ks_ref[...] = _silu(_conv3(kx_ref[0], cwk, cbk)) * scale
    for c in range(nx):
        rows = slice(c * T, (c + 1) * T)
        vt_ref[:, rows] = vx_ref[0, rows, :].T
        csc_ref[rows, :], csr_ref[:, rows] = _mlstm_cumsums(gcx_ref[0, rows, :] + gbc,
                                                            grx_ref[0, :, rows] + gbr, tri_lo, tri_up)

    def body(c, carry):
        for rev in (False, True):
            di = int(rev)
            cidx = (nx - 1 - c) if rev else c
            r0 = pl.multiple_of(cidx * T, T)
            rows = pl.ds(r0, T)
            ht = _mlstm_chunk(rev, qs_ref[rows, :], ks_ref[rows, :], vt_ref[:, rows],
                              gcx_ref[0, rows, :] + gbc, grx_ref[0, :, rows] + gbr,
                              csc_ref[rows, :], csr_ref[:, rows],
                              ct_ref.at[di], n_ref.at[di], m_ref.at[di])
            if rev:
                hb_ref[:, rows] = ht
            else:
                hf_ref[:, rows] = ht
        return carry

    lax.fori_loop(0, nx, body, 0, unroll=2)
    for c in range(nx):
        rows = slice(c * T, (c + 1) * T)
        h = (hf_ref[:, rows] + hb_ref[:, rows]).T
        yx_ref[0, rows, :] = _group_rms_2x64(h * _sigmoid(ox_ref[0, rows, :]))


def _mlstm(seg_c, seg_x, conv_w, conv_b, gate_bc, gate_br):
    B, lx, _ = seg_x[1].shape
    lc = seg_c[1].shape[1]
    assert lc == ML_T and lx % ML_T == 0

    def seg_specs(L):
        col = lambda off: pl.BlockSpec((1, L, LANES), lambda b, p, off=off: (b, 0, off + p))
        return [col(0), col(2), col(0), col(0), col(0),
                pl.BlockSpec((1, SUBLANES, L), lambda b, p: (b, p, 0))]

    def seg_args(s):
        qk, v, o, g, gt = s
        return [qk, qk, v, o, g, gt]

    wspec = lambda rows, off: pl.BlockSpec((rows, LANES), lambda b, p, off=off: (0, off + p))
    return pl.pallas_call(
        _mlstm_kernel,
        grid=(B, 2),
        in_specs=seg_specs(lc) + seg_specs(lx)
                 + [wspec(3, 0), wspec(1, 0), wspec(3, 2), wspec(1, 2), wspec(1, 0),
                    pl.BlockSpec((SUBLANES, 1), lambda b, p: (p, 0))],
        out_specs=[pl.BlockSpec((1, lc, LANES), lambda b, p: (b, 0, p)),
                   pl.BlockSpec((1, lx, LANES), lambda b, p: (b, 0, p))],
        out_shape=[jax.ShapeDtypeStruct((B, lc, 2 * LANES), F32),
                   jax.ShapeDtypeStruct((B, lx, 2 * LANES), F32)],
        scratch_shapes=[pltpu.VMEM((lx, LANES), F32), pltpu.VMEM((lx, LANES), F32),
                        pltpu.VMEM((LANES, lx), F32), pltpu.VMEM((LANES, lx), F32),
                        pltpu.VMEM((LANES, lx), F32),
                        pltpu.VMEM((lx, LANES), F32), pltpu.VMEM((SUBLANES, lx), F32),
                        pltpu.VMEM((2, LANES, LANES), F32), pltpu.VMEM((2, 1, LANES), F32),
                        pltpu.VMEM((2, 2, LANES), F32)],
        compiler_params=_cparams("parallel", "arbitrary"),
        name="mlstm",
    )(*seg_args(seg_c), *seg_args(seg_x), conv_w, conv_b, conv_w, conv_b, gate_bc, gate_br)


def _outproj_kernel(x_ref, g_ref, hy_ref, da_ref, ml_ref, gain_ref, w_ref, o_ref):
    c_hy = hy_ref.shape[2]
    c_da = da_ref.shape[2]
    gain = gain_ref[...]
    acc = jnp.dot((hy_ref[0] * gain[:, 0:c_hy]).astype(BF16), w_ref[0:c_hy, :], preferred_element_type=F32)
    acc += jnp.dot((da_ref[0] * gain[:, c_hy:c_hy + c_da]).astype(BF16), w_ref[c_hy:c_hy + c_da, :],
                   preferred_element_type=F32)
    acc += jnp.dot((ml_ref[0] * gain[:, c_hy + c_da:]).astype(BF16), w_ref[c_hy + c_da:, :],
                   preferred_element_type=F32)
    o_ref[0] = x_ref[0] + g_ref[0] * acc


def _outproj(x, gate, y_hy, y_da, y_ml, gain, w_out):
    B, L, D = x.shape
    tm = min(TM, L)
    tok = lambda w: pl.BlockSpec((1, tm, w), lambda b, i: (b, i, 0))
    return pl.pallas_call(
        _outproj_kernel,
        grid=(B, L // tm),
        in_specs=[tok(D), pl.BlockSpec((1, 1, D), lambda b, i: (b, 0, 0)),
                  tok(y_hy.shape[2]), tok(y_da.shape[2]), tok(y_ml.shape[2]),
                  pl.BlockSpec(gain.shape, lambda b, i: (0, 0)), _const_spec(w_out.shape)],
        out_specs=tok(D),
        out_shape=jax.ShapeDtypeStruct((B, L, D), F32),
        compiler_params=_cparams("parallel", "parallel"),
        name="outproj",
    )(x, gate, y_hy, y_da, y_ml, gain, w_out)


def _ffn_kernel(final, x_ref, prev_ref, next_ref, sh_ref, sc_ref, g_ref, up_ref, cw_ref, cb_ref,
                down_ref, fw_ref, o_ref, act_ref):
    i = pl.program_id(1)
    last = pl.num_programs(1) - 1
    tm = x_ref.shape[1]
    d_ff = down_ref.shape[0]
    x = x_ref[0]
    mod = lambda a: _rms(a) * (1.0 + sc_ref[0]) + sh_ref[0]
    hp = jnp.where(i == 0, 0.0, mod(prev_ref[0]))
    hn = jnp.where(i == last, 0.0, mod(next_ref[0]))
    h = jnp.concatenate([hp, mod(x), hn], axis=0).astype(BF16)
    ext = tm + 2 * SUBLANES

    def conv_cols(lo, hi):
        u = jnp.dot(h, up_ref[:, lo:hi], preferred_element_type=F32)
        w = cw_ref[:, lo:hi]
        c = (pltpu.roll(u, 1, axis=0) * w[0:1] + u * w[1:2] + pltpu.roll(u, ext - 1, axis=0) * w[2:3]
             + cb_ref[:, lo:hi])
        return c[SUBLANES:SUBLANES + tm]

    for j in range(d_ff // FFN_TN):
        lo = j * FFN_TN
        a = conv_cols(lo, lo + FFN_TN)
        g = conv_cols(d_ff + lo, d_ff + lo + FFN_TN)
        act_ref[:, lo:lo + FFN_TN] = (_silu(g) * a).astype(BF16)
    y = x + g_ref[0] * jnp.dot(act_ref[...], down_ref[...], preferred_element_type=F32)
    if final:
        y = _rms(y) * fw_ref[...]
    o_ref[0] = y


def _ffn(x, shift, scale, gate, up, conv_w, conv_b, down, final_w, final):
    B, L, D = x.shape
    tm = min(TM, L)
    nb = tm // SUBLANES
    nrow = L // SUBLANES
    tok = pl.BlockSpec((1, tm, D), lambda b, i: (b, i, 0))
    mod = pl.BlockSpec((1, 1, D), lambda b, i: (b, 0, 0))
    full = lambda a: pl.BlockSpec(a.shape, lambda b, i: (0,) * a.ndim)
    return pl.pallas_call(
        functools.partial(_ffn_kernel, final),
        grid=(B, L // tm),
        in_specs=[tok,
                  pl.BlockSpec((1, SUBLANES, D), lambda b, i: (b, jnp.maximum(i * nb - 1, 0), 0)),
                  pl.BlockSpec((1, SUBLANES, D), lambda b, i: (b, jnp.minimum((i + 1) * nb, nrow - 1), 0)),
                  mod, mod, mod, _const_spec(up.shape), full(conv_w), full(conv_b),
                  _const_spec(down.shape), full(final_w)],
        out_specs=tok,
        out_shape=jax.ShapeDtypeStruct((B, L, D), F32),
        scratch_shapes=[pltpu.VMEM((tm, down.shape[0]), BF16)],
        compiler_params=_cparams("parallel", "parallel"),
        name="ffn_final" if final else "ffn",
    )(x, x, x, shift, scale, gate, up, conv_w, conv_b, down, final_w)


def _rope_tables(L):
    rows_n = L // GRID_W
    rows = jnp.repeat(jnp.arange(rows_n, dtype=F32), GRID_W)
    cols = jnp.tile(jnp.arange(GRID_W, dtype=F32), rows_n)
    nf = HEAD_DIM // 4
    inv = ROPE_BASE ** (-jnp.arange(nf, dtype=F32) / nf)
    cr, sr = jnp.cos(rows[:, None] * inv), jnp.sin(rows[:, None] * inv)
    cc, sc = jnp.cos(cols[:, None] * inv), jnp.sin(cols[:, None] * inv)
    cos64 = jnp.concatenate([cr, cr, cc, cc], axis=-1)
    sin64 = jnp.concatenate([-sr, sr, -sc, sc], axis=-1)
    return jnp.tile(cos64, (1, 2)), jnp.tile(sin64, (1, 2))


def _hy_features(L):
    t = jnp.linspace(0.0, 1.0, L, dtype=F32)
    pos = jnp.arange(L, dtype=F32)
    f = jnp.linspace(1e-4, HY_BANDS - 1, HY_BANDS, dtype=F32)
    ang = (2.0 * math.pi / L) * pos[:, None] * f
    z = jnp.concatenate([t[:, None], jnp.cos(ang), jnp.sin(ang)], axis=-1)
    return jnp.pad(z, ((0, 0), (0, HY_POS_PAD - z.shape[1])))


def _gate_layout(a):
    g = a.reshape(a.shape[:-1] + (4, 2, 2))
    g = jnp.moveaxis(g, -2, -3).reshape(a.shape[:-1] + (2, 8))
    g = jnp.pad(g, [(0, 0)] * (g.ndim - 1) + [(0, LANES - 8)])
    return g.reshape(a.shape[:-1] + (2 * LANES,))


def _gates_t(g):
    return jnp.swapaxes(jnp.concatenate([g[..., 0:8], g[..., LANES:LANES + 8]], axis=-1), 1, 2)


def kernel(x, c, ctx, c_ctx, ada_w, ada_b, w_in, w_out, hy_conv_w, hy_conv_b, hy_w1, hy_b1, hy_w2, hy_b2,
           hy_w3, hy_b3, hy_skip, da_lambda, ml_conv_w, ml_conv_b, ml_gate_b, mix_norm_w, ffn_up,
           ffn_conv_w, ffn_conv_b, ffn_down, final_norm_w):
    B, L, D = x.shape
    lc = ctx.shape[1]
    depth = ada_w.shape[0]
    hy_w = hy_skip.shape[2]
    n_hy = 3 * hy_w
    da_w = 2 * hy_w
    ml_w = hy_w

    rows = ((B + 1 + SUBLANES - 1) // SUBLANES) * SUBLANES
    cc = jnp.zeros((rows, D), F32).at[:B].set(c).at[B].set(c_ctx)
    mods = _adaln(cc, ada_w, ada_b)

    cos_x, sin_x = _rope_tables(L)
    cos_c, sin_c = jnp.ones((lc, LANES), F32), jnp.zeros((lc, LANES), F32)
    tables = _dft_tables(L)
    fd_c, fi_c = _dense_dft_tables(lc)
    zf_x, zf_c = _hy_features(L), _hy_features(lc)
    delta = jnp.abs(jnp.linspace(math.log(HY_DECAY_TARGET) / HY_FAST_DECAY,
                                 math.log(HY_DECAY_TARGET) / HY_SLOW_DECAY, hy_w, dtype=F32))[None, :]
    swap = np.arange(da_w) ^ (HEAD_DIM // 4)
    fw = final_norm_w[None, :]

    for l in range(depth):
        lam_init = 0.8 - 0.6 * math.exp(-0.3 * l)
        update_ctx = l < depth - 1
        mx = [m[:, None, :] for m in jnp.split(mods[l, :B], 6, axis=-1)]
        mc = [jnp.broadcast_to(m[None], (B, 1, D)) for m in jnp.split(mods[l, B:B + 1], 6, axis=-1)]

        w = w_in[l]
        o = n_hy
        wq, wk, wv = w[:, o:o + da_w], w[:, o + da_w:o + 2 * da_w], w[:, o + 2 * da_w:o + 3 * da_w]
        o += 3 * da_w
        wmqk, wmv, wmo = w[:, o:o + 2 * ml_w], w[:, o + 2 * ml_w:o + 3 * ml_w], w[:, o + 3 * ml_w:o + 4 * ml_w]
        wg = _gate_layout(w[:, o + 4 * ml_w:])
        w_ext = jnp.concatenate([w[:, :n_hy], wq, wq[:, swap], wk, wk[:, swap], wv, wmqk, wmv, wmo, wg],
                                axis=1).astype(BF16)
        gate_b = _gate_layout(ml_gate_b[l])
        gate_bc = gate_b[None, :]
        gate_br = _gates_t(gate_b[None, None, :])[0]

        px = _inproj(x, mx[0], mx[1], w_ext, cos_x, sin_x)
        pc = _inproj(ctx, mc[0], mc[1], w_ext, cos_c, sin_c)
        hy_x, q_x, k_x, v_x, mqk_x, mv_x, mo_x, g_x = px
        hy_c, q_c, k_c, v_c, mqk_c, mv_c, mo_c, g_c = pc

        w1p = jnp.pad(hy_w1[l], ((0, HY_POS_PAD - hy_w1.shape[1]), (0, 0)))
        b1, b2, b3 = hy_b1[l][None, :], hy_b2[l][None, :], hy_b3[l][None, :]
        cbias = hy_conv_b[l][None, :]
        kf = _hy_filter(zf_x, w1p, b1, hy_w2[l], b2, hy_w3[l], b3, delta, tables[0], tables[1])
        z1 = _hy_conv(0, hy_x, 0, hy_x, hy_conv_w[l], cbias, hy_skip[l], kf, tables)
        y_hy_x = _hy_conv(1, z1, 0, hy_x, hy_conv_w[l], cbias, hy_skip[l], kf, tables)

        y_da_x = _diff_attn(q_x, [k_c, k_x], [v_c, v_x], da_lambda[l], lam_init)

        seg_c = (mqk_c, mv_c, mo_c, g_c, _gates_t(g_c))
        seg_x = (mqk_x, mv_x, mo_x, g_x, _gates_t(g_x))
        y_ml_c, y_ml_x = _mlstm(seg_c, seg_x, ml_conv_w[l], ml_conv_b[l][None, :], gate_bc, gate_br)

        gain = mix_norm_w[l][None, :]
        wo = w_out[l].astype(BF16)
        up = ffn_up[l].astype(BF16)
        down = ffn_down[l].astype(BF16)
        fcb = ffn_conv_b[l][None, :]
        x = _outproj(x, mx[2], y_hy_x, y_da_x, y_ml_x, gain, wo)
        x = _ffn(x, mx[3], mx[4], mx[5], up, ffn_conv_w[l], fcb, down, fw, final=not update_ctx)

        if update_ctx:
            y_hy_c = _hy_ctx(hy_c, zf_c, w1p, b1, hy_w2[l], b2, hy_w3[l], b3, delta,
                             hy_conv_w[l], cbias, hy_skip[l], fd_c, fi_c)
            y_da_c = _diff_attn(q_c, [k_c], [v_c], da_lambda[l], lam_init)
            ctx = _outproj(ctx, mc[2], y_hy_c, y_da_c, y_ml_c, gain, wo)
            ctx = _ffn(ctx, mc[3], mc[4], mc[5], up, ffn_conv_w[l], fcb, down, fw, final=False)
    return x
```

```python
import functools
import math

import numpy as np
import jax
import jax.numpy as jnp
from jax import lax
from jax.experimental import pallas as pl
from jax.experimental.pallas import tpu as pltpu

F32 = jnp.float32
BF16 = jnp.bfloat16
HI = lax.Precision.HIGHEST

EPS = 1e-6
HEAD_DIM = 64
GRID_W = 64
ROPE_BASE = 10000.0
HY_BANDS = 16
HY_POS_PAD = 64
HY_DECAY_TARGET = 1e-2
HY_FAST_DECAY = 0.3
HY_SLOW_DECAY = 1.5
LANES = 128
SUBLANES = 8
VMEM_LIMIT = 56 * 1024 * 1024

FFT_N1 = 64
FFT_N2 = 128
FFT_UNROLL = 8
ML_T = 256
TQ = 1024
TQ_SUB = 256
TK = 512
TM = 512
FFN_TN = 256


def _cparams(*sem):
    return pltpu.CompilerParams(dimension_semantics=sem, vmem_limit_bytes=VMEM_LIMIT)


def _const_spec(shape):
    n = len(shape)
    return pl.BlockSpec(shape, lambda *_: (0,) * n, pipeline_mode=pl.Buffered(1))


def _silu(x):
    return x / (1.0 + jnp.exp(-x))


def _sigmoid(x):
    return 1.0 / (1.0 + jnp.exp(-x))


def _log_sigmoid(x):
    return jnp.minimum(x, 0.0) - jnp.log(1.0 + jnp.exp(-jnp.abs(x)))


def _rms(x):
    return x * lax.rsqrt(jnp.mean(x * x, axis=-1, keepdims=True) + EPS)


def _group_rms_2x64(y):
    lo = lax.broadcasted_iota(jnp.int32, y.shape, 1) < 64
    y2 = y * y
    s_lo = jnp.sum(jnp.where(lo, y2, 0.0), axis=-1, keepdims=True)
    s_hi = jnp.sum(jnp.where(lo, 0.0, y2), axis=-1, keepdims=True)
    r = jnp.where(lo, lax.rsqrt(s_lo * (1.0 / 64) + EPS), lax.rsqrt(s_hi * (1.0 / 64) + EPS))
    return y * r


def _conv3(x, w, b):
    L = x.shape[0]
    row = lax.broadcasted_iota(jnp.int32, x.shape, 0)
    xm = jnp.where(row == 0, 0.0, pltpu.roll(x, 1, axis=0))
    xp = jnp.where(row == L - 1, 0.0, pltpu.roll(x, L - 1, axis=0))
    return xm * w[0:1] + x * w[1:2] + xp * w[2:3] + b


def _adaln_kernel(c_ref, w_ref, b_ref, o_ref):
    a = _silu(c_ref[...])
    o_ref[0] = jnp.dot(a, w_ref[0], precision=HI, preferred_element_type=F32) + b_ref[0]


def _adaln(cc, ada_w, ada_b):
    depth, d, n6 = ada_w.shape
    tn = 1536
    return pl.pallas_call(
        _adaln_kernel,
        grid=(depth, n6 // tn),
        in_specs=[pl.BlockSpec(cc.shape, lambda l, j: (0, 0)),
                  pl.BlockSpec((1, d, tn), lambda l, j: (l, 0, j)),
                  pl.BlockSpec((1, 1, tn), lambda l, j: (l, 0, j))],
        out_specs=pl.BlockSpec((1, cc.shape[0], tn), lambda l, j: (l, 0, j)),
        out_shape=jax.ShapeDtypeStruct((depth, cc.shape[0], n6), F32),
        compiler_params=_cparams("parallel", "parallel"),
        name="adaln",
    )(cc, ada_w, ada_b.reshape(depth, 1, n6))


_C_HY, _C_Q, _C_QS, _C_K, _C_KS, _C_V, _C_MQK, _C_MV, _C_MO, _C_G, _C_END = (
    0, 768, 1280, 1792, 2304, 2816, 3328, 3840, 4096, 4352, 4608)


def _inproj_kernel(x_ref, sh_ref, sc_ref, w_ref, cos_ref, sin_ref,
                   hy_ref, q_ref, k_ref, v_ref, mqk_ref, mv_ref, mo_ref, g_ref):
    h = (_rms(x_ref[0]) * (1.0 + sc_ref[0]) + sh_ref[0]).astype(BF16)

    def proj(lo, hi):
        return jnp.dot(h, w_ref[:, lo:hi], preferred_element_type=F32)

    hy_ref[0] = proj(_C_HY, _C_Q)
    cos = cos_ref[...]
    sin = sin_ref[...]

    def rope(c0, c1, out_ref, scale):
        a = proj(c0, c1)
        asw = proj(c1, 2 * c1 - c0)
        for j in range((c1 - c0) // LANES):
            sl = slice(LANES * j, LANES * (j + 1))
            out_ref[0, :, sl] = ((a[:, sl] * cos + asw[:, sl] * sin) * scale).astype(BF16)

    rope(_C_Q, _C_QS, q_ref, (HEAD_DIM ** -0.5) * math.log2(math.e))
    rope(_C_K, _C_KS, k_ref, 1.0)
    v_ref[0] = proj(_C_V, _C_MQK).astype(BF16)
    mqk_ref[0] = proj(_C_MQK, _C_MV)
    mv_ref[0] = proj(_C_MV, _C_MO)
    mo_ref[0] = proj(_C_MO, _C_G)
    g_ref[0] = proj(_C_G, _C_END)


def _inproj(x, shift, scale, w_ext, cos, sin):
    B, L, D = x.shape
    tm = min(TM, L)
    tok = lambda w: pl.BlockSpec((1, tm, w), lambda b, i: (b, i, 0))
    mod = pl.BlockSpec((1, 1, D), lambda b, i: (b, 0, 0))
    tab = pl.BlockSpec((tm, LANES), lambda b, i: (i, 0))
    widths = (768, 512, 512, 512, 512, 256, 256, 256)
    dtypes = (F32, BF16, BF16, BF16, F32, F32, F32, F32)
    return pl.pallas_call(
        _inproj_kernel,
        grid=(B, L // tm),
        in_specs=[tok(D), mod, mod, _const_spec(w_ext.shape), tab, tab],
        out_specs=[tok(w) for w in widths],
        out_shape=[jax.ShapeDtypeStruct((B, L, w), dt) for w, dt in zip(widths, dtypes)],
        compiler_params=_cparams("parallel", "parallel"),
        name="inproj",
    )(x, shift, scale, w_ext, cos, sin)


def _dft_tables(L):
    n1, n2, g = FFT_N1, FFT_N2, SUBLANES
    N = n1 * n2
    assert N == 2 * L
    k1 = np.arange(n1)[:, None]
    m1 = np.arange(n1 // 2)[None, :]
    th = 2 * np.pi * ((k1 * m1) % n1) / n1
    f = np.stack([np.cos(th), -np.sin(th)], axis=1)
    eye = np.eye(g)
    m1f = np.einsum('krn,gh->krgnh', f, eye).reshape(2 * n1 * g, (n1 // 2) * g)
    m1i = np.einsum('krn,gh->ngkrh', f, eye).reshape((n1 // 2) * g, 2 * n1 * g)
    a = np.arange(n2)
    th2 = 2 * np.pi * ((a[:, None] * a[None, :]) % n2) / n2
    fr, fi = np.cos(th2), -np.sin(th2)
    tw = 2 * np.pi * ((np.arange(n1)[:, None] * a[None, :]) % N) / N
    tr, ti = np.cos(tw)[:, None, :], -np.sin(tw)[:, None, :]
    gr = fr[None] * tr - fi[None] * ti
    gi = fr[None] * ti + fi[None] * tr
    gf = np.concatenate([np.concatenate([gr, -gi], axis=2),
                         np.concatenate([gi, gr], axis=2)], axis=1)
    gb = np.transpose(gf, (0, 2, 1))
    cvt = lambda m: jnp.asarray(m.astype(np.float32)).astype(BF16)
    return cvt(m1f), cvt(gf), cvt(gb), cvt(m1i)


def _fft_block_fwd(src3_ref, a_ref, m1f_ref):
    nh, n2, c = src3_ref.shape
    m1f = m1f_ref[...]
    group = lambda g: slice(SUBLANES * g, SUBLANES * (g + 1))
    for g in range(0, n2 // SUBLANES, 2):
        xg = jnp.concatenate([src3_ref[:, group(g + i), :].reshape(nh * SUBLANES, c) for i in range(2)],
                             axis=1).astype(BF16)
        out = jnp.dot(m1f, xg, preferred_element_type=F32)
        for i in range(2):
            a_ref[:, group(g + i), :] = out[:, c * i:c * (i + 1)].reshape(4 * nh, SUBLANES, c)


def _fft_block_inv(a_ref, dst3_ref, m1i_ref):
    nh, n2, c = dst3_ref.shape
    m1i = m1i_ref[...]
    group = lambda g: slice(SUBLANES * g, SUBLANES * (g + 1))
    for g in range(0, n2 // SUBLANES, 2):
        ag = jnp.concatenate([a_ref[:, group(g + i), :].reshape(4 * nh * SUBLANES, c) for i in range(2)],
                             axis=1).astype(BF16)
        out = jnp.dot(m1i, ag, preferred_element_type=F32)
        for i in range(2):
            dst3_ref[:, group(g + i), :] = out[:, c * i:c * (i + 1)].reshape(nh, SUBLANES, c)


def _hy_mlp(z_ref, w1_ref, b1_ref, w2_ref, b2_ref):
    h = jnp.sin(jnp.dot(z_ref[...], w1_ref[...], precision=HI, preferred_element_type=F32) + b1_ref[...])
    return jnp.sin(jnp.dot(h, w2_ref[...], precision=HI, preferred_element_type=F32) + b2_ref[...])


def _hy_taps(h2, w3_ref, b3_ref, delta_ref, zero_first):
    L = h2.shape[0]
    h = jnp.dot(h2, w3_ref[...], precision=HI, preferred_element_type=F32) + b3_ref[...]
    row = lax.broadcasted_iota(jnp.int32, h.shape, 0)
    t = row.astype(F32) * (1.0 / (L - 1))
    h = h * jnp.exp(-t * delta_ref[...])
    if zero_first:
        h = jnp.where(row == 0, 0.0, h)
    return h


def _hy_filter_kernel(z_ref, w1_ref, b1_ref, w2_ref, b2_ref, w3f_ref, b3f_ref, w3b_ref, b3b_ref,
                      delta_ref, m1f_ref, gf_ref, kf_ref, src_ref, af_ref, ab_ref):
    h2 = _hy_mlp(z_ref, w1_ref, b1_ref, w2_ref, b2_ref)
    hf = _hy_taps(h2, w3f_ref, b3f_ref, delta_ref, False)
    hb = _hy_taps(h2, w3b_ref, b3b_ref, delta_ref, True)
    L, c = hf.shape
    n_total = 2.0 * L
    inv = 1.0 / ((jnp.sum(jnp.abs(hf), axis=0, keepdims=True)
                  + jnp.sum(jnp.abs(hb), axis=0, keepdims=True)) * n_total)
    nh, n2 = src_ref.shape[0], src_ref.shape[1]
    src_ref[...] = (hf * inv).reshape(nh, n2, c)
    _fft_block_fwd(src_ref, af_ref, m1f_ref)
    src_ref[...] = (hb * inv).reshape(nh, n2, c)
    _fft_block_fwd(src_ref, ab_ref, m1f_ref)

    def body(k1, carry):
        sl = pl.ds(pl.multiple_of(2 * k1, 2), 2)
        g = gf_ref[k1]
        xf = jnp.dot(g, af_ref[sl].reshape(2 * n2, c).astype(BF16), preferred_element_type=F32)
        xb = jnp.dot(g, ab_ref[sl].reshape(2 * n2, c).astype(BF16), preferred_element_type=F32)
        kr = xf[:n2] + xb[:n2]
        ki = xf[n2:] - xb[n2:]
        kf_ref[0, sl] = jnp.concatenate([kr, ki], axis=0).reshape(2, n2, c).astype(BF16)
        return carry

    lax.fori_loop(0, gf_ref.shape[0], body, 0, unroll=FFT_UNROLL)


def _hy_filter(zfeat, w1, b1, w2, b2, w3, b3, delta, m1f, gf):
    L = zfeat.shape[0]
    C = delta.shape[1]
    cb = LANES
    ncb = C // cb
    nh, n2 = FFT_N1 // 2, FFT_N2
    full = lambda a: pl.BlockSpec(a.shape, lambda o, j: (0,) * a.ndim)
    return pl.pallas_call(
        _hy_filter_kernel,
        grid=(2, ncb),
        in_specs=[full(zfeat), full(w1), full(b1), full(w2), full(b2),
                  pl.BlockSpec((w3.shape[0], cb), lambda o, j: (0, o * 2 * ncb + j)),
                  pl.BlockSpec((1, cb), lambda o, j: (0, o * 2 * ncb + j)),
                  pl.BlockSpec((w3.shape[0], cb), lambda o, j: (0, o * 2 * ncb + ncb + j)),
                  pl.BlockSpec((1, cb), lambda o, j: (0, o * 2 * ncb + ncb + j)),
                  pl.BlockSpec((1, cb), lambda o, j: (0, j)),
                  _const_spec(m1f.shape), _const_spec(gf.shape)],
        out_specs=pl.BlockSpec((1, 4 * nh, n2, cb), lambda o, j: (o, 0, 0, j)),
        out_shape=jax.ShapeDtypeStruct((2, 4 * nh, n2, C), BF16),
        scratch_shapes=[pltpu.VMEM((nh, n2, cb), F32), pltpu.VMEM((4 * nh, n2, cb), F32),
                        pltpu.VMEM((4 * nh, n2, cb), F32)],
        compiler_params=_cparams("arbitrary", "arbitrary"),
        name="hy_filter",
    )(zfeat, w1, b1, w2, b2, w3, b3, w3, b3, delta, m1f, gf)


def _hy_conv_kernel(conv_z, norm_out, z_ref, g_ref, cwz_ref, cbz_ref, cwg_ref, cbg_ref, skip_ref,
                    kf_ref, m1f_ref, gf_ref, gb_ref, m1i_ref, o_ref, zs_ref, ys_ref, a_ref):
    nh, n2, c = zs_ref.shape
    z = z_ref[0]
    if conv_z:
        z = _conv3(z, cwz_ref[...], cbz_ref[...])
    zs_ref[...] = z.reshape(nh, n2, c)
    _fft_block_fwd(zs_ref, a_ref, m1f_ref)

    def body(k1, carry):
        sl = pl.ds(pl.multiple_of(2 * k1, 2), 2)
        x = jnp.dot(gf_ref[k1], a_ref[sl].reshape(2 * n2, c).astype(BF16), preferred_element_type=F32)
        kf = kf_ref[0, sl].astype(F32)
        xr, xi, kr, ki = x[:n2], x[n2:], kf[0], kf[1]
        y = jnp.concatenate([xr * kr - xi * ki, xr * ki + xi * kr], axis=0).astype(BF16)
        a_ref[sl] = jnp.dot(gb_ref[k1], y, preferred_element_type=F32).reshape(2, n2, c)
        return carry

    lax.fori_loop(0, gf_ref.shape[0], body, 0, unroll=FFT_UNROLL)
    _fft_block_inv(a_ref, ys_ref, m1i_ref)
    y = ys_ref[...].reshape(nh * n2, c)
    z = zs_ref[...].reshape(nh * n2, c)
    gate = _conv3(g_ref[0], cwg_ref[...], cbg_ref[...])
    out = gate * (y + skip_ref[...] * z)
    if norm_out:
        out = _group_rms_2x64(out)
    o_ref[0] = out


def _hy_conv(order, z, zcol, p_hy, conv_w, conv_b, skip, kf, tables):
    m1f, gf, gb, m1i = tables
    B, L, _ = p_hy.shape
    C = skip.shape[1]
    cb = LANES
    ncb = C // cb
    nh, n2 = FFT_N1 // 2, FFT_N2
    gcol = (order + 1) * ncb
    cw = lambda base: pl.BlockSpec((3, cb), lambda j, b: (0, base + j))
    cbs = lambda base: pl.BlockSpec((1, cb), lambda j, b: (0, base + j))
    return pl.pallas_call(
        functools.partial(_hy_conv_kernel, order == 0, order == 1),
        grid=(ncb, B),
        in_specs=[pl.BlockSpec((1, L, cb), lambda j, b: (b, 0, zcol + j)),
                  pl.BlockSpec((1, L, cb), lambda j, b: (b, 0, gcol + j)),
                  cw(0), cbs(0), cw(gcol), cbs(gcol),
                  pl.BlockSpec((1, cb), lambda j, b: (0, j)),
                  pl.BlockSpec((1, 4 * nh, n2, cb), lambda j, b: (order, 0, 0, j)),
                  _const_spec(m1f.shape), _const_spec(gf.shape), _const_spec(gb.shape),
                  _const_spec(m1i.shape)],
        out_specs=pl.BlockSpec((1, L, cb), lambda j, b: (b, 0, j)),
        out_shape=jax.ShapeDtypeStruct((B, L, C), F32),
        scratch_shapes=[pltpu.VMEM((nh, n2, cb), F32), pltpu.VMEM((nh, n2, cb), F32),
                        pltpu.VMEM((4 * nh, n2, cb), F32)],
        compiler_params=_cparams("arbitrary", "arbitrary"),
        name=f"hy_conv{order}",
    )(z, p_hy, conv_w, conv_b, conv_w, conv_b, skip[order:order + 1], kf, m1f, gf, gb, m1i)


def _hy_ctx_kernel(p_ref, z_ref, w1_ref, b1_ref, w2_ref, b2_ref, w3_ref, b3_ref, delta_ref,
                   cw_ref, cb_ref, skip_ref, fd_ref, fi_ref, o_ref):
    lc = p_ref.shape[1]
    C = skip_ref.shape[1]
    h2 = _hy_mlp(z_ref, w1_ref, b1_ref, w2_ref, b2_ref)
    u = _conv3(p_ref[0], cw_ref[...], cb_ref[...])
    z = u[:, 0:C]
    fd = fd_ref[...]
    fi = fi_ref[...]
    nf = fd.shape[0] // 2
    dot = lambda a, b: jnp.dot(a.astype(BF16), b.astype(BF16), preferred_element_type=F32)
    for o in range(2):
        base = 2 * o * C
        hf = _hy_taps(h2, w3_ref.at[:, base:base + C], b3_ref.at[:, base:base + C], delta_ref, False)
        hb = _hy_taps(h2, w3_ref.at[:, base + C:base + 2 * C], b3_ref.at[:, base + C:base + 2 * C],
                      delta_ref, True)
        inv = 1.0 / ((jnp.sum(jnp.abs(hf), axis=0, keepdims=True)
                      + jnp.sum(jnp.abs(hb), axis=0, keepdims=True)) * (2.0 * lc))
        xf = dot(fd, hf * inv)
        xb = dot(fd, hb * inv)
        kr = xf[:nf] + xb[:nf]
        ki = xf[nf:] - xb[nf:]
        x = dot(fd, z)
        xr, xi = x[:nf], x[nf:]
        y = dot(fi, jnp.concatenate([xr * kr - xi * ki, xr * ki + xi * kr], axis=0))
        z = u[:, (o + 1) * C:(o + 2) * C] * (y + skip_ref[o:o + 1, :] * z)
    for j in range(C // LANES):
        o_ref[0, :, j * LANES:(j + 1) * LANES] = _group_rms_2x64(z[:, j * LANES:(j + 1) * LANES])


def _dense_dft_tables(lc):
    n = 2 * lc
    k = np.arange(n)[:, None]
    t = np.arange(lc)[None, :]
    th = 2 * np.pi * ((k * t) % n) / n
    fd = np.concatenate([np.cos(th), -np.sin(th)], axis=0)
    fi = np.concatenate([np.cos(th).T, -np.sin(th).T], axis=1)
    return jnp.asarray(fd.astype(np.float32)), jnp.asarray(fi.astype(np.float32))


def _hy_ctx(p_hy, zfeat, w1, b1, w2, b2, w3, b3, delta, conv_w, conv_b, skip, fd, fi):
    B, lc, w = p_hy.shape
    C = skip.shape[1]
    full = lambda a: pl.BlockSpec(a.shape, lambda b: (0,) * a.ndim)
    args = (zfeat, w1, b1, w2, b2, w3, b3, delta, conv_w, conv_b, skip, fd, fi)
    return pl.pallas_call(
        _hy_ctx_kernel,
        grid=(B,),
        in_specs=[pl.BlockSpec((1, lc, w), lambda b: (b, 0, 0))] + [full(a) for a in args],
        out_specs=pl.BlockSpec((1, lc, C), lambda b: (b, 0, 0)),
        out_shape=jax.ShapeDtypeStruct((B, lc, C), F32),
        compiler_params=_cparams("parallel"),
        name="hy_ctx",
    )(p_hy, *args)


def _attn_kernel(seg_lens, lam_init, q_ref, lam_ref, *refs):
    nseg = len(seg_lens)
    k_refs, v_refs = refs[:nseg], refs[nseg:2 * nseg]
    o_ref = refs[2 * nseg]
    n_s, n_p = 3, 2
    s_refs = refs[2 * nseg + 1:2 * nseg + 1 + n_s]
    p_flat = refs[2 * nseg + 1 + n_s:2 * nseg + 1 + n_s + 2 * n_p]
    p_refs = [p_flat[2 * i:2 * i + 2] for i in range(n_p)]
    tq = q_ref.shape[1]
    th = s_refs[0].shape[0]
    dl = lam_ref[...]
    lam = (jnp.exp(jnp.sum(dl[0:1] * dl[1:2], keepdims=True))
           - jnp.exp(jnp.sum(dl[2:3] * dl[3:4], keepdims=True))) + lam_init

    chunks = []
    off = 0
    for s, n in enumerate(seg_lens):
        for st in range(0, n, TK):
            w = min(TK, n - st)
            chunks.append((s, st, off, w))
            off += w

    nh = tq // th

    def lane_tiles(a):
        return [a[:, t * LANES:(t + 1) * LANES] for t in range(a.shape[1] // LANES)]

    def scores(h, m):
        q = q_ref[0, h * th:(h + 1) * th, :]
        lane = lax.broadcasted_iota(jnp.int32, q.shape, 1)
        qm = jnp.where((lane < HEAD_DIM) == (m == 0), q, jnp.zeros_like(q))
        s_ref = s_refs[(2 * h + m) % n_s]
        mxw = jnp.full((th, LANES), -jnp.inf, F32)
        for (s, st, o, w) in chunks:
            sc = lax.dot_general(qm, k_refs[s][0, st:st + w, :], (((1,), (1,)), ((), ())),
                                 preferred_element_type=F32)
            s_ref[:, o:o + w] = sc
            for t in lane_tiles(sc):
                mxw = jnp.maximum(mxw, t)
        return jnp.max(mxw, axis=-1, keepdims=True)

    def probs(h, m, mx):
        s_ref = s_refs[(2 * h + m) % n_s]
        p_ref = p_refs[h % n_p][m]
        lw = jnp.zeros((th, LANES), F32)
        for (s, st, o, w) in chunks:
            p = jnp.exp2(s_ref[:, o:o + w] - mx)
            for t in lane_tiles(p):
                lw = lw + t
            p_ref[:, o:o + w] = p.astype(BF16)
        return 1.0 / jnp.sum(lw, axis=-1, keepdims=True)

    def output(h, inv_l0, inv_l1):
        p0_ref, p1_ref = p_refs[h % n_p]
        r = jnp.broadcast_to(lam * inv_l1 / inv_l0, (th, LANES)).astype(BF16)
        acc = jnp.zeros((th, v_refs[0].shape[2]), F32)
        for (s, st, o, w) in chunks:
            rb = jnp.concatenate([r] * (w // LANES), axis=1)
            wgt = p0_ref[:, o:o + w] - p1_ref[:, o:o + w] * rb
            acc = acc + jnp.dot(wgt, v_refs[s][0, st:st + w, :], preferred_element_type=F32)
        o_ref[0, h * th:(h + 1) * th, :] = _rms(acc * inv_l0) * (1.0 - lam_init)

    mx = {(0, 0): scores(0, 0), (0, 1): scores(0, 1)}
    for h in range(nh):
        il0 = probs(h, 0, mx[h, 0])
        if h + 1 < nh:
            mx[h + 1, 0] = scores(h + 1, 0)
        il1 = probs(h, 1, mx[h, 1])
        if h + 1 < nh:
            mx[h + 1, 1] = scores(h + 1, 1)
        output(h, il0, il1)


def _diff_attn(q, ks, vs, da_lambda, lam_init):
    B, lq, w = q.shape
    H = w // LANES
    tq = min(TQ, lq)
    th = min(TQ_SUB, tq // 2)
    seg_lens = tuple(k.shape[1] for k in ks)
    seg = lambda n: pl.BlockSpec((1, n, LANES), lambda b, h, i: (b, 0, h))
    return pl.pallas_call(
        functools.partial(_attn_kernel, seg_lens, lam_init),
        grid=(B, H, lq // tq),
        in_specs=[pl.BlockSpec((1, tq, LANES), lambda b, h, i: (b, i, h)),
                  pl.BlockSpec(da_lambda.shape, lambda b, h, i: (0, 0))]
                 + [seg(n) for n in seg_lens] + [seg(n) for n in seg_lens],
        out_specs=pl.BlockSpec((1, tq, LANES), lambda b, h, i: (b, i, h)),
        out_shape=jax.ShapeDtypeStruct((B, lq, w), F32),
        scratch_shapes=[pltpu.VMEM((th, sum(seg_lens)), F32)] * 3
                       + [pltpu.VMEM((th, sum(seg_lens)), BF16)] * 4,
        compiler_params=_cparams("parallel", "parallel", "arbitrary"),
        name="diff_attn",
    )(q, da_lambda, *ks, *vs)


def _split3(x):
    hi = x.astype(BF16)
    r1 = x - hi.astype(F32)
    mid = r1.astype(BF16)
    return hi, mid, (r1 - mid.astype(F32)).astype(BF16)


def _mlstm_cumsums(gc, gr, tri_lo, tri_up):
    dot = lambda a, b: jnp.dot(a, b, preferred_element_type=F32)
    lc = _split3(_log_sigmoid(gc))
    lr = _split3(_log_sigmoid(gr))
    pre_c = sum(dot(tri_lo, t) for t in lc)
    suf_c = sum(dot(tri_up, t) for t in lc)
    pre_r = sum(dot(t, tri_up) for t in lr)
    suf_r = sum(dot(t, tri_lo) for t in lr)
    fwd_lane = lax.broadcasted_iota(jnp.int32, gc.shape, 1) < 4
    fwd_row = lax.broadcasted_iota(jnp.int32, gr.shape, 0) < 4
    return jnp.where(fwd_lane, pre_c, suf_c), jnp.where(fwd_row, pre_r, suf_r)


def _mlstm_chunk(rev, q, k, vt, gc, gr, csum_c, csum_r, c_ref, n_ref, m_ref):
    T = q.shape[0]
    d = 6 if rev else 2
    ii = 4 if rev else 0
    lo = lax.broadcasted_iota(jnp.int32, (T, LANES), 1) < HEAD_DIM
    lo_row = lax.broadcasted_iota(jnp.int32, (1, LANES), 1) < HEAD_DIM
    sub = lax.broadcasted_iota(jnp.int32, (LANES, T), 0) < HEAD_DIM
    key_i = lax.broadcasted_iota(jnp.int32, (T, T), 0)
    qry_i = lax.broadcasted_iota(jnp.int32, (T, T), 1)
    mask = (key_i >= qry_i) if rev else (key_i <= qry_i)
    last = 0 if rev else T - 1
    qb = q.astype(BF16)
    kb = k.astype(BF16)
    vtb = vt.astype(BF16)
    zero = jnp.zeros_like(kb)
    nt = (((1,), (1,)), ((), ()))
    brow, igrow, mloc, numt, dsum = [], [], [], [], []
    for j in range(2):
        brow.append(csum_r[d + j:d + j + 1, :])
        igrow.append(gr[ii + j:ii + j + 1, :])
        ccol = gc[:, ii + j:ii + j + 1] - csum_c[:, d + j:d + j + 1]
        logd = jnp.where(mask, brow[j] + ccol, -jnp.inf)
        mloc.append(jnp.max(logd, axis=0, keepdims=True))
        kj = jnp.where(lo, kb, zero) if j == 0 else jnp.where(lo, zero, kb)
        st = lax.dot_general(kj, qb, nt, preferred_element_type=F32) * jnp.exp(logd - mloc[j])
        numt.append(jnp.dot(vtb, st.astype(BF16), preferred_element_type=F32))
        dsum.append(jnp.sum(st, axis=0, keepdims=True))
    num_loc = jnp.where(sub, numt[0], numt[1])

    n = n_ref[...]
    row8 = lax.broadcasted_iota(jnp.int32, (SUBLANES, LANES), 0)
    lo8 = lax.broadcasted_iota(jnp.int32, (SUBLANES, LANES), 1) < HEAD_DIM
    n8 = jnp.where(row8 == jnp.where(lo8, 0, 1), jnp.broadcast_to(n, (SUBLANES, LANES)), 0.0)
    qn = lax.dot_general(n8.astype(BF16), qb, nt, preferred_element_type=F32)
    hden, wrow, decay, interw, aloc, m_new = [], [], [], [], [], []
    for j in range(2):
        m_prev = m_ref[j:j + 1, 0:1]
        mrow = jnp.maximum(brow[j] + m_prev, mloc[j])
        a = jnp.exp(mloc[j] - mrow)
        iw = jnp.exp(brow[j] + m_prev - mrow)
        dn = a * dsum[j] + iw * qn[j:j + 1, :]
        hden.append(jnp.maximum(jnp.abs(dn), jnp.exp(-mrow)))
        aloc.append(a)
        interw.append(iw)
        mn = mrow[:, last:last + 1]
        btot = brow[j][:, last:last + 1]
        wrow.append(jnp.exp(btot - brow[j] + igrow[j] - mn))
        decay.append(jnp.exp(btot + m_prev - mn))
        m_new.append(mn)
    c = c_ref[...]
    intert = lax.dot_general(c.astype(BF16), qb, nt, preferred_element_type=F32)
    num = jnp.where(sub, aloc[0], aloc[1]) * num_loc + jnp.where(sub, interw[0], interw[1]) * intert
    ht = num / jnp.where(sub, hden[0], hden[1])
    wvt = (vt * jnp.where(sub, wrow[0], wrow[1])).astype(BF16)
    upd = jnp.dot(wvt, kb, preferred_element_type=F32)
    rr = lax.broadcasted_iota(jnp.int32, (LANES, LANES), 0) < HEAD_DIM
    cc = lax.broadcasted_iota(jnp.int32, (LANES, LANES), 1) < HEAD_DIM
    dcol = jnp.where(lax.broadcasted_iota(jnp.int32, (LANES, 1), 0) < HEAD_DIM, decay[0], decay[1])
    c_ref[...] = dcol * c + jnp.where(rr == cc, upd, 0.0)
    rowt = lax.broadcasted_iota(jnp.int32, (SUBLANES, T), 0)
    w8 = jnp.where(rowt == 0, wrow[0], jnp.where(rowt == 1, wrow[1], 0.0))
    r8 = jnp.dot(w8.astype(BF16), kb, preferred_element_type=F32)
    n_ref[...] = jnp.where(lo_row, decay[0] * n + r8[0:1, :], decay[1] * n + r8[1:2, :])
    m_ref[0:1, :] = jnp.broadcast_to(m_new[0], (1, LANES))
    m_ref[1:2, :] = jnp.broadcast_to(m_new[1], (1, LANES))
    return ht


def _mlstm_kernel(qc_ref, kc_ref, vc_ref, oc_ref, gcc_ref, grc_ref,
                  qx_ref, kx_ref, vx_ref, ox_ref, gcx_ref, grx_ref,
                  cwq_ref, cbq_ref, cwk_ref, cbk_ref, gbc_ref, gbr_ref,
                  yc_ref, yx_ref,
                  qs_ref, ks_ref, vt_ref, hf_ref, hb_ref, csc_ref, csr_ref, ct_ref, n_ref, m_ref):
    T = ML_T
    lx = qx_ref.shape[1]
    nx = lx // T
    r_i = lax.broadcasted_iota(jnp.int32, (T, T), 0)
    c_i = lax.broadcasted_iota(jnp.int32, (T, T), 1)
    tri_lo = jnp.where(c_i <= r_i, 1.0, 0.0).astype(BF16)
    tri_up = jnp.where(c_i >= r_i, 1.0, 0.0).astype(BF16)
    ct_ref[...] = jnp.zeros_like(ct_ref)
    n_ref[...] = jnp.zeros_like(n_ref)
    m_ref[...] = jnp.zeros_like(m_ref)
    scale = HEAD_DIM ** -0.5
    cwq, cbq, cwk, cbk = cwq_ref[...], cbq_ref[...], cwk_ref[...], cbk_ref[...]
    gbc, gbr = gbc_ref[...], gbr_ref[...]

    qc = _silu(_conv3(qc_ref[0], cwq, cbq))
    kc = _silu(_conv3(kc_ref[0], cwk, cbk)) * scale
    vct = vc_ref[0].T
    gcc = gcc_ref[0] + gbc
    grc = grc_ref[0] + gbr
    h_ct = None
    csc, csr = _mlstm_cumsums(gcc, grc, tri_lo, tri_up)
    for rev in (False, True):
        di = int(rev)
        ht = _mlstm_chunk(rev, qc, kc, vct, gcc, grc, csc, csr,
                          ct_ref.at[di], n_ref.at[di], m_ref.at[di])
        h_ct = ht if h_ct is None else h_ct + ht
    yc_ref[0] = _group_rms_2x64(h_ct.T * _sigmoid(oc_ref[0]))

    qs_ref[...] = _silu(_conv3(qx_ref[0], cwq, cbq))
    ks_ref[...] = _silu(_conv3(kx_ref[0], cwk, cbk)) * scale
    for c in range(nx):
        rows = slice(c * T, (c + 1) * T)
        vt_ref[:, rows] = vx_ref[0, rows, :].T
        csc_ref[rows, :], csr_ref[:, rows] = _mlstm_cumsums(gcx_ref[0, rows, :] + gbc,
                                                            grx_ref[0, :, rows] + gbr, tri_lo, tri_up)

    def body(c, carry):
        for rev in (False, True):
            di = int(rev)
            cidx = (nx - 1 - c) if rev else c
            r0 = pl.multiple_of(cidx * T, T)
            rows = pl.ds(r0, T)
            ht = _mlstm_chunk(rev, qs_ref[rows, :], ks_ref[rows, :], vt_ref[:, rows],
                              gcx_ref[0, rows, :] + gbc, grx_ref[0, :, rows] + gbr,
                              csc_ref[rows, :], csr_ref[:, rows],
                              ct_ref.at[di], n_ref.at[di], m_ref.at[di])
            if rev:
                hb_ref[:, rows] = ht
            else:
                hf_ref[:, rows] = ht
        return carry

    lax.fori_loop(0, nx, body, 0, unroll=4)
    for c in range(nx):
        rows = slice(c * T, (c + 1) * T)
        h = (hf_ref[:, rows] + hb_ref[:, rows]).T
        yx_ref[0, rows, :] = _group_rms_2x64(h * _sigmoid(ox_ref[0, rows, :]))


def _mlstm(seg_c, seg_x, conv_w, conv_b, gate_bc, gate_br):
    B, lx, _ = seg_x[1].shape
    lc = seg_c[1].shape[1]
    assert lc == ML_T and lx % ML_T == 0

    def seg_specs(L):
        col = lambda off: pl.BlockSpec((1, L, LANES), lambda b, p, off=off: (b, 0, off + p))
        return [col(0), col(2), col(0), col(0), col(0),
                pl.BlockSpec((1, SUBLANES, L), lambda b, p: (b, p, 0))]

    def seg_args(s):
        qk, v, o, g, gt = s
        return [qk, qk, v, o, g, gt]

    wspec = lambda rows, off: pl.BlockSpec((rows, LANES), lambda b, p, off=off: (0, off + p))
    return pl.pallas_call(
        _mlstm_kernel,
        grid=(B, 2),
        in_specs=seg_specs(lc) + seg_specs(lx)
                 + [wspec(3, 0), wspec(1, 0), wspec(3, 2), wspec(1, 2), wspec(1, 0),
                    pl.BlockSpec((SUBLANES, 1), lambda b, p: (p, 0))],
        out_specs=[pl.BlockSpec((1, lc, LANES), lambda b, p: (b, 0, p)),
                   pl.BlockSpec((1, lx, LANES), lambda b, p: (b, 0, p))],
        out_shape=[jax.ShapeDtypeStruct((B, lc, 2 * LANES), F32),
                   jax.ShapeDtypeStruct((B, lx, 2 * LANES), F32)],
        scratch_shapes=[pltpu.VMEM((lx, LANES), F32), pltpu.VMEM((lx, LANES), F32),
                        pltpu.VMEM((LANES, lx), F32), pltpu.VMEM((LANES, lx), F32),
                        pltpu.VMEM((LANES, lx), F32),
                        pltpu.VMEM((lx, LANES), F32), pltpu.VMEM((SUBLANES, lx), F32),
                        pltpu.VMEM((2, LANES, LANES), F32), pltpu.VMEM((2, 1, LANES), F32),
                        pltpu.VMEM((2, 2, LANES), F32)],
        compiler_params=_cparams("parallel", "arbitrary"),
        name="mlstm",
    )(*seg_args(seg_c), *seg_args(seg_x), conv_w, conv_b, conv_w, conv_b, gate_bc, gate_br)


def _outproj_kernel(x_ref, g_ref, hy_ref, da_ref, ml_ref, gain_ref, w_ref, o_ref):
    c_hy = hy_ref.shape[2]
    c_da = da_ref.shape[2]
    gain = gain_ref[...]
    acc = jnp.dot((hy_ref[0] * gain[:, 0:c_hy]).astype(BF16), w_ref[0:c_hy, :], preferred_element_type=F32)
    acc += jnp.dot((da_ref[0] * gain[:, c_hy:c_hy + c_da]).astype(BF16), w_ref[c_hy:c_hy + c_da, :],
                   preferred_element_type=F32)
    acc += jnp.dot((ml_ref[0] * gain[:, c_hy + c_da:]).astype(BF16), w_ref[c_hy + c_da:, :],
                   preferred_element_type=F32)
    o_ref[0] = x_ref[0] + g_ref[0] * acc


def _outproj(x, gate, y_hy, y_da, y_ml, gain, w_out):
    B, L, D = x.shape
    tm = min(TM, L)
    tok = lambda w: pl.BlockSpec((1, tm, w), lambda b, i: (b, i, 0))
    return pl.pallas_call(
        _outproj_kernel,
        grid=(B, L // tm),
        in_specs=[tok(D), pl.BlockSpec((1, 1, D), lambda b, i: (b, 0, 0)),
                  tok(y_hy.shape[2]), tok(y_da.shape[2]), tok(y_ml.shape[2]),
                  pl.BlockSpec(gain.shape, lambda b, i: (0, 0)), _const_spec(w_out.shape)],
        out_specs=tok(D),
        out_shape=jax.ShapeDtypeStruct((B, L, D), F32),
        compiler_params=_cparams("parallel", "parallel"),
        name="outproj",
    )(x, gate, y_hy, y_da, y_ml, gain, w_out)


def _ffn_kernel(final, x_ref, prev_ref, next_ref, sh_ref, sc_ref, g_ref, up_ref, cw_ref, cb_ref,
                down_ref, fw_ref, o_ref, act_ref):
    i = pl.program_id(1)
    last = pl.num_programs(1) - 1
    tm = x_ref.shape[1]
    d_ff = down_ref.shape[0]
    x = x_ref[0]
    mod = lambda a: _rms(a) * (1.0 + sc_ref[0]) + sh_ref[0]
    hp = jnp.where(i == 0, 0.0, mod(prev_ref[0]))
    hn = jnp.where(i == last, 0.0, mod(next_ref[0]))
    h = jnp.concatenate([hp, mod(x), hn], axis=0).astype(BF16)
    ext = tm + 2 * SUBLANES

    def conv_cols(lo, hi):
        u = jnp.dot(h, up_ref[:, lo:hi], preferred_element_type=F32)
        w = cw_ref[:, lo:hi]
        c = (pltpu.roll(u, 1, axis=0) * w[0:1] + u * w[1:2] + pltpu.roll(u, ext - 1, axis=0) * w[2:3]
             + cb_ref[:, lo:hi])
        return c[SUBLANES:SUBLANES + tm]

    for j in range(d_ff // FFN_TN):
        lo = j * FFN_TN
        a = conv_cols(lo, lo + FFN_TN)
        g = conv_cols(d_ff + lo, d_ff + lo + FFN_TN)
        act_ref[:, lo:lo + FFN_TN] = (_silu(g) * a).astype(BF16)
    y = x + g_ref[0] * jnp.dot(act_ref[...], down_ref[...], preferred_element_type=F32)
    if final:
        y = _rms(y) * fw_ref[...]
    o_ref[0] = y


def _ffn(x, shift, scale, gate, up, conv_w, conv_b, down, final_w, final):
    B, L, D = x.shape
    tm = min(TM, L)
    nb = tm // SUBLANES
    nrow = L // SUBLANES
    tok = pl.BlockSpec((1, tm, D), lambda b, i: (b, i, 0))
    mod = pl.BlockSpec((1, 1, D), lambda b, i: (b, 0, 0))
    full = lambda a: pl.BlockSpec(a.shape, lambda b, i: (0,) * a.ndim)
    return pl.pallas_call(
        functools.partial(_ffn_kernel, final),
        grid=(B, L // tm),
        in_specs=[tok,
                  pl.BlockSpec((1, SUBLANES, D), lambda b, i: (b, jnp.maximum(i * nb - 1, 0), 0)),
                  pl.BlockSpec((1, SUBLANES, D), lambda b, i: (b, jnp.minimum((i + 1) * nb, nrow - 1), 0)),
                  mod, mod, mod, _const_spec(up.shape), full(conv_w), full(conv_b),
                  _const_spec(down.shape), full(final_w)],
        out_specs=tok,
        out_shape=jax.ShapeDtypeStruct((B, L, D), F32),
        scratch_shapes=[pltpu.VMEM((tm, down.shape[0]), BF16)],
        compiler_params=_cparams("parallel", "parallel"),
        name="ffn_final" if final else "ffn",
    )(x, x, x, shift, scale, gate, up, conv_w, conv_b, down, final_w)


def _rope_tables(L):
    rows_n = L // GRID_W
    rows = jnp.repeat(jnp.arange(rows_n, dtype=F32), GRID_W)
    cols = jnp.tile(jnp.arange(GRID_W, dtype=F32), rows_n)
    nf = HEAD_DIM // 4
    inv = ROPE_BASE ** (-jnp.arange(nf, dtype=F32) / nf)
    cr, sr = jnp.cos(rows[:, None] * inv), jnp.sin(rows[:, None] * inv)
    cc, sc = jnp.cos(cols[:, None] * inv), jnp.sin(cols[:, None] * inv)
    cos64 = jnp.concatenate([cr, cr, cc, cc], axis=-1)
    sin64 = jnp.concatenate([-sr, sr, -sc, sc], axis=-1)
    return jnp.tile(cos64, (1, 2)), jnp.tile(sin64, (1, 2))


def _hy_features(L):
    t = jnp.linspace(0.0, 1.0, L, dtype=F32)
    pos = jnp.arange(L, dtype=F32)
    f = jnp.linspace(1e-4, HY_BANDS - 1, HY_BANDS, dtype=F32)
    ang = (2.0 * math.pi / L) * pos[:, None] * f
    z = jnp.concatenate([t[:, None], jnp.cos(ang), jnp.sin(ang)], axis=-1)
    return jnp.pad(z, ((0, 0), (0, HY_POS_PAD - z.shape[1])))


def _gate_layout(a):
    g = a.reshape(a.shape[:-1] + (4, 2, 2))
    g = jnp.moveaxis(g, -2, -3).reshape(a.shape[:-1] + (2, 8))
    g = jnp.pad(g, [(0, 0)] * (g.ndim - 1) + [(0, LANES - 8)])
    return g.reshape(a.shape[:-1] + (2 * LANES,))


def _gates_t(g):
    return jnp.swapaxes(jnp.concatenate([g[..., 0:8], g[..., LANES:LANES + 8]], axis=-1), 1, 2)


def kernel(x, c, ctx, c_ctx, ada_w, ada_b, w_in, w_out, hy_conv_w, hy_conv_b, hy_w1, hy_b1, hy_w2, hy_b2,
           hy_w3, hy_b3, hy_skip, da_lambda, ml_conv_w, ml_conv_b, ml_gate_b, mix_norm_w, ffn_up,
           ffn_conv_w, ffn_conv_b, ffn_down, final_norm_w):
    B, L, D = x.shape
    lc = ctx.shape[1]
    depth = ada_w.shape[0]
    hy_w = hy_skip.shape[2]
    n_hy = 3 * hy_w
    da_w = 2 * hy_w
    ml_w = hy_w

    rows = ((B + 1 + SUBLANES - 1) // SUBLANES) * SUBLANES
    cc = jnp.zeros((rows, D), F32).at[:B].set(c).at[B].set(c_ctx)
    mods = _adaln(cc, ada_w, ada_b)

    cos_x, sin_x = _rope_tables(L)
    cos_c, sin_c = jnp.ones((lc, LANES), F32), jnp.zeros((lc, LANES), F32)
    tables = _dft_tables(L)
    fd_c, fi_c = _dense_dft_tables(lc)
    zf_x, zf_c = _hy_features(L), _hy_features(lc)
    delta = jnp.abs(jnp.linspace(math.log(HY_DECAY_TARGET) / HY_FAST_DECAY,
                                 math.log(HY_DECAY_TARGET) / HY_SLOW_DECAY, hy_w, dtype=F32))[None, :]
    swap = np.arange(da_w) ^ (HEAD_DIM // 4)
    fw = final_norm_w[None, :]

    for l in range(depth):
        lam_init = 0.8 - 0.6 * math.exp(-0.3 * l)
        update_ctx = l < depth - 1
        mx = [m[:, None, :] for m in jnp.split(mods[l, :B], 6, axis=-1)]
        mc = [jnp.broadcast_to(m[None], (B, 1, D)) for m in jnp.split(mods[l, B:B + 1], 6, axis=-1)]

        w = w_in[l]
        o = n_hy
        wq, wk, wv = w[:, o:o + da_w], w[:, o + da_w:o + 2 * da_w], w[:, o + 2 * da_w:o + 3 * da_w]
        o += 3 * da_w
        wmqk, wmv, wmo = w[:, o:o + 2 * ml_w], w[:, o + 2 * ml_w:o + 3 * ml_w], w[:, o + 3 * ml_w:o + 4 * ml_w]
        wg = _gate_layout(w[:, o + 4 * ml_w:])
        w_ext = jnp.concatenate([w[:, :n_hy], wq, wq[:, swap], wk, wk[:, swap], wv, wmqk, wmv, wmo, wg],
                                axis=1).astype(BF16)
        gate_b = _gate_layout(ml_gate_b[l])
        gate_bc = gate_b[None, :]
        gate_br = _gates_t(gate_b[None, None, :])[0]

        px = _inproj(x, mx[0], mx[1], w_ext, cos_x, sin_x)
        pc = _inproj(ctx, mc[0], mc[1], w_ext, cos_c, sin_c)
        hy_x, q_x, k_x, v_x, mqk_x, mv_x, mo_x, g_x = px
        hy_c, q_c, k_c, v_c, mqk_c, mv_c, mo_c, g_c = pc

        w1p = jnp.pad(hy_w1[l], ((0, HY_POS_PAD - hy_w1.shape[1]), (0, 0)))
        b1, b2, b3 = hy_b1[l][None, :], hy_b2[l][None, :], hy_b3[l][None, :]
        cbias = hy_conv_b[l][None, :]
        kf = _hy_filter(zf_x, w1p, b1, hy_w2[l], b2, hy_w3[l], b3, delta, tables[0], tables[1])
        z1 = _hy_conv(0, hy_x, 0, hy_x, hy_conv_w[l], cbias, hy_skip[l], kf, tables)
        y_hy_x = _hy_conv(1, z1, 0, hy_x, hy_conv_w[l], cbias, hy_skip[l], kf, tables)

        y_da_x = _diff_attn(q_x, [k_c, k_x], [v_c, v_x], da_lambda[l], lam_init)

        seg_c = (mqk_c, mv_c, mo_c, g_c, _gates_t(g_c))
        seg_x = (mqk_x, mv_x, mo_x, g_x, _gates_t(g_x))
        y_ml_c, y_ml_x = _mlstm(seg_c, seg_x, ml_conv_w[l], ml_conv_b[l][None, :], gate_bc, gate_br)

        gain = mix_norm_w[l][None, :]
        wo = w_out[l].astype(BF16)
        up = ffn_up[l].astype(BF16)
        down = ffn_down[l].astype(BF16)
        fcb = ffn_conv_b[l][None, :]
        x = _outproj(x, mx[2], y_hy_x, y_da_x, y_ml_x, gain, wo)
        x = _ffn(x, mx[3], mx[4], mx[5], up, ffn_conv_w[l], fcb, down, fw, final=not update_ctx)

        if update_ctx:
            y_hy_c = _hy_ctx(hy_c, zf_c, w1p, b1, hy_w2[l], b2, hy_w3[l], b3, delta,
                             hy_conv_w[l], cbias, hy_skip[l], fd_c, fi_c)
            y_da_c = _diff_attn(q_c, [k_c], [v_c], da_lambda[l], lam_init)
            ctx = _outproj(ctx, mc[2], y_hy_c, y_da_c, y_ml_c, gain, wo)
            ctx = _ffn(ctx, mc[3], mc[4], mc[5], up, ffn_conv_w[l], fcb, down, fw, final=False)
    return x
```

```python
import functools
import math

import numpy as np
import jax
import jax.numpy as jnp
from jax import lax
from jax.experimental import pallas as pl
from jax.experimental.pallas import tpu as pltpu

F32 = jnp.float32
BF16 = jnp.bfloat16
HI = lax.Precision.HIGHEST

EPS = 1e-6
HEAD_DIM = 64
GRID_W = 64
ROPE_BASE = 10000.0
HY_BANDS = 16
HY_POS_PAD = 64
HY_DECAY_TARGET = 1e-2
HY_FAST_DECAY = 0.3
HY_SLOW_DECAY = 1.5
LANES = 128
SUBLANES = 8
VMEM_LIMIT = 56 * 1024 * 1024

FFT_N1 = 64
FFT_N2 = 128
FFT_UNROLL = 8
ML_T = 256
TQ = 1024
TQ_SUB = 256
TK = 512
TM = 512
FFN_TN = 256


def _cparams(*sem):
    return pltpu.CompilerParams(dimension_semantics=sem, vmem_limit_bytes=VMEM_LIMIT)


def _const_spec(shape):
    n = len(shape)
    return pl.BlockSpec(shape, lambda *_: (0,) * n, pipeline_mode=pl.Buffered(1))


def _silu(x):
    return x / (1.0 + jnp.exp(-x))


def _sigmoid(x):
    return 1.0 / (1.0 + jnp.exp(-x))


def _log_sigmoid(x):
    return jnp.minimum(x, 0.0) - jnp.log(1.0 + jnp.exp(-jnp.abs(x)))


def _rms(x):
    return x * lax.rsqrt(jnp.mean(x * x, axis=-1, keepdims=True) + EPS)


def _group_rms_2x64(y):
    lo = lax.broadcasted_iota(jnp.int32, y.shape, 1) < 64
    y2 = y * y
    s_lo = jnp.sum(jnp.where(lo, y2, 0.0), axis=-1, keepdims=True)
    s_hi = jnp.sum(jnp.where(lo, 0.0, y2), axis=-1, keepdims=True)
    r = jnp.where(lo, lax.rsqrt(s_lo * (1.0 / 64) + EPS), lax.rsqrt(s_hi * (1.0 / 64) + EPS))
    return y * r


def _conv3(x, w, b):
    L = x.shape[0]
    row = lax.broadcasted_iota(jnp.int32, x.shape, 0)
    xm = jnp.where(row == 0, 0.0, pltpu.roll(x, 1, axis=0))
    xp = jnp.where(row == L - 1, 0.0, pltpu.roll(x, L - 1, axis=0))
    return xm * w[0:1] + x * w[1:2] + xp * w[2:3] + b


def _adaln_kernel(c_ref, w_ref, b_ref, o_ref):
    a = _silu(c_ref[...])
    o_ref[0] = jnp.dot(a, w_ref[0], precision=HI, preferred_element_type=F32) + b_ref[0]


def _adaln(cc, ada_w, ada_b):
    depth, d, n6 = ada_w.shape
    tn = 1536
    return pl.pallas_call(
        _adaln_kernel,
        grid=(depth, n6 // tn),
        in_specs=[pl.BlockSpec(cc.shape, lambda l, j: (0, 0)),
                  pl.BlockSpec((1, d, tn), lambda l, j: (l, 0, j)),
                  pl.BlockSpec((1, 1, tn), lambda l, j: (l, 0, j))],
        out_specs=pl.BlockSpec((1, cc.shape[0], tn), lambda l, j: (l, 0, j)),
        out_shape=jax.ShapeDtypeStruct((depth, cc.shape[0], n6), F32),
        compiler_params=_cparams("parallel", "parallel"),
        name="adaln",
    )(cc, ada_w, ada_b.reshape(depth, 1, n6))


_C_HY, _C_Q, _C_QS, _C_K, _C_KS, _C_V, _C_MQK, _C_MV, _C_MO, _C_G, _C_END = (
    0, 768, 1280, 1792, 2304, 2816, 3328, 3840, 4096, 4352, 4608)


def _inproj_kernel(x_ref, sh_ref, sc_ref, w_ref, cos_ref, sin_ref,
                   hy_ref, q_ref, k_ref, v_ref, mqk_ref, mv_ref, mo_ref, g_ref):
    h = (_rms(x_ref[0]) * (1.0 + sc_ref[0]) + sh_ref[0]).astype(BF16)

    def proj(lo, hi):
        return jnp.dot(h, w_ref[:, lo:hi], preferred_element_type=F32)

    hy_ref[0] = proj(_C_HY, _C_Q)
    cos = cos_ref[...]
    sin = sin_ref[...]

    def rope(c0, c1, out_ref, scale):
        a = proj(c0, c1)
        asw = proj(c1, 2 * c1 - c0)
        for j in range((c1 - c0) // LANES):
            sl = slice(LANES * j, LANES * (j + 1))
            out_ref[0, :, sl] = ((a[:, sl] * cos + asw[:, sl] * sin) * scale).astype(BF16)

    rope(_C_Q, _C_QS, q_ref, (HEAD_DIM ** -0.5) * math.log2(math.e))
    rope(_C_K, _C_KS, k_ref, 1.0)
    v_ref[0] = proj(_C_V, _C_MQK).astype(BF16)
    mqk_ref[0] = proj(_C_MQK, _C_MV)
    mv_ref[0] = proj(_C_MV, _C_MO)
    mo_ref[0] = proj(_C_MO, _C_G)
    g_ref[0] = proj(_C_G, _C_END)


def _inproj(x, shift, scale, w_ext, cos, sin):
    B, L, D = x.shape
    tm = min(TM, L)
    tok = lambda w: pl.BlockSpec((1, tm, w), lambda b, i: (b, i, 0))
    mod = pl.BlockSpec((1, 1, D), lambda b, i: (b, 0, 0))
    tab = pl.BlockSpec((tm, LANES), lambda b, i: (i, 0))
    widths = (768, 512, 512, 512, 512, 256, 256, 256)
    dtypes = (F32, BF16, BF16, BF16, F32, F32, F32, F32)
    return pl.pallas_call(
        _inproj_kernel,
        grid=(B, L // tm),
        in_specs=[tok(D), mod, mod, _const_spec(w_ext.shape), tab, tab],
        out_specs=[tok(w) for w in widths],
        out_shape=[jax.ShapeDtypeStruct((B, L, w), dt) for w, dt in zip(widths, dtypes)],
        compiler_params=_cparams("parallel", "parallel"),
        name="inproj",
    )(x, shift, scale, w_ext, cos, sin)


def _dft_tables(L):
    n1, n2, g = FFT_N1, FFT_N2, SUBLANES
    N = n1 * n2
    assert N == 2 * L
    k1 = np.arange(n1)[:, None]
    m1 = np.arange(n1 // 2)[None, :]
    th = 2 * np.pi * ((k1 * m1) % n1) / n1
    f = np.stack([np.cos(th), -np.sin(th)], axis=1)
    eye = np.eye(g)
    m1f = np.einsum('krn,gh->krgnh', f, eye).reshape(2 * n1 * g, (n1 // 2) * g)
    m1i = np.einsum('krn,gh->ngkrh', f, eye).reshape((n1 // 2) * g, 2 * n1 * g)
    a = np.arange(n2)
    th2 = 2 * np.pi * ((a[:, None] * a[None, :]) % n2) / n2
    fr, fi = np.cos(th2), -np.sin(th2)
    tw = 2 * np.pi * ((np.arange(n1)[:, None] * a[None, :]) % N) / N
    tr, ti = np.cos(tw)[:, None, :], -np.sin(tw)[:, None, :]
    gr = fr[None] * tr - fi[None] * ti
    gi = fr[None] * ti + fi[None] * tr
    gf = np.concatenate([np.concatenate([gr, -gi], axis=2),
                         np.concatenate([gi, gr], axis=2)], axis=1)
    gb = np.transpose(gf, (0, 2, 1))
    cvt = lambda m: jnp.asarray(m.astype(np.float32)).astype(BF16)
    return cvt(m1f), cvt(gf), cvt(gb), cvt(m1i)


def _fft_block_fwd(src3_ref, a_ref, m1f_ref):
    nh, n2, c = src3_ref.shape
    m1f = m1f_ref[...]
    group = lambda g: slice(SUBLANES * g, SUBLANES * (g + 1))
    for g in range(0, n2 // SUBLANES, 2):
        xg = jnp.concatenate([src3_ref[:, group(g + i), :].reshape(nh * SUBLANES, c) for i in range(2)],
                             axis=1).astype(BF16)
        out = jnp.dot(m1f, xg, preferred_element_type=F32)
        for i in range(2):
            a_ref[:, group(g + i), :] = out[:, c * i:c * (i + 1)].reshape(4 * nh, SUBLANES, c)


def _fft_block_inv(a_ref, dst3_ref, m1i_ref):
    nh, n2, c = dst3_ref.shape
    m1i = m1i_ref[...]
    group = lambda g: slice(SUBLANES * g, SUBLANES * (g + 1))
    for g in range(0, n2 // SUBLANES, 2):
        ag = jnp.concatenate([a_ref[:, group(g + i), :].reshape(4 * nh * SUBLANES, c) for i in range(2)],
                             axis=1).astype(BF16)
        out = jnp.dot(m1i, ag, preferred_element_type=F32)
        for i in range(2):
            dst3_ref[:, group(g + i), :] = out[:, c * i:c * (i + 1)].reshape(nh, SUBLANES, c)


def _hy_mlp(z_ref, w1_ref, b1_ref, w2_ref, b2_ref):
    h = jnp.sin(jnp.dot(z_ref[...], w1_ref[...], precision=HI, preferred_element_type=F32) + b1_ref[...])
    return jnp.sin(jnp.dot(h, w2_ref[...], precision=HI, preferred_element_type=F32) + b2_ref[...])


def _hy_taps(h2, w3_ref, b3_ref, delta_ref, zero_first):
    L = h2.shape[0]
    h = jnp.dot(h2, w3_ref[...], precision=HI, preferred_element_type=F32) + b3_ref[...]
    row = lax.broadcasted_iota(jnp.int32, h.shape, 0)
    t = row.astype(F32) * (1.0 / (L - 1))
    h = h * jnp.exp(-t * delta_ref[...])
    if zero_first:
        h = jnp.where(row == 0, 0.0, h)
    return h


def _hy_mlp_kernel(z_ref, w1_ref, b1_ref, w2_ref, b2_ref, o_ref):
    o_ref[...] = _hy_mlp(z_ref, w1_ref, b1_ref, w2_ref, b2_ref)


def _hy_hidden(zfeat, w1, b1, w2, b2):
    args = (zfeat, w1, b1, w2, b2)
    return pl.pallas_call(
        _hy_mlp_kernel,
        grid=(1,),
        in_specs=[pl.BlockSpec(a.shape, lambda i: (0, 0)) for a in args],
        out_specs=pl.BlockSpec((zfeat.shape[0], w2.shape[1]), lambda i: (0, 0)),
        out_shape=jax.ShapeDtypeStruct((zfeat.shape[0], w2.shape[1]), F32),
        compiler_params=_cparams("arbitrary"),
        name="hy_mlp",
    )(*args)


def _hy_filter_kernel(h2_ref, w3f_ref, b3f_ref, w3b_ref, b3b_ref,
                      delta_ref, m1f_ref, gf_ref, kf_ref, src_ref, af_ref, ab_ref):
    h2 = h2_ref[...]
    hf = _hy_taps(h2, w3f_ref, b3f_ref, delta_ref, False)
    hb = _hy_taps(h2, w3b_ref, b3b_ref, delta_ref, True)
    L, c = hf.shape
    n_total = 2.0 * L
    inv = 1.0 / ((jnp.sum(jnp.abs(hf), axis=0, keepdims=True)
                  + jnp.sum(jnp.abs(hb), axis=0, keepdims=True)) * n_total)
    nh, n2 = src_ref.shape[0], src_ref.shape[1]
    src_ref[...] = (hf * inv).reshape(nh, n2, c)
    _fft_block_fwd(src_ref, af_ref, m1f_ref)
    src_ref[...] = (hb * inv).reshape(nh, n2, c)
    _fft_block_fwd(src_ref, ab_ref, m1f_ref)

    def body(k1, carry):
        sl = pl.ds(pl.multiple_of(2 * k1, 2), 2)
        g = gf_ref[k1]
        xf = jnp.dot(g, af_ref[sl].reshape(2 * n2, c).astype(BF16), preferred_element_type=F32)
        xb = jnp.dot(g, ab_ref[sl].reshape(2 * n2, c).astype(BF16), preferred_element_type=F32)
        kr = xf[:n2] + xb[:n2]
        ki = xf[n2:] - xb[n2:]
        kf_ref[0, sl] = jnp.concatenate([kr, ki], axis=0).reshape(2, n2, c).astype(BF16)
        return carry

    lax.fori_loop(0, gf_ref.shape[0], body, 0, unroll=FFT_UNROLL)


def _hy_filter(h2, w3, b3, delta, m1f, gf):
    C = delta.shape[1]
    cb = LANES
    ncb = C // cb
    nh, n2 = FFT_N1 // 2, FFT_N2
    full = lambda a: pl.BlockSpec(a.shape, lambda o, j: (0,) * a.ndim)
    return pl.pallas_call(
        _hy_filter_kernel,
        grid=(2, ncb),
        in_specs=[full(h2),
                  pl.BlockSpec((w3.shape[0], cb), lambda o, j: (0, o * 2 * ncb + j)),
                  pl.BlockSpec((1, cb), lambda o, j: (0, o * 2 * ncb + j)),
                  pl.BlockSpec((w3.shape[0], cb), lambda o, j: (0, o * 2 * ncb + ncb + j)),
                  pl.BlockSpec((1, cb), lambda o, j: (0, o * 2 * ncb + ncb + j)),
                  pl.BlockSpec((1, cb), lambda o, j: (0, j)),
                  _const_spec(m1f.shape), _const_spec(gf.shape)],
        out_specs=pl.BlockSpec((1, 4 * nh, n2, cb), lambda o, j: (o, 0, 0, j)),
        out_shape=jax.ShapeDtypeStruct((2, 4 * nh, n2, C), BF16),
        scratch_shapes=[pltpu.VMEM((nh, n2, cb), F32), pltpu.VMEM((4 * nh, n2, cb), F32),
                        pltpu.VMEM((4 * nh, n2, cb), F32)],
        compiler_params=_cparams("arbitrary", "arbitrary"),
        name="hy_filter",
    )(h2, w3, b3, w3, b3, delta, m1f, gf)


def _hy_conv_kernel(conv_z, norm_out, z_ref, g_ref, cwz_ref, cbz_ref, cwg_ref, cbg_ref, skip_ref,
                    kf_ref, m1f_ref, gf_ref, gb_ref, m1i_ref, o_ref, zs_ref, ys_ref, a_ref):
    nh, n2, c = zs_ref.shape
    z = z_ref[0]
    if conv_z:
        z = _conv3(z, cwz_ref[...], cbz_ref[...])
    zs_ref[...] = z.reshape(nh, n2, c)
    _fft_block_fwd(zs_ref, a_ref, m1f_ref)

    def body(k1, carry):
        sl = pl.ds(pl.multiple_of(2 * k1, 2), 2)
        x = jnp.dot(gf_ref[k1], a_ref[sl].reshape(2 * n2, c).astype(BF16), preferred_element_type=F32)
        kf = kf_ref[0, sl].astype(F32)
        xr, xi, kr, ki = x[:n2], x[n2:], kf[0], kf[1]
        y = jnp.concatenate([xr * kr - xi * ki, xr * ki + xi * kr], axis=0).astype(BF16)
        a_ref[sl] = jnp.dot(gb_ref[k1], y, preferred_element_type=F32).reshape(2, n2, c)
        return carry

    lax.fori_loop(0, gf_ref.shape[0], body, 0, unroll=FFT_UNROLL)
    _fft_block_inv(a_ref, ys_ref, m1i_ref)
    y = ys_ref[...].reshape(nh * n2, c)
    z = zs_ref[...].reshape(nh * n2, c)
    gate = _conv3(g_ref[0], cwg_ref[...], cbg_ref[...])
    out = gate * (y + skip_ref[...] * z)
    if norm_out:
        out = _group_rms_2x64(out)
    o_ref[0] = out.astype(o_ref.dtype)


def _hy_conv(order, z, p_hy, conv_w, conv_b, skip, kf, tables):
    m1f, gf, gb, m1i = tables
    B, L, _ = p_hy.shape
    C = skip.shape[1]
    cb = LANES
    ncb = C // cb
    nh, n2 = FFT_N1 // 2, FFT_N2
    gcol = (order + 1) * ncb
    cw = lambda base: pl.BlockSpec((3, cb), lambda j, b: (0, base + j))
    cbs = lambda base: pl.BlockSpec((1, cb), lambda j, b: (0, base + j))
    return pl.pallas_call(
        functools.partial(_hy_conv_kernel, order == 0, order == 1),
        grid=(ncb, B),
        in_specs=[pl.BlockSpec((1, L, cb), lambda j, b: (b, 0, j)),
                  pl.BlockSpec((1, L, cb), lambda j, b: (b, 0, gcol + j)),
                  cw(0), cbs(0), cw(gcol), cbs(gcol),
                  pl.BlockSpec((1, cb), lambda j, b: (0, j)),
                  pl.BlockSpec((1, 4 * nh, n2, cb), lambda j, b: (order, 0, 0, j)),
                  _const_spec(m1f.shape), _const_spec(gf.shape), _const_spec(gb.shape),
                  _const_spec(m1i.shape)],
        out_specs=pl.BlockSpec((1, L, cb), lambda j, b: (b, 0, j)),
        out_shape=jax.ShapeDtypeStruct((B, L, C), F32 if order == 0 else BF16),
        scratch_shapes=[pltpu.VMEM((nh, n2, cb), F32), pltpu.VMEM((nh, n2, cb), F32),
                        pltpu.VMEM((4 * nh, n2, cb), F32)],
        compiler_params=_cparams("arbitrary", "arbitrary"),
        name=f"hy_conv{order}",
    )(z, p_hy, conv_w, conv_b, conv_w, conv_b, skip[order:order + 1], kf, m1f, gf, gb, m1i)


def _hy_ctx_kernel(p_ref, h2_ref, w3_ref, b3_ref, delta_ref,
                   cw_ref, cb_ref, skip_ref, fd_ref, fi_ref, o_ref):
    lc = p_ref.shape[1]
    C = skip_ref.shape[1]
    h2 = h2_ref[...]
    u = _conv3(p_ref[0], cw_ref[...], cb_ref[...])
    z = u[:, 0:C]
    fd = fd_ref[...]
    fi = fi_ref[...]
    nf = fd.shape[0] // 2
    dot = lambda a, b: jnp.dot(a.astype(BF16), b.astype(BF16), preferred_element_type=F32)
    for o in range(2):
        base = 2 * o * C
        hf = _hy_taps(h2, w3_ref.at[:, base:base + C], b3_ref.at[:, base:base + C], delta_ref, False)
        hb = _hy_taps(h2, w3_ref.at[:, base + C:base + 2 * C], b3_ref.at[:, base + C:base + 2 * C],
                      delta_ref, True)
        inv = 1.0 / ((jnp.sum(jnp.abs(hf), axis=0, keepdims=True)
                      + jnp.sum(jnp.abs(hb), axis=0, keepdims=True)) * (2.0 * lc))
        xf = dot(fd, hf * inv)
        xb = dot(fd, hb * inv)
        kr = xf[:nf] + xb[:nf]
        ki = xf[nf:] - xb[nf:]
        x = dot(fd, z)
        xr, xi = x[:nf], x[nf:]
        y = dot(fi, jnp.concatenate([xr * kr - xi * ki, xr * ki + xi * kr], axis=0))
        z = u[:, (o + 1) * C:(o + 2) * C] * (y + skip_ref[o:o + 1, :] * z)
    for j in range(C // LANES):
        o_ref[0, :, j * LANES:(j + 1) * LANES] = _group_rms_2x64(z[:, j * LANES:(j + 1) * LANES]).astype(BF16)


def _dense_dft_tables(lc):
    n = 2 * lc
    k = np.arange(n)[:, None]
    t = np.arange(lc)[None, :]
    th = 2 * np.pi * ((k * t) % n) / n
    fd = np.concatenate([np.cos(th), -np.sin(th)], axis=0)
    fi = np.concatenate([np.cos(th).T, -np.sin(th).T], axis=1)
    return jnp.asarray(fd.astype(np.float32)), jnp.asarray(fi.astype(np.float32))


def _hy_ctx(p_hy, h2, w3, b3, delta, conv_w, conv_b, skip, fd, fi):
    B, lc, w = p_hy.shape
    C = skip.shape[1]
    full = lambda a: pl.BlockSpec(a.shape, lambda b: (0,) * a.ndim)
    args = (h2, w3, b3, delta, conv_w, conv_b, skip, fd, fi)
    return pl.pallas_call(
        _hy_ctx_kernel,
        grid=(B,),
        in_specs=[pl.BlockSpec((1, lc, w), lambda b: (b, 0, 0))] + [full(a) for a in args],
        out_specs=pl.BlockSpec((1, lc, C), lambda b: (b, 0, 0)),
        out_shape=jax.ShapeDtypeStruct((B, lc, C), BF16),
        compiler_params=_cparams("parallel"),
        name="hy_ctx",
    )(p_hy, *args)


def _attn_kernel(seg_lens, lam_init, q_ref, lam_ref, *refs):
    nseg = len(seg_lens)
    k_refs, v_refs = refs[:nseg], refs[nseg:2 * nseg]
    o_ref = refs[2 * nseg]
    n_s, n_p = 3, 2
    s_refs = refs[2 * nseg + 1:2 * nseg + 1 + n_s]
    p_flat = refs[2 * nseg + 1 + n_s:2 * nseg + 1 + n_s + 2 * n_p]
    p_refs = [p_flat[2 * i:2 * i + 2] for i in range(n_p)]
    tq = q_ref.shape[1]
    th = s_refs[0].shape[0]
    dl = lam_ref[...]
    lam = (jnp.exp(jnp.sum(dl[0:1] * dl[1:2], keepdims=True))
           - jnp.exp(jnp.sum(dl[2:3] * dl[3:4], keepdims=True))) + lam_init

    chunks = []
    off = 0
    for s, n in enumerate(seg_lens):
        for st in range(0, n, TK):
            w = min(TK, n - st)
            chunks.append((s, st, off, w))
            off += w

    nh = tq // th

    def lane_tiles(a):
        return [a[:, t * LANES:(t + 1) * LANES] for t in range(a.shape[1] // LANES)]

    def scores(h, m):
        q = q_ref[0, h * th:(h + 1) * th, :]
        lane = lax.broadcasted_iota(jnp.int32, q.shape, 1)
        qm = jnp.where((lane < HEAD_DIM) == (m == 0), q, jnp.zeros_like(q))
        s_ref = s_refs[(2 * h + m) % n_s]
        mxw = jnp.full((th, LANES), -jnp.inf, F32)
        for (s, st, o, w) in chunks:
            sc = lax.dot_general(qm, k_refs[s][0, st:st + w, :], (((1,), (1,)), ((), ())),
                                 preferred_element_type=F32)
            s_ref[:, o:o + w] = sc
            for t in lane_tiles(sc):
                mxw = jnp.maximum(mxw, t)
        return jnp.max(mxw, axis=-1, keepdims=True)

    def probs(h, m, mx):
        s_ref = s_refs[(2 * h + m) % n_s]
        p_ref = p_refs[h % n_p][m]
        lw = jnp.zeros((th, LANES), F32)
        for (s, st, o, w) in chunks:
            p = jnp.exp2(s_ref[:, o:o + w] - mx)
            for t in lane_tiles(p):
                lw = lw + t
            p_ref[:, o:o + w] = p.astype(BF16)
        return 1.0 / jnp.sum(lw, axis=-1, keepdims=True)

    def output(h, inv_l0, inv_l1):
        p0_ref, p1_ref = p_refs[h % n_p]
        r = jnp.broadcast_to(lam * inv_l1 / inv_l0, (th, LANES)).astype(BF16)
        acc = jnp.zeros((th, v_refs[0].shape[2]), F32)
        for (s, st, o, w) in chunks:
            rb = jnp.concatenate([r] * (w // LANES), axis=1)
            wgt = p0_ref[:, o:o + w] - p1_ref[:, o:o + w] * rb
            acc = acc + jnp.dot(wgt, v_refs[s][0, st:st + w, :], preferred_element_type=F32)
        o_ref[0, h * th:(h + 1) * th, :] = (_rms(acc * inv_l0) * (1.0 - lam_init)).astype(BF16)

    mx = {(0, 0): scores(0, 0), (0, 1): scores(0, 1)}
    for h in range(nh):
        il0 = probs(h, 0, mx[h, 0])
        if h + 1 < nh:
            mx[h + 1, 0] = scores(h + 1, 0)
        il1 = probs(h, 1, mx[h, 1])
        if h + 1 < nh:
            mx[h + 1, 1] = scores(h + 1, 1)
        output(h, il0, il1)


def _diff_attn(q, ks, vs, da_lambda, lam_init):
    B, lq, w = q.shape
    H = w // LANES
    tq = min(TQ, lq)
    th = min(TQ_SUB, tq // 2)
    seg_lens = tuple(k.shape[1] for k in ks)
    seg = lambda n: pl.BlockSpec((1, n, LANES), lambda b, h, i: (b, 0, h))
    return pl.pallas_call(
        functools.partial(_attn_kernel, seg_lens, lam_init),
        grid=(B, H, lq // tq),
        in_specs=[pl.BlockSpec((1, tq, LANES), lambda b, h, i: (b, i, h)),
                  pl.BlockSpec(da_lambda.shape, lambda b, h, i: (0, 0))]
                 + [seg(n) for n in seg_lens] + [seg(n) for n in seg_lens],
        out_specs=pl.BlockSpec((1, tq, LANES), lambda b, h, i: (b, i, h)),
        out_shape=jax.ShapeDtypeStruct((B, lq, w), BF16),
        scratch_shapes=[pltpu.VMEM((th, sum(seg_lens)), F32)] * 3
                       + [pltpu.VMEM((th, sum(seg_lens)), BF16)] * 4,
        compiler_params=_cparams("parallel", "parallel", "arbitrary"),
        name="diff_attn",
    )(q, da_lambda, *ks, *vs)


def _split3(x):
    hi = x.astype(BF16)
    r1 = x - hi.astype(F32)
    mid = r1.astype(BF16)
    return hi, mid, (r1 - mid.astype(F32)).astype(BF16)


def _mlstm_cumsums(gc, gr, tri_lo, tri_up):
    dot = lambda a, b: jnp.dot(a, b, preferred_element_type=F32)
    lc = _split3(_log_sigmoid(gc))
    lr = _split3(_log_sigmoid(gr))
    pre_c = sum(dot(tri_lo, t) for t in lc)
    suf_c = sum(dot(tri_up, t) for t in lc)
    pre_r = sum(dot(t, tri_up) for t in lr)
    suf_r = sum(dot(t, tri_lo) for t in lr)
    fwd_lane = lax.broadcasted_iota(jnp.int32, gc.shape, 1) < 4
    fwd_row = lax.broadcasted_iota(jnp.int32, gr.shape, 0) < 4
    return jnp.where(fwd_lane, pre_c, suf_c), jnp.where(fwd_row, pre_r, suf_r)


def _mlstm_chunk(rev, q, k, vt, gc, gr, csum_c, csum_r, c_ref, n_ref, m_ref):
    T = q.shape[0]
    d = 6 if rev else 2
    ii = 4 if rev else 0
    lo = lax.broadcasted_iota(jnp.int32, (T, LANES), 1) < HEAD_DIM
    lo_row = lax.broadcasted_iota(jnp.int32, (1, LANES), 1) < HEAD_DIM
    sub = lax.broadcasted_iota(jnp.int32, (LANES, T), 0) < HEAD_DIM
    key_i = lax.broadcasted_iota(jnp.int32, (T, T), 0)
    qry_i = lax.broadcasted_iota(jnp.int32, (T, T), 1)
    mask = (key_i >= qry_i) if rev else (key_i <= qry_i)
    last = 0 if rev else T - 1
    qb = q.astype(BF16)
    kb = k.astype(BF16)
    vtb = vt.astype(BF16)
    zero = jnp.zeros_like(kb)
    nt = (((1,), (1,)), ((), ()))
    brow, igrow, mloc, numt, dsum = [], [], [], [], []
    for j in range(2):
        brow.append(csum_r[d + j:d + j + 1, :])
        igrow.append(gr[ii + j:ii + j + 1, :])
        ccol = gc[:, ii + j:ii + j + 1] - csum_c[:, d + j:d + j + 1]
        logd = jnp.where(mask, brow[j] + ccol, -jnp.inf)
        mloc.append(jnp.max(logd, axis=0, keepdims=True))
        kj = jnp.where(lo, kb, zero) if j == 0 else jnp.where(lo, zero, kb)
        st = lax.dot_general(kj, qb, nt, preferred_element_type=F32) * jnp.exp(logd - mloc[j])
        numt.append(jnp.dot(vtb, st.astype(BF16), preferred_element_type=F32))
        dsum.append(jnp.sum(st, axis=0, keepdims=True))
    num_loc = jnp.where(sub, numt[0], numt[1])

    n = n_ref[...]
    row8 = lax.broadcasted_iota(jnp.int32, (SUBLANES, LANES), 0)
    lo8 = lax.broadcasted_iota(jnp.int32, (SUBLANES, LANES), 1) < HEAD_DIM
    n8 = jnp.where(row8 == jnp.where(lo8, 0, 1), jnp.broadcast_to(n, (SUBLANES, LANES)), 0.0)
    qn = lax.dot_general(n8.astype(BF16), qb, nt, preferred_element_type=F32)
    hden, wrow, decay, interw, aloc, m_new = [], [], [], [], [], []
    for j in range(2):
        m_prev = m_ref[j:j + 1, 0:1]
        mrow = jnp.maximum(brow[j] + m_prev, mloc[j])
        a = jnp.exp(mloc[j] - mrow)
        iw = jnp.exp(brow[j] + m_prev - mrow)
        dn = a * dsum[j] + iw * qn[j:j + 1, :]
        hden.append(jnp.maximum(jnp.abs(dn), jnp.exp(-mrow)))
        aloc.append(a)
        interw.append(iw)
        mn = mrow[:, last:last + 1]
        btot = brow[j][:, last:last + 1]
        wrow.append(jnp.exp(btot - brow[j] + igrow[j] - mn))
        decay.append(jnp.exp(btot + m_prev - mn))
        m_new.append(mn)
    c = c_ref[...]
    intert = lax.dot_general(c.astype(BF16), qb, nt, preferred_element_type=F32)
    num = jnp.where(sub, aloc[0], aloc[1]) * num_loc + jnp.where(sub, interw[0], interw[1]) * intert
    ht = num / jnp.where(sub, hden[0], hden[1])
    wvt = (vt * jnp.where(sub, wrow[0], wrow[1])).astype(BF16)
    upd = jnp.dot(wvt, kb, preferred_element_type=F32)
    rr = lax.broadcasted_iota(jnp.int32, (LANES, LANES), 0) < HEAD_DIM
    cc = lax.broadcasted_iota(jnp.int32, (LANES, LANES), 1) < HEAD_DIM
    dcol = jnp.where(lax.broadcasted_iota(jnp.int32, (LANES, 1), 0) < HEAD_DIM, decay[0], decay[1])
    c_ref[...] = dcol * c + jnp.where(rr == cc, upd, 0.0)
    rowt = lax.broadcasted_iota(jnp.int32, (SUBLANES, T), 0)
    w8 = jnp.where(rowt == 0, wrow[0], jnp.where(rowt == 1, wrow[1], 0.0))
    r8 = jnp.dot(w8.astype(BF16), kb, preferred_element_type=F32)
    n_ref[...] = jnp.where(lo_row, decay[0] * n + r8[0:1, :], decay[1] * n + r8[1:2, :])
    m_ref[0:1, :] = jnp.broadcast_to(m_new[0], (1, LANES))
    m_ref[1:2, :] = jnp.broadcast_to(m_new[1], (1, LANES))
    return ht


def _mlstm_kernel(qc_ref, kc_ref, vc_ref, oc_ref, gcc_ref, grc_ref,
                  qx_ref, kx_ref, vx_ref, ox_ref, gcx_ref, grx_ref,
                  cwq_ref, cbq_ref, cwk_ref, cbk_ref, gbc_ref, gbr_ref,
                  yc_ref, yx_ref,
                  qs_ref, ks_ref, vt_ref, hf_ref, hb_ref, csc_ref, csr_ref, ct_ref, n_ref, m_ref):
    T = ML_T
    lx = qx_ref.shape[1]
    nx = lx // T
    r_i = lax.broadcasted_iota(jnp.int32, (T, T), 0)
    c_i = lax.broadcasted_iota(jnp.int32, (T, T), 1)
    tri_lo = jnp.where(c_i <= r_i, 1.0, 0.0).astype(BF16)
    tri_up = jnp.where(c_i >= r_i, 1.0, 0.0).astype(BF16)
    ct_ref[...] = jnp.zeros_like(ct_ref)
    n_ref[...] = jnp.zeros_like(n_ref)
    m_ref[...] = jnp.zeros_like(m_ref)
    scale = HEAD_DIM ** -0.5
    cwq, cbq, cwk, cbk = cwq_ref[...], cbq_ref[...], cwk_ref[...], cbk_ref[...]
    gbc, gbr = gbc_ref[...], gbr_ref[...]

    qc = _silu(_conv3(qc_ref[0], cwq, cbq))
    kc = _silu(_conv3(kc_ref[0], cwk, cbk)) * scale
    vct = vc_ref[0].T
    gcc = gcc_ref[0] + gbc
    grc = grc_ref[0] + gbr
    h_ct = None
    csc, csr = _mlstm_cumsums(gcc, grc, tri_lo, tri_up)
    for rev in (False, True):
        di = int(rev)
        ht = _mlstm_chunk(rev, qc, kc, vct, gcc, grc, csc, csr,
                          ct_ref.at[di], n_ref.at[di], m_ref.at[di])
        h_ct = ht if h_ct is None else h_ct + ht
    yc_ref[0] = _group_rms_2x64(h_ct.T * _sigmoid(oc_ref[0])).astype(BF16)

    qs_ref[...] = _silu(_conv3(qx_ref[0], cwq, cbq))
    ks_ref[...] = _silu(_conv3(kx_ref[0], cwk, cbk)) * scale
    for c in range(nx):
        rows = slice(c * T, (c + 1) * T)
        vt_ref[:, rows] = vx_ref[0, rows, :].T
        csc_ref[rows, :], csr_ref[:, rows] = _mlstm_cumsums(gcx_ref[0, rows, :] + gbc,
                                                            grx_ref[0, :, rows] + gbr, tri_lo, tri_up)

    def body(c, carry):
        for rev in (False, True):
            di = int(rev)
            cidx = (nx - 1 - c) if rev else c
            r0 = pl.multiple_of(cidx * T, T)
            rows = pl.ds(r0, T)
            ht = _mlstm_chunk(rev, qs_ref[rows, :], ks_ref[rows, :], vt_ref[:, rows],
                              gcx_ref[0, rows, :] + gbc, grx_ref[0, :, rows] + gbr,
                              csc_ref[rows, :], csr_ref[:, rows],
                              ct_ref.at[di], n_ref.at[di], m_ref.at[di])
            if rev:
                hb_ref[:, rows] = ht
            else:
                hf_ref[:, rows] = ht
        return carry

    lax.fori_loop(0, nx, body, 0, unroll=4)
    for c in range(nx):
        rows = slice(c * T, (c + 1) * T)
        h = (hf_ref[:, rows] + hb_ref[:, rows]).T
        yx_ref[0, rows, :] = _group_rms_2x64(h * _sigmoid(ox_ref[0, rows, :])).astype(BF16)


def _mlstm(seg_c, seg_x, conv_w, conv_b, gate_bc, gate_br):
    B, lx, _ = seg_x[1].shape
    lc = seg_c[1].shape[1]
    assert lc == ML_T and lx % ML_T == 0

    def seg_specs(L):
        col = lambda off: pl.BlockSpec((1, L, LANES), lambda b, p, off=off: (b, 0, off + p))
        return [col(0), col(2), col(0), col(0), col(0),
                pl.BlockSpec((1, SUBLANES, L), lambda b, p: (b, p, 0))]

    def seg_args(s):
        qk, v, o, g, gt = s
        return [qk, qk, v, o, g, gt]

    wspec = lambda rows, off: pl.BlockSpec((rows, LANES), lambda b, p, off=off: (0, off + p))
    return pl.pallas_call(
        _mlstm_kernel,
        grid=(B, 2),
        in_specs=seg_specs(lc) + seg_specs(lx)
                 + [wspec(3, 0), wspec(1, 0), wspec(3, 2), wspec(1, 2), wspec(1, 0),
                    pl.BlockSpec((SUBLANES, 1), lambda b, p: (p, 0))],
        out_specs=[pl.BlockSpec((1, lc, LANES), lambda b, p: (b, 0, p)),
                   pl.BlockSpec((1, lx, LANES), lambda b, p: (b, 0, p))],
        out_shape=[jax.ShapeDtypeStruct((B, lc, 2 * LANES), BF16),
                   jax.ShapeDtypeStruct((B, lx, 2 * LANES), BF16)],
        scratch_shapes=[pltpu.VMEM((lx, LANES), F32), pltpu.VMEM((lx, LANES), F32),
                        pltpu.VMEM((LANES, lx), F32), pltpu.VMEM((LANES, lx), F32),
                        pltpu.VMEM((LANES, lx), F32),
                        pltpu.VMEM((lx, LANES), F32), pltpu.VMEM((SUBLANES, lx), F32),
                        pltpu.VMEM((2, LANES, LANES), F32), pltpu.VMEM((2, 1, LANES), F32),
                        pltpu.VMEM((2, 2, LANES), F32)],
        compiler_params=_cparams("parallel", "arbitrary"),
        name="mlstm",
    )(*seg_args(seg_c), *seg_args(seg_x), conv_w, conv_b, conv_w, conv_b, gate_bc, gate_br)


def _outproj_kernel(x_ref, g_ref, hy_ref, da_ref, ml_ref, gain_ref, w_ref, o_ref):
    c_hy = hy_ref.shape[2]
    c_da = da_ref.shape[2]
    gain = gain_ref[...]
    acc = jnp.dot((hy_ref[0] * gain[:, 0:c_hy]).astype(BF16), w_ref[0:c_hy, :], preferred_element_type=F32)
    acc += jnp.dot((da_ref[0] * gain[:, c_hy:c_hy + c_da]).astype(BF16), w_ref[c_hy:c_hy + c_da, :],
                   preferred_element_type=F32)
    acc += jnp.dot((ml_ref[0] * gain[:, c_hy + c_da:]).astype(BF16), w_ref[c_hy + c_da:, :],
                   preferred_element_type=F32)
    o_ref[0] = x_ref[0] + g_ref[0] * acc


def _outproj(x, gate, y_hy, y_da, y_ml, gain, w_out):
    B, L, D = x.shape
    tm = min(TM, L)
    tok = lambda w: pl.BlockSpec((1, tm, w), lambda b, i: (b, i, 0))
    return pl.pallas_call(
        _outproj_kernel,
        grid=(B, L // tm),
        in_specs=[tok(D), pl.BlockSpec((1, 1, D), lambda b, i: (b, 0, 0)),
                  tok(y_hy.shape[2]), tok(y_da.shape[2]), tok(y_ml.shape[2]),
                  pl.BlockSpec(gain.shape, lambda b, i: (0, 0)), _const_spec(w_out.shape)],
        out_specs=tok(D),
        out_shape=jax.ShapeDtypeStruct((B, L, D), F32),
        compiler_params=_cparams("parallel", "parallel"),
        name="outproj",
    )(x, gate, y_hy, y_da, y_ml, gain, w_out)


def _ffn_kernel(final, x_ref, prev_ref, next_ref, sh_ref, sc_ref, g_ref, up_ref, cw_ref, cb_ref,
                down_ref, fw_ref, o_ref, act_ref):
    i = pl.program_id(1)
    last = pl.num_programs(1) - 1
    tm = x_ref.shape[1]
    d_ff = down_ref.shape[0]
    x = x_ref[0]
    mod = lambda a: _rms(a) * (1.0 + sc_ref[0]) + sh_ref[0]
    hp = jnp.where(i == 0, 0.0, mod(prev_ref[0]))
    hn = jnp.where(i == last, 0.0, mod(next_ref[0]))
    h = jnp.concatenate([hp, mod(x), hn], axis=0).astype(BF16)
    ext = tm + 2 * SUBLANES

    def conv_cols(lo, hi):
        u = jnp.dot(h, up_ref[:, lo:hi], preferred_element_type=F32)
        w = cw_ref[:, lo:hi]
        c = (pltpu.roll(u, 1, axis=0) * w[0:1] + u * w[1:2] + pltpu.roll(u, ext - 1, axis=0) * w[2:3]
             + cb_ref[:, lo:hi])
        return c[SUBLANES:SUBLANES + tm]

    for j in range(d_ff // FFN_TN):
        lo = j * FFN_TN
        a = conv_cols(lo, lo + FFN_TN)
        g = conv_cols(d_ff + lo, d_ff + lo + FFN_TN)
        act_ref[:, lo:lo + FFN_TN] = (_silu(g) * a).astype(BF16)
    y = x + g_ref[0] * jnp.dot(act_ref[...], down_ref[...], preferred_element_type=F32)
    if final:
        y = _rms(y) * fw_ref[...]
    o_ref[0] = y


def _ffn(x, shift, scale, gate, up, conv_w, conv_b, down, final_w, final):
    B, L, D = x.shape
    tm = min(TM, L)
    nb = tm // SUBLANES
    nrow = L // SUBLANES
    tok = pl.BlockSpec((1, tm, D), lambda b, i: (b, i, 0))
    mod = pl.BlockSpec((1, 1, D), lambda b, i: (b, 0, 0))
    full = lambda a: pl.BlockSpec(a.shape, lambda b, i: (0,) * a.ndim)
    return pl.pallas_call(
        functools.partial(_ffn_kernel, final),
        grid=(B, L // tm),
        in_specs=[tok,
                  pl.BlockSpec((1, SUBLANES, D), lambda b, i: (b, jnp.maximum(i * nb - 1, 0), 0)),
                  pl.BlockSpec((1, SUBLANES, D), lambda b, i: (b, jnp.minimum((i + 1) * nb, nrow - 1), 0)),
                  mod, mod, mod, _const_spec(up.shape), full(conv_w), full(conv_b),
                  _const_spec(down.shape), full(final_w)],
        out_specs=tok,
        out_shape=jax.ShapeDtypeStruct((B, L, D), F32),
        scratch_shapes=[pltpu.VMEM((tm, down.shape[0]), BF16)],
        compiler_params=_cparams("parallel", "parallel"),
        name="ffn_final" if final else "ffn",
    )(x, x, x, shift, scale, gate, up, conv_w, conv_b, down, final_w)


def _rope_tables(L):
    rows_n = L // GRID_W
    rows = jnp.repeat(jnp.arange(rows_n, dtype=F32), GRID_W)
    cols = jnp.tile(jnp.arange(GRID_W, dtype=F32), rows_n)
    nf = HEAD_DIM // 4
    inv = ROPE_BASE ** (-jnp.arange(nf, dtype=F32) / nf)
    cr, sr = jnp.cos(rows[:, None] * inv), jnp.sin(rows[:, None] * inv)
    cc, sc = jnp.cos(cols[:, None] * inv), jnp.sin(cols[:, None] * inv)
    cos64 = jnp.concatenate([cr, cr, cc, cc], axis=-1)
    sin64 = jnp.concatenate([-sr, sr, -sc, sc], axis=-1)
    return jnp.tile(cos64, (1, 2)), jnp.tile(sin64, (1, 2))


def _hy_features(L):
    t = jnp.linspace(0.0, 1.0, L, dtype=F32)
    pos = jnp.arange(L, dtype=F32)
    f = jnp.linspace(1e-4, HY_BANDS - 1, HY_BANDS, dtype=F32)
    ang = (2.0 * math.pi / L) * pos[:, None] * f
    z = jnp.concatenate([t[:, None], jnp.cos(ang), jnp.sin(ang)], axis=-1)
    return jnp.pad(z, ((0, 0), (0, HY_POS_PAD - z.shape[1])))


def _gate_layout(a):
    g = a.reshape(a.shape[:-1] + (4, 2, 2))
    g = jnp.moveaxis(g, -2, -3).reshape(a.shape[:-1] + (2, 8))
    g = jnp.pad(g, [(0, 0)] * (g.ndim - 1) + [(0, LANES - 8)])
    return g.reshape(a.shape[:-1] + (2 * LANES,))


def _gates_t(g):
    return jnp.swapaxes(jnp.concatenate([g[..., 0:8], g[..., LANES:LANES + 8]], axis=-1), 1, 2)


def kernel(x, c, ctx, c_ctx, ada_w, ada_b, w_in, w_out, hy_conv_w, hy_conv_b, hy_w1, hy_b1, hy_w2, hy_b2,
           hy_w3, hy_b3, hy_skip, da_lambda, ml_conv_w, ml_conv_b, ml_gate_b, mix_norm_w, ffn_up,
           ffn_conv_w, ffn_conv_b, ffn_down, final_norm_w):
    B, L, D = x.shape
    lc = ctx.shape[1]
    depth = ada_w.shape[0]
    hy_w = hy_skip.shape[2]
    n_hy = 3 * hy_w
    da_w = 2 * hy_w
    ml_w = hy_w

    rows = ((B + 1 + SUBLANES - 1) // SUBLANES) * SUBLANES
    cc = jnp.zeros((rows, D), F32).at[:B].set(c).at[B].set(c_ctx)
    mods = _adaln(cc, ada_w, ada_b)

    cos_x, sin_x = _rope_tables(L)
    cos_c, sin_c = jnp.ones((lc, LANES), F32), jnp.zeros((lc, LANES), F32)
    tables = _dft_tables(L)
    fd_c, fi_c = _dense_dft_tables(lc)
    zf_x, zf_c = _hy_features(L), _hy_features(lc)
    delta = jnp.abs(jnp.linspace(math.log(HY_DECAY_TARGET) / HY_FAST_DECAY,
                                 math.log(HY_DECAY_TARGET) / HY_SLOW_DECAY, hy_w, dtype=F32))[None, :]
    swap = np.arange(da_w) ^ (HEAD_DIM // 4)
    fw = final_norm_w[None, :]

    for l in range(depth):
        lam_init = 0.8 - 0.6 * math.exp(-0.3 * l)
        update_ctx = l < depth - 1
        mx = [m[:, None, :] for m in jnp.split(mods[l, :B], 6, axis=-1)]
        mc = [jnp.broadcast_to(m[None], (B, 1, D)) for m in jnp.split(mods[l, B:B + 1], 6, axis=-1)]

        w = w_in[l]
        o = n_hy
        wq, wk, wv = w[:, o:o + da_w], w[:, o + da_w:o + 2 * da_w], w[:, o + 2 * da_w:o + 3 * da_w]
        o += 3 * da_w
        wmqk, wmv, wmo = w[:, o:o + 2 * ml_w], w[:, o + 2 * ml_w:o + 3 * ml_w], w[:, o + 3 * ml_w:o + 4 * ml_w]
        wg = _gate_layout(w[:, o + 4 * ml_w:])
        w_ext = jnp.concatenate([w[:, :n_hy], wq, wq[:, swap], wk, wk[:, swap], wv, wmqk, wmv, wmo, wg],
                                axis=1).astype(BF16)
        gate_b = _gate_layout(ml_gate_b[l])
        gate_bc = gate_b[None, :]
        gate_br = _gates_t(gate_b[None, None, :])[0]

        px = _inproj(x, mx[0], mx[1], w_ext, cos_x, sin_x)
        pc = _inproj(ctx, mc[0], mc[1], w_ext, cos_c, sin_c)
        hy_x, q_x, k_x, v_x, mqk_x, mv_x, mo_x, g_x = px
        hy_c, q_c, k_c, v_c, mqk_c, mv_c, mo_c, g_c = pc

        w1p = jnp.pad(hy_w1[l], ((0, HY_POS_PAD - hy_w1.shape[1]), (0, 0)))
        b1, b2, b3 = hy_b1[l][None, :], hy_b2[l][None, :], hy_b3[l][None, :]
        cbias = hy_conv_b[l][None, :]
        kf = _hy_filter(_hy_hidden(zf_x, w1p, b1, hy_w2[l], b2), hy_w3[l], b3, delta, tables[0], tables[1])
        z1 = _hy_conv(0, hy_x, hy_x, hy_conv_w[l], cbias, hy_skip[l], kf, tables)
        y_hy_x = _hy_conv(1, z1, hy_x, hy_conv_w[l], cbias, hy_skip[l], kf, tables)

        y_da_x = _diff_attn(q_x, [k_c, k_x], [v_c, v_x], da_lambda[l], lam_init)

        seg_c = (mqk_c, mv_c, mo_c, g_c, _gates_t(g_c))
        seg_x = (mqk_x, mv_x, mo_x, g_x, _gates_t(g_x))
        y_ml_c, y_ml_x = _mlstm(seg_c, seg_x, ml_conv_w[l], ml_conv_b[l][None, :], gate_bc, gate_br)

        gain = mix_norm_w[l][None, :]
        wo = w_out[l].astype(BF16)
        up = ffn_up[l].astype(BF16)
        down = ffn_down[l].astype(BF16)
        fcb = ffn_conv_b[l][None, :]
        x = _outproj(x, mx[2], y_hy_x, y_da_x, y_ml_x, gain, wo)
        x = _ffn(x, mx[3], mx[4], mx[5], up, ffn_conv_w[l], fcb, down, fw, final=not update_ctx)

        if update_ctx:
            y_hy_c = _hy_ctx(hy_c, _hy_hidden(zf_c, w1p, b1, hy_w2[l], b2), hy_w3[l], b3, delta,
                             hy_conv_w[l], cbias, hy_skip[l], fd_c, fi_c)
            y_da_c = _diff_attn(q_c, [k_c], [v_c], da_lambda[l], lam_init)
            ctx = _outproj(ctx, mc[2], y_hy_c, y_da_c, y_ml_c, gain, wo)
            ctx = _ffn(ctx, mc[3], mc[4], mc[5], up, ffn_conv_w[l], fcb, down, fw, final=False)
    return x
```

```python
import functools
import math

import numpy as np
import jax
import jax.numpy as jnp
from jax import lax
from jax.experimental import pallas as pl
from jax.experimental.pallas import tpu as pltpu

F32 = jnp.float32
BF16 = jnp.bfloat16
HI = lax.Precision.HIGHEST

EPS = 1e-6
HEAD_DIM = 64
GRID_W = 64
ROPE_BASE = 10000.0
HY_BANDS = 16
HY_POS_PAD = 64
HY_DECAY_TARGET = 1e-2
HY_FAST_DECAY = 0.3
HY_SLOW_DECAY = 1.5
LANES = 128
SUBLANES = 8
VMEM_LIMIT = 56 * 1024 * 1024

FFT_N1 = 64
FFT_N2 = 128
FFT_UNROLL = 8
ML_T = 256
TQ = 1024
TQ_SUB = 256
TK = 512
TM = 512
FFN_TN = 256


def _cparams(*sem):
    return pltpu.CompilerParams(dimension_semantics=sem, vmem_limit_bytes=VMEM_LIMIT)


def _const_spec(shape):
    n = len(shape)
    return pl.BlockSpec(shape, lambda *_: (0,) * n, pipeline_mode=pl.Buffered(1))


def _silu(x):
    return x / (1.0 + jnp.exp(-x))


def _sigmoid(x):
    return 1.0 / (1.0 + jnp.exp(-x))


def _log_sigmoid(x):
    return jnp.minimum(x, 0.0) - jnp.log(1.0 + jnp.exp(-jnp.abs(x)))


def _rms(x):
    return x * lax.rsqrt(jnp.mean(x * x, axis=-1, keepdims=True) + EPS)


def _group_rms_2x64(y):
    lo = lax.broadcasted_iota(jnp.int32, y.shape, 1) < 64
    y2 = y * y
    s_lo = jnp.sum(jnp.where(lo, y2, 0.0), axis=-1, keepdims=True)
    s_hi = jnp.sum(jnp.where(lo, 0.0, y2), axis=-1, keepdims=True)
    r = jnp.where(lo, lax.rsqrt(s_lo * (1.0 / 64) + EPS), lax.rsqrt(s_hi * (1.0 / 64) + EPS))
    return y * r


def _conv3(x, w, b):
    L = x.shape[0]
    row = lax.broadcasted_iota(jnp.int32, x.shape, 0)
    xm = jnp.where(row == 0, 0.0, pltpu.roll(x, 1, axis=0))
    xp = jnp.where(row == L - 1, 0.0, pltpu.roll(x, L - 1, axis=0))
    return xm * w[0:1] + x * w[1:2] + xp * w[2:3] + b


def _adaln_kernel(c_ref, w_ref, b_ref, o_ref):
    a = _silu(c_ref[...])
    o_ref[0] = jnp.dot(a, w_ref[0], precision=HI, preferred_element_type=F32) + b_ref[0]


def _adaln(cc, ada_w, ada_b):
    depth, d, n6 = ada_w.shape
    tn = 1536
    return pl.pallas_call(
        _adaln_kernel,
        grid=(depth, n6 // tn),
        in_specs=[pl.BlockSpec(cc.shape, lambda l, j: (0, 0)),
                  pl.BlockSpec((1, d, tn), lambda l, j: (l, 0, j)),
                  pl.BlockSpec((1, 1, tn), lambda l, j: (l, 0, j))],
        out_specs=pl.BlockSpec((1, cc.shape[0], tn), lambda l, j: (l, 0, j)),
        out_shape=jax.ShapeDtypeStruct((depth, cc.shape[0], n6), F32),
        compiler_params=_cparams("parallel", "parallel"),
        name="adaln",
    )(cc, ada_w, ada_b.reshape(depth, 1, n6))


_C_HY, _C_Q, _C_QS, _C_K, _C_KS, _C_V, _C_MQK, _C_MV, _C_MO, _C_G, _C_END = (
    0, 768, 1280, 1792, 2304, 2816, 3328, 3840, 4096, 4352, 4608)


def _inproj_kernel(x_ref, sh_ref, sc_ref, w_ref, cos_ref, sin_ref,
                   hy_ref, q_ref, k_ref, v_ref, mqk_ref, mv_ref, mo_ref, g_ref):
    h = (_rms(x_ref[0]) * (1.0 + sc_ref[0]) + sh_ref[0]).astype(BF16)

    def proj(lo, hi):
        return jnp.dot(h, w_ref[:, lo:hi], preferred_element_type=F32)

    hy_ref[0] = proj(_C_HY, _C_Q)
    cos = cos_ref[...]
    sin = sin_ref[...]

    def rope(c0, c1, out_ref, scale):
        a = proj(c0, c1)
        asw = proj(c1, 2 * c1 - c0)
        for j in range((c1 - c0) // LANES):
            sl = slice(LANES * j, LANES * (j + 1))
            out_ref[0, :, sl] = ((a[:, sl] * cos + asw[:, sl] * sin) * scale).astype(BF16)

    rope(_C_Q, _C_QS, q_ref, (HEAD_DIM ** -0.5) * math.log2(math.e))
    rope(_C_K, _C_KS, k_ref, 1.0)
    v_ref[0] = proj(_C_V, _C_MQK).astype(BF16)
    mqk_ref[0] = proj(_C_MQK, _C_MV)
    mv_ref[0] = proj(_C_MV, _C_MO)
    mo_ref[0] = proj(_C_MO, _C_G)
    g_ref[0] = proj(_C_G, _C_END)


def _inproj(x, shift, scale, w_ext, cos, sin):
    B, L, D = x.shape
    tm = min(TM, L)
    tok = lambda w: pl.BlockSpec((1, tm, w), lambda b, i: (b, i, 0))
    mod = pl.BlockSpec((1, 1, D), lambda b, i: (b, 0, 0))
    tab = pl.BlockSpec((tm, LANES), lambda b, i: (i, 0))
    widths = (768, 512, 512, 512, 512, 256, 256, 256)
    dtypes = (F32, BF16, BF16, BF16, F32, F32, F32, F32)
    return pl.pallas_call(
        _inproj_kernel,
        grid=(B, L // tm),
        in_specs=[tok(D), mod, mod, _const_spec(w_ext.shape), tab, tab],
        out_specs=[tok(w) for w in widths],
        out_shape=[jax.ShapeDtypeStruct((B, L, w), dt) for w, dt in zip(widths, dtypes)],
        compiler_params=_cparams("parallel", "parallel"),
        name="inproj",
    )(x, shift, scale, w_ext, cos, sin)


def _dft_tables(L):
    n1, n2, g = FFT_N1, FFT_N2, SUBLANES
    N = n1 * n2
    assert N == 2 * L
    nk = n1 // 2 + 1
    k1 = np.arange(nk)[:, None]
    m1 = np.arange(n1 // 2)[None, :]
    th = 2 * np.pi * ((k1 * m1) % n1) / n1
    f = np.stack([np.cos(th), -np.sin(th)], axis=1)
    fold = np.where((k1 == 0) | (k1 == n1 // 2), 1.0, 2.0)[:, :, None]
    eye = np.eye(g)
    m1f = np.einsum('krn,gh->krgnh', f, eye).reshape(2 * nk * g, (n1 // 2) * g)
    m1i = np.einsum('krn,gh->ngkrh', f * fold, eye).reshape((n1 // 2) * g, 2 * nk * g)
    a = np.arange(n2)
    th2 = 2 * np.pi * ((a[:, None] * a[None, :]) % n2) / n2
    fr, fi = np.cos(th2), -np.sin(th2)
    tw = 2 * np.pi * ((np.arange(nk)[:, None] * a[None, :]) % N) / N
    tr, ti = np.cos(tw)[:, None, :], -np.sin(tw)[:, None, :]
    gr = fr[None] * tr - fi[None] * ti
    gi = fr[None] * ti + fi[None] * tr
    gf = np.concatenate([np.concatenate([gr, -gi], axis=2),
                         np.concatenate([gi, gr], axis=2)], axis=1)
    gb = np.transpose(gf, (0, 2, 1))
    cvt = lambda m: jnp.asarray(m.astype(np.float32)).astype(BF16)
    return cvt(m1f), cvt(gf), cvt(gb), cvt(m1i)


def _fft_block_fwd(src3_ref, a_ref, m1f_ref):
    nh, n2, c = src3_ref.shape
    na = a_ref.shape[0]
    m1f = m1f_ref[...]
    group = lambda g: slice(SUBLANES * g, SUBLANES * (g + 1))
    for g in range(0, n2 // SUBLANES, 2):
        xg = jnp.concatenate([src3_ref[:, group(g + i), :].reshape(nh * SUBLANES, c) for i in range(2)],
                             axis=1).astype(BF16)
        out = jnp.dot(m1f, xg, preferred_element_type=F32)
        for i in range(2):
            a_ref[:, group(g + i), :] = out[:, c * i:c * (i + 1)].reshape(na, SUBLANES, c)


def _fft_block_inv(a_ref, dst3_ref, m1i_ref):
    nh, n2, c = dst3_ref.shape
    na = a_ref.shape[0]
    m1i = m1i_ref[...]
    group = lambda g: slice(SUBLANES * g, SUBLANES * (g + 1))
    for g in range(0, n2 // SUBLANES, 2):
        ag = jnp.concatenate([a_ref[:, group(g + i), :].reshape(na * SUBLANES, c) for i in range(2)],
                             axis=1).astype(BF16)
        out = jnp.dot(m1i, ag, preferred_element_type=F32)
        for i in range(2):
            dst3_ref[:, group(g + i), :] = out[:, c * i:c * (i + 1)].reshape(nh, SUBLANES, c)


def _hy_mlp(z_ref, w1_ref, b1_ref, w2_ref, b2_ref):
    h = jnp.sin(jnp.dot(z_ref[...], w1_ref[...], precision=HI, preferred_element_type=F32) + b1_ref[...])
    return jnp.sin(jnp.dot(h, w2_ref[...], precision=HI, preferred_element_type=F32) + b2_ref[...])


def _hy_taps(h2, w3_ref, b3_ref, delta_ref, zero_first):
    L = h2.shape[0]
    h = jnp.dot(h2, w3_ref[...], precision=HI, preferred_element_type=F32) + b3_ref[...]
    row = lax.broadcasted_iota(jnp.int32, h.shape, 0)
    t = row.astype(F32) * (1.0 / (L - 1))
    h = h * jnp.exp(-t * delta_ref[...])
    if zero_first:
        h = jnp.where(row == 0, 0.0, h)
    return h


def _hy_mlp_kernel(z_ref, w1_ref, b1_ref, w2_ref, b2_ref, o_ref):
    o_ref[...] = _hy_mlp(z_ref, w1_ref, b1_ref, w2_ref, b2_ref)


def _hy_hidden(zfeat, w1, b1, w2, b2):
    args = (zfeat, w1, b1, w2, b2)
    return pl.pallas_call(
        _hy_mlp_kernel,
        grid=(1,),
        in_specs=[pl.BlockSpec(a.shape, lambda i: (0, 0)) for a in args],
        out_specs=pl.BlockSpec((zfeat.shape[0], w2.shape[1]), lambda i: (0, 0)),
        out_shape=jax.ShapeDtypeStruct((zfeat.shape[0], w2.shape[1]), F32),
        compiler_params=_cparams("arbitrary"),
        name="hy_mlp",
    )(*args)


def _hy_filter_kernel(h2_ref, w3f_ref, b3f_ref, w3b_ref, b3b_ref,
                      delta_ref, m1f_ref, gf_ref, kf_ref, src_ref, af_ref, ab_ref):
    h2 = h2_ref[...]
    hf = _hy_taps(h2, w3f_ref, b3f_ref, delta_ref, False)
    hb = _hy_taps(h2, w3b_ref, b3b_ref, delta_ref, True)
    L, c = hf.shape
    n_total = 2.0 * L
    inv = 1.0 / ((jnp.sum(jnp.abs(hf), axis=0, keepdims=True)
                  + jnp.sum(jnp.abs(hb), axis=0, keepdims=True)) * n_total)
    nh, n2 = src_ref.shape[0], src_ref.shape[1]
    src_ref[...] = (hf * inv).reshape(nh, n2, c)
    _fft_block_fwd(src_ref, af_ref, m1f_ref)
    src_ref[...] = (hb * inv).reshape(nh, n2, c)
    _fft_block_fwd(src_ref, ab_ref, m1f_ref)

    def body(k1, carry):
        sl = pl.ds(pl.multiple_of(2 * k1, 2), 2)
        g = gf_ref[k1]
        xf = jnp.dot(g, af_ref[sl].reshape(2 * n2, c).astype(BF16), preferred_element_type=F32)
        xb = jnp.dot(g, ab_ref[sl].reshape(2 * n2, c).astype(BF16), preferred_element_type=F32)
        kr = xf[:n2] + xb[:n2]
        ki = xf[n2:] - xb[n2:]
        kf_ref[0, sl] = jnp.concatenate([kr, ki], axis=0).reshape(2, n2, c).astype(BF16)
        return carry

    lax.fori_loop(0, gf_ref.shape[0], body, 0, unroll=FFT_UNROLL)


def _hy_filter(h2, w3, b3, delta, m1f, gf):
    C = delta.shape[1]
    cb = LANES
    ncb = C // cb
    nh, n2 = FFT_N1 // 2, FFT_N2
    na = 2 * gf.shape[0]
    full = lambda a: pl.BlockSpec(a.shape, lambda o, j: (0,) * a.ndim)
    return pl.pallas_call(
        _hy_filter_kernel,
        grid=(2, ncb),
        in_specs=[full(h2),
                  pl.BlockSpec((w3.shape[0], cb), lambda o, j: (0, o * 2 * ncb + j)),
                  pl.BlockSpec((1, cb), lambda o, j: (0, o * 2 * ncb + j)),
                  pl.BlockSpec((w3.shape[0], cb), lambda o, j: (0, o * 2 * ncb + ncb + j)),
                  pl.BlockSpec((1, cb), lambda o, j: (0, o * 2 * ncb + ncb + j)),
                  pl.BlockSpec((1, cb), lambda o, j: (0, j)),
                  _const_spec(m1f.shape), _const_spec(gf.shape)],
        out_specs=pl.BlockSpec((1, na, n2, cb), lambda o, j: (o, 0, 0, j)),
        out_shape=jax.ShapeDtypeStruct((2, na, n2, C), BF16),
        scratch_shapes=[pltpu.VMEM((nh, n2, cb), F32), pltpu.VMEM((na, n2, cb), F32),
                        pltpu.VMEM((na, n2, cb), F32)],
        compiler_params=_cparams("arbitrary", "arbitrary"),
        name="hy_filter",
    )(h2, w3, b3, w3, b3, delta, m1f, gf)


def _hy_conv_kernel(conv_z, norm_out, z_ref, g_ref, cwz_ref, cbz_ref, cwg_ref, cbg_ref, skip_ref,
                    kf_ref, m1f_ref, gf_ref, gb_ref, m1i_ref, o_ref, zs_ref, ys_ref, a_ref):
    nh, n2, c = zs_ref.shape
    z = z_ref[0]
    if conv_z:
        z = _conv3(z, cwz_ref[...], cbz_ref[...])
    zs_ref[...] = z.reshape(nh, n2, c)
    _fft_block_fwd(zs_ref, a_ref, m1f_ref)

    def body(k1, carry):
        sl = pl.ds(pl.multiple_of(2 * k1, 2), 2)
        x = jnp.dot(gf_ref[k1], a_ref[sl].reshape(2 * n2, c).astype(BF16), preferred_element_type=F32)
        kf = kf_ref[0, sl].astype(F32)
        xr, xi, kr, ki = x[:n2], x[n2:], kf[0], kf[1]
        y = jnp.concatenate([xr * kr - xi * ki, xr * ki + xi * kr], axis=0).astype(BF16)
        a_ref[sl] = jnp.dot(gb_ref[k1], y, preferred_element_type=F32).reshape(2, n2, c)
        return carry

    lax.fori_loop(0, gf_ref.shape[0], body, 0, unroll=FFT_UNROLL)
    _fft_block_inv(a_ref, ys_ref, m1i_ref)
    y = ys_ref[...].reshape(nh * n2, c)
    z = zs_ref[...].reshape(nh * n2, c)
    gate = _conv3(g_ref[0], cwg_ref[...], cbg_ref[...])
    out = gate * (y + skip_ref[...] * z)
    if norm_out:
        out = _group_rms_2x64(out)
    o_ref[0] = out.astype(o_ref.dtype)


def _hy_conv(order, z, p_hy, conv_w, conv_b, skip, kf, tables):
    m1f, gf, gb, m1i = tables
    B, L, _ = p_hy.shape
    C = skip.shape[1]
    cb = LANES
    ncb = C // cb
    nh, n2 = FFT_N1 // 2, FFT_N2
    na = 2 * gf.shape[0]
    gcol = (order + 1) * ncb
    cw = lambda base: pl.BlockSpec((3, cb), lambda j, b: (0, base + j))
    cbs = lambda base: pl.BlockSpec((1, cb), lambda j, b: (0, base + j))
    return pl.pallas_call(
        functools.partial(_hy_conv_kernel, order == 0, order == 1),
        grid=(ncb, B),
        in_specs=[pl.BlockSpec((1, L, cb), lambda j, b: (b, 0, j)),
                  pl.BlockSpec((1, L, cb), lambda j, b: (b, 0, gcol + j)),
                  cw(0), cbs(0), cw(gcol), cbs(gcol),
                  pl.BlockSpec((1, cb), lambda j, b: (0, j)),
                  pl.BlockSpec((1, na, n2, cb), lambda j, b: (order, 0, 0, j)),
                  _const_spec(m1f.shape), _const_spec(gf.shape), _const_spec(gb.shape),
                  _const_spec(m1i.shape)],
        out_specs=pl.BlockSpec((1, L, cb), lambda j, b: (b, 0, j)),
        out_shape=jax.ShapeDtypeStruct((B, L, C), F32 if order == 0 else BF16),
        scratch_shapes=[pltpu.VMEM((nh, n2, cb), F32), pltpu.VMEM((nh, n2, cb), F32),
                        pltpu.VMEM((na, n2, cb), F32)],
        compiler_params=_cparams("arbitrary", "arbitrary"),
        name=f"hy_conv{order}",
    )(z, p_hy, conv_w, conv_b, conv_w, conv_b, skip[order:order + 1], kf, m1f, gf, gb, m1i)


def _hy_ctx_kernel(p_ref, h2_ref, w3_ref, b3_ref, delta_ref,
                   cw_ref, cb_ref, skip_ref, fd_ref, fi_ref, o_ref):
    lc = p_ref.shape[1]
    C = skip_ref.shape[1]
    h2 = h2_ref[...]
    u = _conv3(p_ref[0], cw_ref[...], cb_ref[...])
    z = u[:, 0:C]
    fd = fd_ref[...]
    fi = fi_ref[...]
    nf = fd.shape[0] // 2
    dot = lambda a, b: jnp.dot(a.astype(BF16), b.astype(BF16), preferred_element_type=F32)
    for o in range(2):
        base = 2 * o * C
        hf = _hy_taps(h2, w3_ref.at[:, base:base + C], b3_ref.at[:, base:base + C], delta_ref, False)
        hb = _hy_taps(h2, w3_ref.at[:, base + C:base + 2 * C], b3_ref.at[:, base + C:base + 2 * C],
                      delta_ref, True)
        inv = 1.0 / ((jnp.sum(jnp.abs(hf), axis=0, keepdims=True)
                      + jnp.sum(jnp.abs(hb), axis=0, keepdims=True)) * (2.0 * lc))
        xf = dot(fd, hf * inv)
        xb = dot(fd, hb * inv)
        kr = xf[:nf] + xb[:nf]
        ki = xf[nf:] - xb[nf:]
        x = dot(fd, z)
        xr, xi = x[:nf], x[nf:]
        y = dot(fi, jnp.concatenate([xr * kr - xi * ki, xr * ki + xi * kr], axis=0))
        z = u[:, (o + 1) * C:(o + 2) * C] * (y + skip_ref[o:o + 1, :] * z)
    for j in range(C // LANES):
        o_ref[0, :, j * LANES:(j + 1) * LANES] = _group_rms_2x64(z[:, j * LANES:(j + 1) * LANES]).astype(BF16)


def _dense_dft_tables(lc):
    n = 2 * lc
    k = np.arange(n)[:, None]
    t = np.arange(lc)[None, :]
    th = 2 * np.pi * ((k * t) % n) / n
    fd = np.concatenate([np.cos(th), -np.sin(th)], axis=0)
    fi = np.concatenate([np.cos(th).T, -np.sin(th).T], axis=1)
    return jnp.asarray(fd.astype(np.float32)), jnp.asarray(fi.astype(np.float32))


def _hy_ctx(p_hy, h2, w3, b3, delta, conv_w, conv_b, skip, fd, fi):
    B, lc, w = p_hy.shape
    C = skip.shape[1]
    full = lambda a: pl.BlockSpec(a.shape, lambda b: (0,) * a.ndim)
    args = (h2, w3, b3, delta, conv_w, conv_b, skip, fd, fi)
    return pl.pallas_call(
        _hy_ctx_kernel,
        grid=(B,),
        in_specs=[pl.BlockSpec((1, lc, w), lambda b: (b, 0, 0))] + [full(a) for a in args],
        out_specs=pl.BlockSpec((1, lc, C), lambda b: (b, 0, 0)),
        out_shape=jax.ShapeDtypeStruct((B, lc, C), BF16),
        compiler_params=_cparams("parallel"),
        name="hy_ctx",
    )(p_hy, *args)


def _attn_kernel(seg_lens, lam_init, q_ref, lam_ref, *refs):
    nseg = len(seg_lens)
    k_refs, v_refs = refs[:nseg], refs[nseg:2 * nseg]
    o_ref = refs[2 * nseg]
    n_s, n_p = 3, 2
    s_refs = refs[2 * nseg + 1:2 * nseg + 1 + n_s]
    p_flat = refs[2 * nseg + 1 + n_s:2 * nseg + 1 + n_s + 2 * n_p]
    p_refs = [p_flat[2 * i:2 * i + 2] for i in range(n_p)]
    tq = q_ref.shape[1]
    th = s_refs[0].shape[0]
    dl = lam_ref[...]
    lam = (jnp.exp(jnp.sum(dl[0:1] * dl[1:2], keepdims=True))
           - jnp.exp(jnp.sum(dl[2:3] * dl[3:4], keepdims=True))) + lam_init

    chunks = []
    off = 0
    for s, n in enumerate(seg_lens):
        for st in range(0, n, TK):
            w = min(TK, n - st)
            chunks.append((s, st, off, w))
            off += w

    nh = tq // th

    def lane_tiles(a):
        return [a[:, t * LANES:(t + 1) * LANES] for t in range(a.shape[1] // LANES)]

    def scores(h, m):
        q = q_ref[0, h * th:(h + 1) * th, :]
        lane = lax.broadcasted_iota(jnp.int32, q.shape, 1)
        qm = jnp.where((lane < HEAD_DIM) == (m == 0), q, jnp.zeros_like(q))
        s_ref = s_refs[(2 * h + m) % n_s]
        mxw = jnp.full((th, LANES), -jnp.inf, F32)
        for (s, st, o, w) in chunks:
            sc = lax.dot_general(qm, k_refs[s][0, st:st + w, :], (((1,), (1,)), ((), ())),
                                 preferred_element_type=F32)
            s_ref[:, o:o + w] = sc
            for t in lane_tiles(sc):
                mxw = jnp.maximum(mxw, t)
        return jnp.max(mxw, axis=-1, keepdims=True)

    def probs(h, m, mx):
        s_ref = s_refs[(2 * h + m) % n_s]
        p_ref = p_refs[h % n_p][m]
        lw = jnp.zeros((th, LANES), F32)
        for (s, st, o, w) in chunks:
            p = jnp.exp2(s_ref[:, o:o + w] - mx)
            for t in lane_tiles(p):
                lw = lw + t
            p_ref[:, o:o + w] = p.astype(BF16)
        return 1.0 / jnp.sum(lw, axis=-1, keepdims=True)

    def output(h, inv_l0, inv_l1):
        p0_ref, p1_ref = p_refs[h % n_p]
        r = jnp.broadcast_to(lam * inv_l1 / inv_l0, (th, LANES)).astype(BF16)
        acc = jnp.zeros((th, v_refs[0].shape[2]), F32)
        for (s, st, o, w) in chunks:
            rb = jnp.concatenate([r] * (w // LANES), axis=1)
            wgt = p0_ref[:, o:o + w] - p1_ref[:, o:o + w] * rb
            acc = acc + jnp.dot(wgt, v_refs[s][0, st:st + w, :], preferred_element_type=F32)
        o_ref[0, h * th:(h + 1) * th, :] = (_rms(acc * inv_l0) * (1.0 - lam_init)).astype(BF16)

    mx = {(0, 0): scores(0, 0), (0, 1): scores(0, 1)}
    for h in range(nh):
        il0 = probs(h, 0, mx[h, 0])
        if h + 1 < nh:
            mx[h + 1, 0] = scores(h + 1, 0)
        il1 = probs(h, 1, mx[h, 1])
        if h + 1 < nh:
            mx[h + 1, 1] = scores(h + 1, 1)
        output(h, il0, il1)


def _diff_attn(q, ks, vs, da_lambda, lam_init):
    B, lq, w = q.shape
    H = w // LANES
    tq = min(TQ, lq)
    th = min(TQ_SUB, tq // 2)
    seg_lens = tuple(k.shape[1] for k in ks)
    seg = lambda n: pl.BlockSpec((1, n, LANES), lambda b, h, i: (b, 0, h))
    return pl.pallas_call(
        functools.partial(_attn_kernel, seg_lens, lam_init),
        grid=(B, H, lq // tq),
        in_specs=[pl.BlockSpec((1, tq, LANES), lambda b, h, i: (b, i, h)),
                  pl.BlockSpec(da_lambda.shape, lambda b, h, i: (0, 0))]
                 + [seg(n) for n in seg_lens] + [seg(n) for n in seg_lens],
        out_specs=pl.BlockSpec((1, tq, LANES), lambda b, h, i: (b, i, h)),
        out_shape=jax.ShapeDtypeStruct((B, lq, w), BF16),
        scratch_shapes=[pltpu.VMEM((th, sum(seg_lens)), F32)] * 3
                       + [pltpu.VMEM((th, sum(seg_lens)), BF16)] * 4,
        compiler_params=_cparams("parallel", "parallel", "arbitrary"),
        name="diff_attn",
    )(q, da_lambda, *ks, *vs)


def _split3(x):
    hi = x.astype(BF16)
    r1 = x - hi.astype(F32)
    mid = r1.astype(BF16)
    return hi, mid, (r1 - mid.astype(F32)).astype(BF16)


def _mlstm_cumsums(gc, gr, tri_lo, tri_up):
    dot = lambda a, b: jnp.dot(a, b, preferred_element_type=F32)
    lc = _split3(_log_sigmoid(gc))
    lr = _split3(_log_sigmoid(gr))
    pre_c = sum(dot(tri_lo, t) for t in lc)
    suf_c = sum(dot(tri_up, t) for t in lc)
    pre_r = sum(dot(t, tri_up) for t in lr)
    suf_r = sum(dot(t, tri_lo) for t in lr)
    fwd_lane = lax.broadcasted_iota(jnp.int32, gc.shape, 1) < 4
    fwd_row = lax.broadcasted_iota(jnp.int32, gr.shape, 0) < 4
    return jnp.where(fwd_lane, pre_c, suf_c), jnp.where(fwd_row, pre_r, suf_r)


def _mlstm_chunk(rev, q, k, vt, gc, gr, csum_c, csum_r, c_ref, n_ref, m_ref):
    T = q.shape[0]
    d = 6 if rev else 2
    ii = 4 if rev else 0
    lo = lax.broadcasted_iota(jnp.int32, (T, LANES), 1) < HEAD_DIM
    lo_row = lax.broadcasted_iota(jnp.int32, (1, LANES), 1) < HEAD_DIM
    sub = lax.broadcasted_iota(jnp.int32, (LANES, T), 0) < HEAD_DIM
    key_i = lax.broadcasted_iota(jnp.int32, (T, T), 0)
    qry_i = lax.broadcasted_iota(jnp.int32, (T, T), 1)
    mask = (key_i >= qry_i) if rev else (key_i <= qry_i)
    last = 0 if rev else T - 1
    qb = q.astype(BF16)
    kb = k.astype(BF16)
    vtb = vt.astype(BF16)
    zero = jnp.zeros_like(kb)
    nt = (((1,), (1,)), ((), ()))
    brow, igrow, mloc, numt, dsum = [], [], [], [], []
    for j in range(2):
        brow.append(csum_r[d + j:d + j + 1, :])
        igrow.append(gr[ii + j:ii + j + 1, :])
        ccol = gc[:, ii + j:ii + j + 1] - csum_c[:, d + j:d + j + 1]
        logd = jnp.where(mask, brow[j] + ccol, -jnp.inf)
        mloc.append(jnp.max(logd, axis=0, keepdims=True))
        kj = jnp.where(lo, kb, zero) if j == 0 else jnp.where(lo, zero, kb)
        st = lax.dot_general(kj, qb, nt, preferred_element_type=F32) * jnp.exp(logd - mloc[j])
        numt.append(jnp.dot(vtb, st.astype(BF16), preferred_element_type=F32))
        dsum.append(jnp.sum(st, axis=0, keepdims=True))
    num_loc = jnp.where(sub, numt[0], numt[1])

    n = n_ref[...]
    row8 = lax.broadcasted_iota(jnp.int32, (SUBLANES, LANES), 0)
    lo8 = lax.broadcasted_iota(jnp.int32, (SUBLANES, LANES), 1) < HEAD_DIM
    n8 = jnp.where(row8 == jnp.where(lo8, 0, 1), jnp.broadcast_to(n, (SUBLANES, LANES)), 0.0)
    qn = lax.dot_general(n8.astype(BF16), qb, nt, preferred_element_type=F32)
    hden, wrow, decay, interw, aloc, m_new = [], [], [], [], [], []
    for j in range(2):
        m_prev = m_ref[j:j + 1, 0:1]
        mrow = jnp.maximum(brow[j] + m_prev, mloc[j])
        a = jnp.exp(mloc[j] - mrow)
        iw = jnp.exp(brow[j] + m_prev - mrow)
        dn = a * dsum[j] + iw * qn[j:j + 1, :]
        hden.append(jnp.maximum(jnp.abs(dn), jnp.exp(-mrow)))
        aloc.append(a)
        interw.append(iw)
        mn = mrow[:, last:last + 1]
        btot = brow[j][:, last:last + 1]
        wrow.append(jnp.exp(btot - brow[j] + igrow[j] - mn))
        decay.append(jnp.exp(btot + m_prev - mn))
        m_new.append(mn)
    c = c_ref[...]
    intert = lax.dot_general(c.astype(BF16), qb, nt, preferred_element_type=F32)
    num = jnp.where(sub, aloc[0], aloc[1]) * num_loc + jnp.where(sub, interw[0], interw[1]) * intert
    ht = num / jnp.where(sub, hden[0], hden[1])
    wvt = (vt * jnp.where(sub, wrow[0], wrow[1])).astype(BF16)
    upd = jnp.dot(wvt, kb, preferred_element_type=F32)
    rr = lax.broadcasted_iota(jnp.int32, (LANES, LANES), 0) < HEAD_DIM
    cc = lax.broadcasted_iota(jnp.int32, (LANES, LANES), 1) < HEAD_DIM
    dcol = jnp.where(lax.broadcasted_iota(jnp.int32, (LANES, 1), 0) < HEAD_DIM, decay[0], decay[1])
    c_ref[...] = dcol * c + jnp.where(rr == cc, upd, 0.0)
    rowt = lax.broadcasted_iota(jnp.int32, (SUBLANES, T), 0)
    w8 = jnp.where(rowt == 0, wrow[0], jnp.where(rowt == 1, wrow[1], 0.0))
    r8 = jnp.dot(w8.astype(BF16), kb, preferred_element_type=F32)
    n_ref[...] = jnp.where(lo_row, decay[0] * n + r8[0:1, :], decay[1] * n + r8[1:2, :])
    m_ref[0:1, :] = jnp.broadcast_to(m_new[0], (1, LANES))
    m_ref[1:2, :] = jnp.broadcast_to(m_new[1], (1, LANES))
    return ht


def _mlstm_kernel(qc_ref, kc_ref, vc_ref, oc_ref, gcc_ref, grc_ref,
                  qx_ref, kx_ref, vx_ref, ox_ref, gcx_ref, grx_ref,
                  cwq_ref, cbq_ref, cwk_ref, cbk_ref, gbc_ref, gbr_ref,
                  yc_ref, yx_ref,
                  qs_ref, ks_ref, vt_ref, hf_ref, hb_ref, csc_ref, csr_ref, ct_ref, n_ref, m_ref):
    T = ML_T
    lx = qx_ref.shape[1]
    nx = lx // T
    r_i = lax.broadcasted_iota(jnp.int32, (T, T), 0)
    c_i = lax.broadcasted_iota(jnp.int32, (T, T), 1)
    tri_lo = jnp.where(c_i <= r_i, 1.0, 0.0).astype(BF16)
    tri_up = jnp.where(c_i >= r_i, 1.0, 0.0).astype(BF16)
    ct_ref[...] = jnp.zeros_like(ct_ref)
    n_ref[...] = jnp.zeros_like(n_ref)
    m_ref[...] = jnp.zeros_like(m_ref)
    scale = HEAD_DIM ** -0.5
    cwq, cbq, cwk, cbk = cwq_ref[...], cbq_ref[...], cwk_ref[...], cbk_ref[...]
    gbc, gbr = gbc_ref[...], gbr_ref[...]

    qc = _silu(_conv3(qc_ref[0], cwq, cbq))
    kc = _silu(_conv3(kc_ref[0], cwk, cbk)) * scale
    vct = vc_ref[0].T
    gcc = gcc_ref[0] + gbc
    grc = grc_ref[0] + gbr
    h_ct = None
    csc, csr = _mlstm_cumsums(gcc, grc, tri_lo, tri_up)
    for rev in (False, True):
        di = int(rev)
        ht = _mlstm_chunk(rev, qc, kc, vct, gcc, grc, csc, csr,
                          ct_ref.at[di], n_ref.at[di], m_ref.at[di])
        h_ct = ht if h_ct is None else h_ct + ht
    yc_ref[0] = _group_rms_2x64(h_ct.T * _sigmoid(oc_ref[0])).astype(BF16)

    qs_ref[...] = _silu(_conv3(qx_ref[0], cwq, cbq))
    ks_ref[...] = _silu(_conv3(kx_ref[0], cwk, cbk)) * scale
    for c in range(nx):
        rows = slice(c * T, (c + 1) * T)
        vt_ref[:, rows] = vx_ref[0, rows, :].T
        csc_ref[rows, :], csr_ref[:, rows] = _mlstm_cumsums(gcx_ref[0, rows, :] + gbc,
                                                            grx_ref[0, :, rows] + gbr, tri_lo, tri_up)

    def body(c, carry):
        for rev in (False, True):
            di = int(rev)
            cidx = (nx - 1 - c) if rev else c
            r0 = pl.multiple_of(cidx * T, T)
            rows = pl.ds(r0, T)
            ht = _mlstm_chunk(rev, qs_ref[rows, :], ks_ref[rows, :], vt_ref[:, rows],
                              gcx_ref[0, rows, :] + gbc, grx_ref[0, :, rows] + gbr,
                              csc_ref[rows, :], csr_ref[:, rows],
                              ct_ref.at[di], n_ref.at[di], m_ref.at[di])
            if rev:
                hb_ref[:, rows] = ht
            else:
                hf_ref[:, rows] = ht
        return carry

    lax.fori_loop(0, nx, body, 0, unroll=4)
    for c in range(nx):
        rows = slice(c * T, (c + 1) * T)
        h = (hf_ref[:, rows] + hb_ref[:, rows]).T
        yx_ref[0, rows, :] = _group_rms_2x64(h * _sigmoid(ox_ref[0, rows, :])).astype(BF16)


def _mlstm(seg_c, seg_x, conv_w, conv_b, gate_bc, gate_br):
    B, lx, _ = seg_x[1].shape
    lc = seg_c[1].shape[1]
    assert lc == ML_T and lx % ML_T == 0

    def seg_specs(L):
        col = lambda off: pl.BlockSpec((1, L, LANES), lambda b, p, off=off: (b, 0, off + p))
        return [col(0), col(2), col(0), col(0), col(0),
                pl.BlockSpec((1, SUBLANES, L), lambda b, p: (b, p, 0))]

    def seg_args(s):
        qk, v, o, g, gt = s
        return [qk, qk, v, o, g, gt]

    wspec = lambda rows, off: pl.BlockSpec((rows, LANES), lambda b, p, off=off: (0, off + p))
    return pl.pallas_call(
        _mlstm_kernel,
        grid=(B, 2),
        in_specs=seg_specs(lc) + seg_specs(lx)
                 + [wspec(3, 0), wspec(1, 0), wspec(3, 2), wspec(1, 2), wspec(1, 0),
                    pl.BlockSpec((SUBLANES, 1), lambda b, p: (p, 0))],
        out_specs=[pl.BlockSpec((1, lc, LANES), lambda b, p: (b, 0, p)),
                   pl.BlockSpec((1, lx, LANES), lambda b, p: (b, 0, p))],
        out_shape=[jax.ShapeDtypeStruct((B, lc, 2 * LANES), BF16),
                   jax.ShapeDtypeStruct((B, lx, 2 * LANES), BF16)],
        scratch_shapes=[pltpu.VMEM((lx, LANES), F32), pltpu.VMEM((lx, LANES), F32),
                        pltpu.VMEM((LANES, lx), F32), pltpu.VMEM((LANES, lx), F32),
                        pltpu.VMEM((LANES, lx), F32),
                        pltpu.VMEM((lx, LANES), F32), pltpu.VMEM((SUBLANES, lx), F32),
                        pltpu.VMEM((2, LANES, LANES), F32), pltpu.VMEM((2, 1, LANES), F32),
                        pltpu.VMEM((2, 2, LANES), F32)],
        compiler_params=_cparams("parallel", "arbitrary"),
        name="mlstm",
    )(*seg_args(seg_c), *seg_args(seg_x), conv_w, conv_b, conv_w, conv_b, gate_bc, gate_br)


def _outproj_kernel(x_ref, g_ref, hy_ref, da_ref, ml_ref, gain_ref, w_ref, o_ref):
    c_hy = hy_ref.shape[2]
    c_da = da_ref.shape[2]
    gain = gain_ref[...]
    acc = jnp.dot((hy_ref[0] * gain[:, 0:c_hy]).astype(BF16), w_ref[0:c_hy, :], preferred_element_type=F32)
    acc += jnp.dot((da_ref[0] * gain[:, c_hy:c_hy + c_da]).astype(BF16), w_ref[c_hy:c_hy + c_da, :],
                   preferred_element_type=F32)
    acc += jnp.dot((ml_ref[0] * gain[:, c_hy + c_da:]).astype(BF16), w_ref[c_hy + c_da:, :],
                   preferred_element_type=F32)
    o_ref[0] = x_ref[0] + g_ref[0] * acc


def _outproj(x, gate, y_hy, y_da, y_ml, gain, w_out):
    B, L, D = x.shape
    tm = min(TM, L)
    tok = lambda w: pl.BlockSpec((1, tm, w), lambda b, i: (b, i, 0))
    return pl.pallas_call(
        _outproj_kernel,
        grid=(B, L // tm),
        in_specs=[tok(D), pl.BlockSpec((1, 1, D), lambda b, i: (b, 0, 0)),
                  tok(y_hy.shape[2]), tok(y_da.shape[2]), tok(y_ml.shape[2]),
                  pl.BlockSpec(gain.shape, lambda b, i: (0, 0)), _const_spec(w_out.shape)],
        out_specs=tok(D),
        out_shape=jax.ShapeDtypeStruct((B, L, D), F32),
        compiler_params=_cparams("parallel", "parallel"),
        name="outproj",
    )(x, gate, y_hy, y_da, y_ml, gain, w_out)


def _ffn_kernel(final, x_ref, prev_ref, next_ref, sh_ref, sc_ref, g_ref, up_ref, cw_ref, cb_ref,
                down_ref, fw_ref, o_ref, act_ref):
    i = pl.program_id(1)
    last = pl.num_programs(1) - 1
    tm = x_ref.shape[1]
    d_ff = down_ref.shape[0]
    x = x_ref[0]
    mod = lambda a: _rms(a) * (1.0 + sc_ref[0]) + sh_ref[0]
    hp = jnp.where(i == 0, 0.0, mod(prev_ref[0]))
    hn = jnp.where(i == last, 0.0, mod(next_ref[0]))
    h = jnp.concatenate([hp, mod(x), hn], axis=0).astype(BF16)
    ext = tm + 2 * SUBLANES

    def conv_cols(lo, hi):
        u = jnp.dot(h, up_ref[:, lo:hi], preferred_element_type=F32)
        w = cw_ref[:, lo:hi]
        c = (pltpu.roll(u, 1, axis=0) * w[0:1] + u * w[1:2] + pltpu.roll(u, ext - 1, axis=0) * w[2:3]
             + cb_ref[:, lo:hi])
        return c[SUBLANES:SUBLANES + tm]

    for j in range(d_ff // FFN_TN):
        lo = j * FFN_TN
        a = conv_cols(lo, lo + FFN_TN)
        g = conv_cols(d_ff + lo, d_ff + lo + FFN_TN)
        act_ref[:, lo:lo + FFN_TN] = (_silu(g) * a).astype(BF16)
    y = x + g_ref[0] * jnp.dot(act_ref[...], down_ref[...], preferred_element_type=F32)
    if final:
        y = _rms(y) * fw_ref[...]
    o_ref[0] = y


def _ffn(x, shift, scale, gate, up, conv_w, conv_b, down, final_w, final):
    B, L, D = x.shape
    tm = min(TM, L)
    nb = tm // SUBLANES
    nrow = L // SUBLANES
    tok = pl.BlockSpec((1, tm, D), lambda b, i: (b, i, 0))
    mod = pl.BlockSpec((1, 1, D), lambda b, i: (b, 0, 0))
    full = lambda a: pl.BlockSpec(a.shape, lambda b, i: (0,) * a.ndim)
    return pl.pallas_call(
        functools.partial(_ffn_kernel, final),
        grid=(B, L // tm),
        in_specs=[tok,
                  pl.BlockSpec((1, SUBLANES, D), lambda b, i: (b, jnp.maximum(i * nb - 1, 0), 0)),
                  pl.BlockSpec((1, SUBLANES, D), lambda b, i: (b, jnp.minimum((i + 1) * nb, nrow - 1), 0)),
                  mod, mod, mod, _const_spec(up.shape), full(conv_w), full(conv_b),
                  _const_spec(down.shape), full(final_w)],
        out_specs=tok,
        out_shape=jax.ShapeDtypeStruct((B, L, D), F32),
        scratch_shapes=[pltpu.VMEM((tm, down.shape[0]), BF16)],
        compiler_params=_cparams("parallel", "parallel"),
        name="ffn_final" if final else "ffn",
    )(x, x, x, shift, scale, gate, up, conv_w, conv_b, down, final_w)


def _rope_tables(L):
    rows_n = L // GRID_W
    rows = jnp.repeat(jnp.arange(rows_n, dtype=F32), GRID_W)
    cols = jnp.tile(jnp.arange(GRID_W, dtype=F32), rows_n)
    nf = HEAD_DIM // 4
    inv = ROPE_BASE ** (-jnp.arange(nf, dtype=F32) / nf)
    cr, sr = jnp.cos(rows[:, None] * inv), jnp.sin(rows[:, None] * inv)
    cc, sc = jnp.cos(cols[:, None] * inv), jnp.sin(cols[:, None] * inv)
    cos64 = jnp.concatenate([cr, cr, cc, cc], axis=-1)
    sin64 = jnp.concatenate([-sr, sr, -sc, sc], axis=-1)
    return jnp.tile(cos64, (1, 2)), jnp.tile(sin64, (1, 2))


def _hy_features(L):
    t = jnp.linspace(0.0, 1.0, L, dtype=F32)
    pos = jnp.arange(L, dtype=F32)
    f = jnp.linspace(1e-4, HY_BANDS - 1, HY_BANDS, dtype=F32)
    ang = (2.0 * math.pi / L) * pos[:, None] * f
    z = jnp.concatenate([t[:, None], jnp.cos(ang), jnp.sin(ang)], axis=-1)
    return jnp.pad(z, ((0, 0), (0, HY_POS_PAD - z.shape[1])))


def _gate_layout(a):
    g = a.reshape(a.shape[:-1] + (4, 2, 2))
    g = jnp.moveaxis(g, -2, -3).reshape(a.shape[:-1] + (2, 8))
    g = jnp.pad(g, [(0, 0)] * (g.ndim - 1) + [(0, LANES - 8)])
    return g.reshape(a.shape[:-1] + (2 * LANES,))


def _gates_t(g):
    return jnp.swapaxes(jnp.concatenate([g[..., 0:8], g[..., LANES:LANES + 8]], axis=-1), 1, 2)


def kernel(x, c, ctx, c_ctx, ada_w, ada_b, w_in, w_out, hy_conv_w, hy_conv_b, hy_w1, hy_b1, hy_w2, hy_b2,
           hy_w3, hy_b3, hy_skip, da_lambda, ml_conv_w, ml_conv_b, ml_gate_b, mix_norm_w, ffn_up,
           ffn_conv_w, ffn_conv_b, ffn_down, final_norm_w):
    B, L, D = x.shape
    lc = ctx.shape[1]
    depth = ada_w.shape[0]
    hy_w = hy_skip.shape[2]
    n_hy = 3 * hy_w
    da_w = 2 * hy_w
    ml_w = hy_w

    rows = ((B + 1 + SUBLANES - 1) // SUBLANES) * SUBLANES
    cc = jnp.zeros((rows, D), F32).at[:B].set(c).at[B].set(c_ctx)
    mods = _adaln(cc, ada_w, ada_b)

    cos_x, sin_x = _rope_tables(L)
    cos_c, sin_c = jnp.ones((lc, LANES), F32), jnp.zeros((lc, LANES), F32)
    tables = _dft_tables(L)
    fd_c, fi_c = _dense_dft_tables(lc)
    zf_x, zf_c = _hy_features(L), _hy_features(lc)
    delta = jnp.abs(jnp.linspace(math.log(HY_DECAY_TARGET) / HY_FAST_DECAY,
                                 math.log(HY_DECAY_TARGET) / HY_SLOW_DECAY, hy_w, dtype=F32))[None, :]
    swap = np.arange(da_w) ^ (HEAD_DIM // 4)
    fw = final_norm_w[None, :]

    for l in range(depth):
        lam_init = 0.8 - 0.6 * math.exp(-0.3 * l)
        update_ctx = l < depth - 1
        mx = [m[:, None, :] for m in jnp.split(mods[l, :B], 6, axis=-1)]
        mc = [jnp.broadcast_to(m[None], (B, 1, D)) for m in jnp.split(mods[l, B:B + 1], 6, axis=-1)]

        w = w_in[l]
        o = n_hy
        wq, wk, wv = w[:, o:o + da_w], w[:, o + da_w:o + 2 * da_w], w[:, o + 2 * da_w:o + 3 * da_w]
        o += 3 * da_w
        wmqk, wmv, wmo = w[:, o:o + 2 * ml_w], w[:, o + 2 * ml_w:o + 3 * ml_w], w[:, o + 3 * ml_w:o + 4 * ml_w]
        wg = _gate_layout(w[:, o + 4 * ml_w:])
        w_ext = jnp.concatenate([w[:, :n_hy], wq, wq[:, swap], wk, wk[:, swap], wv, wmqk, wmv, wmo, wg],
                                axis=1).astype(BF16)
        gate_b = _gate_layout(ml_gate_b[l])
        gate_bc = gate_b[None, :]
        gate_br = _gates_t(gate_b[None, None, :])[0]

        px = _inproj(x, mx[0], mx[1], w_ext, cos_x, sin_x)
        pc = _inproj(ctx, mc[0], mc[1], w_ext, cos_c, sin_c)
        hy_x, q_x, k_x, v_x, mqk_x, mv_x, mo_x, g_x = px
        hy_c, q_c, k_c, v_c, mqk_c, mv_c, mo_c, g_c = pc

        w1p = jnp.pad(hy_w1[l], ((0, HY_POS_PAD - hy_w1.shape[1]), (0, 0)))
        b1, b2, b3 = hy_b1[l][None, :], hy_b2[l][None, :], hy_b3[l][None, :]
        cbias = hy_conv_b[l][None, :]
        kf = _hy_filter(_hy_hidden(zf_x, w1p, b1, hy_w2[l], b2), hy_w3[l], b3, delta, tables[0], tables[1])
        z1 = _hy_conv(0, hy_x, hy_x, hy_conv_w[l], cbias, hy_skip[l], kf, tables)
        y_hy_x = _hy_conv(1, z1, hy_x, hy_conv_w[l], cbias, hy_skip[l], kf, tables)

        y_da_x = _diff_attn(q_x, [k_c, k_x], [v_c, v_x], da_lambda[l], lam_init)

        seg_c = (mqk_c, mv_c, mo_c, g_c, _gates_t(g_c))
        seg_x = (mqk_x, mv_x, mo_x, g_x, _gates_t(g_x))
        y_ml_c, y_ml_x = _mlstm(seg_c, seg_x, ml_conv_w[l], ml_conv_b[l][None, :], gate_bc, gate_br)

        gain = mix_norm_w[l][None, :]
        wo = w_out[l].astype(BF16)
        up = ffn_up[l].astype(BF16)
        down = ffn_down[l].astype(BF16)
        fcb = ffn_conv_b[l][None, :]
        x = _outproj(x, mx[2], y_hy_x, y_da_x, y_ml_x, gain, wo)
        x = _ffn(x, mx[3], mx[4], mx[5], up, ffn_conv_w[l], fcb, down, fw, final=not update_ctx)

        if update_ctx:
            y_hy_c = _hy_ctx(hy_c, _hy_hidden(zf_c, w1p, b1, hy_w2[l], b2), hy_w3[l], b3, delta,
                             hy_conv_w[l], cbias, hy_skip[l], fd_c, fi_c)
            y_da_c = _diff_attn(q_c, [k_c], [v_c], da_lambda[l], lam_init)
            ctx = _outproj(ctx, mc[2], y_hy_c, y_da_c, y_ml_c, gain, wo)
            ctx = _ffn(ctx, mc[3], mc[4], mc[5], up, ffn_conv_w[l], fcb, down, fw, final=False)
    return x
```

```python
import functools
import math

import numpy as np
import jax
import jax.numpy as jnp
from jax import lax
from jax.experimental import pallas as pl
from jax.experimental.pallas import tpu as pltpu

F32 = jnp.float32
BF16 = jnp.bfloat16
HI = lax.Precision.HIGHEST

EPS = 1e-6
HEAD_DIM = 64
GRID_W = 64
ROPE_BASE = 10000.0
HY_BANDS = 16
HY_POS_PAD = 64
HY_DECAY_TARGET = 1e-2
HY_FAST_DECAY = 0.3
HY_SLOW_DECAY = 1.5
LANES = 128
SUBLANES = 8
VMEM_LIMIT = 56 * 1024 * 1024

FFT_N1 = 64
FFT_N2 = 128
FFT_UNROLL = 8
ML_T = 256
TQ = 1024
TQ_SUB = 256
TK = 512
TM = 1024
FFN_TN = 256


def _cparams(*sem):
    return pltpu.CompilerParams(dimension_semantics=sem, vmem_limit_bytes=VMEM_LIMIT)


def _const_spec(shape):
    n = len(shape)
    return pl.BlockSpec(shape, lambda *_: (0,) * n, pipeline_mode=pl.Buffered(1))


def _silu(x):
    return x / (1.0 + jnp.exp(-x))


def _sigmoid(x):
    return 1.0 / (1.0 + jnp.exp(-x))


def _log_sigmoid(x):
    return jnp.minimum(x, 0.0) - jnp.log(1.0 + jnp.exp(-jnp.abs(x)))


def _rms(x):
    return x * lax.rsqrt(jnp.mean(x * x, axis=-1, keepdims=True) + EPS)


def _group_rms_2x64(y):
    lo = lax.broadcasted_iota(jnp.int32, y.shape, 1) < 64
    y2 = y * y
    s_lo = jnp.sum(jnp.where(lo, y2, 0.0), axis=-1, keepdims=True)
    s_hi = jnp.sum(jnp.where(lo, 0.0, y2), axis=-1, keepdims=True)
    r = jnp.where(lo, lax.rsqrt(s_lo * (1.0 / 64) + EPS), lax.rsqrt(s_hi * (1.0 / 64) + EPS))
    return y * r


def _conv3(x, w, b):
    L = x.shape[0]
    row = lax.broadcasted_iota(jnp.int32, x.shape, 0)
    xm = jnp.where(row == 0, 0.0, pltpu.roll(x, 1, axis=0))
    xp = jnp.where(row == L - 1, 0.0, pltpu.roll(x, L - 1, axis=0))
    return xm * w[0:1] + x * w[1:2] + xp * w[2:3] + b


def _adaln_kernel(c_ref, w_ref, b_ref, o_ref):
    a = _silu(c_ref[...])
    o_ref[0] = jnp.dot(a, w_ref[0], precision=HI, preferred_element_type=F32) + b_ref[0]


def _adaln(cc, ada_w, ada_b):
    depth, d, n6 = ada_w.shape
    tn = 1536
    return pl.pallas_call(
        _adaln_kernel,
        grid=(depth, n6 // tn),
        in_specs=[pl.BlockSpec(cc.shape, lambda l, j: (0, 0)),
                  pl.BlockSpec((1, d, tn), lambda l, j: (l, 0, j)),
                  pl.BlockSpec((1, 1, tn), lambda l, j: (l, 0, j))],
        out_specs=pl.BlockSpec((1, cc.shape[0], tn), lambda l, j: (l, 0, j)),
        out_shape=jax.ShapeDtypeStruct((depth, cc.shape[0], n6), F32),
        compiler_params=_cparams("parallel", "parallel"),
        name="adaln",
    )(cc, ada_w, ada_b.reshape(depth, 1, n6))


_C_HY, _C_Q, _C_QS, _C_K, _C_KS, _C_V, _C_MQK, _C_MV, _C_MO, _C_G, _C_END = (
    0, 768, 1280, 1792, 2304, 2816, 3328, 3840, 4096, 4352, 4608)


def _inproj_kernel(x_ref, sh_ref, sc_ref, w_ref, cos_ref, sin_ref,
                   hy_ref, q_ref, k_ref, v_ref, mqk_ref, mv_ref, mo_ref, g_ref):
    h = (_rms(x_ref[0]) * (1.0 + sc_ref[0]) + sh_ref[0]).astype(BF16)

    def proj(lo, hi):
        return jnp.dot(h, w_ref[:, lo:hi], preferred_element_type=F32)

    hy_ref[0] = proj(_C_HY, _C_Q)
    cos = cos_ref[...]
    sin = sin_ref[...]

    def rope(c0, c1, out_ref, scale):
        a = proj(c0, c1)
        asw = proj(c1, 2 * c1 - c0)
        for j in range((c1 - c0) // LANES):
            sl = slice(LANES * j, LANES * (j + 1))
            out_ref[0, :, sl] = ((a[:, sl] * cos + asw[:, sl] * sin) * scale).astype(BF16)

    rope(_C_Q, _C_QS, q_ref, (HEAD_DIM ** -0.5) * math.log2(math.e))
    rope(_C_K, _C_KS, k_ref, 1.0)
    v_ref[0] = proj(_C_V, _C_MQK).astype(BF16)
    mqk_ref[0] = proj(_C_MQK, _C_MV)
    mv_ref[0] = proj(_C_MV, _C_MO)
    mo_ref[0] = proj(_C_MO, _C_G)
    g_ref[0] = proj(_C_G, _C_END)


def _inproj(x, shift, scale, w_ext, cos, sin):
    B, L, D = x.shape
    tm = min(TM, L)
    tok = lambda w: pl.BlockSpec((1, tm, w), lambda b, i: (b, i, 0))
    mod = pl.BlockSpec((1, 1, D), lambda b, i: (b, 0, 0))
    tab = pl.BlockSpec((tm, LANES), lambda b, i: (i, 0))
    widths = (768, 512, 512, 512, 512, 256, 256, 256)
    dtypes = (F32, BF16, BF16, BF16, F32, F32, F32, F32)
    return pl.pallas_call(
        _inproj_kernel,
        grid=(B, L // tm),
        in_specs=[tok(D), mod, mod, _const_spec(w_ext.shape), tab, tab],
        out_specs=[tok(w) for w in widths],
        out_shape=[jax.ShapeDtypeStruct((B, L, w), dt) for w, dt in zip(widths, dtypes)],
        compiler_params=_cparams("parallel", "parallel"),
        name="inproj",
    )(x, shift, scale, w_ext, cos, sin)


def _dft_tables(L):
    n1, n2, g = FFT_N1, FFT_N2, SUBLANES
    N = n1 * n2
    assert N == 2 * L
    nk = n1 // 2 + 1
    k1 = np.arange(nk)[:, None]
    m1 = np.arange(n1 // 2)[None, :]
    th = 2 * np.pi * ((k1 * m1) % n1) / n1
    f = np.stack([np.cos(th), -np.sin(th)], axis=1)
    fold = np.where((k1 == 0) | (k1 == n1 // 2), 1.0, 2.0)[:, :, None]
    eye = np.eye(g)
    m1f = np.einsum('krn,gh->krgnh', f, eye).reshape(2 * nk * g, (n1 // 2) * g)
    m1i = np.einsum('krn,gh->ngkrh', f * fold, eye).reshape((n1 // 2) * g, 2 * nk * g)
    a = np.arange(n2)
    th2 = 2 * np.pi * ((a[:, None] * a[None, :]) % n2) / n2
    fr, fi = np.cos(th2), -np.sin(th2)
    tw = 2 * np.pi * ((np.arange(nk)[:, None] * a[None, :]) % N) / N
    tr, ti = np.cos(tw)[:, None, :], -np.sin(tw)[:, None, :]
    gr = fr[None] * tr - fi[None] * ti
    gi = fr[None] * ti + fi[None] * tr
    gf = np.concatenate([np.concatenate([gr, -gi], axis=2),
                         np.concatenate([gi, gr], axis=2)], axis=1)
    gb = np.transpose(gf, (0, 2, 1))
    cvt = lambda m: jnp.asarray(m.astype(np.float32)).astype(BF16)
    return cvt(m1f), cvt(gf), cvt(gb), cvt(m1i)


def _fft_block_fwd(src3_ref, a_ref, m1f_ref):
    nh, n2, c = src3_ref.shape
    na = a_ref.shape[0]
    m1f = m1f_ref[...]
    group = lambda g: slice(SUBLANES * g, SUBLANES * (g + 1))
    for g in range(0, n2 // SUBLANES, 2):
        xg = jnp.concatenate([src3_ref[:, group(g + i), :].reshape(nh * SUBLANES, c) for i in range(2)],
                             axis=1).astype(BF16)
        out = jnp.dot(m1f, xg, preferred_element_type=F32)
        for i in range(2):
            a_ref[:, group(g + i), :] = out[:, c * i:c * (i + 1)].reshape(na, SUBLANES, c)


def _fft_block_inv(a_ref, dst3_ref, m1i_ref):
    nh, n2, c = dst3_ref.shape
    na = a_ref.shape[0]
    m1i = m1i_ref[...]
    group = lambda g: slice(SUBLANES * g, SUBLANES * (g + 1))
    for g in range(0, n2 // SUBLANES, 2):
        ag = jnp.concatenate([a_ref[:, group(g + i), :].reshape(na * SUBLANES, c) for i in range(2)],
                             axis=1).astype(BF16)
        out = jnp.dot(m1i, ag, preferred_element_type=F32)
        for i in range(2):
            dst3_ref[:, group(g + i), :] = out[:, c * i:c * (i + 1)].reshape(nh, SUBLANES, c)


def _hy_mlp(z_ref, w1_ref, b1_ref, w2_ref, b2_ref):
    h = jnp.sin(jnp.dot(z_ref[...], w1_ref[...], precision=HI, preferred_element_type=F32) + b1_ref[...])
    return jnp.sin(jnp.dot(h, w2_ref[...], precision=HI, preferred_element_type=F32) + b2_ref[...])


def _hy_taps(h2, w3_ref, b3_ref, delta_ref, zero_first):
    L = h2.shape[0]
    h = jnp.dot(h2, w3_ref[...], precision=HI, preferred_element_type=F32) + b3_ref[...]
    row = lax.broadcasted_iota(jnp.int32, h.shape, 0)
    t = row.astype(F32) * (1.0 / (L - 1))
    h = h * jnp.exp(-t * delta_ref[...])
    if zero_first:
        h = jnp.where(row == 0, 0.0, h)
    return h


def _hy_mlp_kernel(z_ref, w1_ref, b1_ref, w2_ref, b2_ref, o_ref):
    o_ref[...] = _hy_mlp(z_ref, w1_ref, b1_ref, w2_ref, b2_ref)


def _hy_hidden(zfeat, w1, b1, w2, b2):
    args = (zfeat, w1, b1, w2, b2)
    return pl.pallas_call(
        _hy_mlp_kernel,
        grid=(1,),
        in_specs=[pl.BlockSpec(a.shape, lambda i: (0, 0)) for a in args],
        out_specs=pl.BlockSpec((zfeat.shape[0], w2.shape[1]), lambda i: (0, 0)),
        out_shape=jax.ShapeDtypeStruct((zfeat.shape[0], w2.shape[1]), F32),
        compiler_params=_cparams("arbitrary"),
        name="hy_mlp",
    )(*args)


def _hy_filter_kernel(h2_ref, w3f_ref, b3f_ref, w3b_ref, b3b_ref,
                      delta_ref, m1f_ref, gf_ref, kf_ref, src_ref, af_ref, ab_ref):
    h2 = h2_ref[...]
    hf = _hy_taps(h2, w3f_ref, b3f_ref, delta_ref, False)
    hb = _hy_taps(h2, w3b_ref, b3b_ref, delta_ref, True)
    L, c = hf.shape
    n_total = 2.0 * L
    inv = 1.0 / ((jnp.sum(jnp.abs(hf), axis=0, keepdims=True)
                  + jnp.sum(jnp.abs(hb), axis=0, keepdims=True)) * n_total)
    nh, n2 = src_ref.shape[0], src_ref.shape[1]
    src_ref[...] = (hf * inv).reshape(nh, n2, c)
    _fft_block_fwd(src_ref, af_ref, m1f_ref)
    src_ref[...] = (hb * inv).reshape(nh, n2, c)
    _fft_block_fwd(src_ref, ab_ref, m1f_ref)

    def body(k1, carry):
        sl = pl.ds(pl.multiple_of(2 * k1, 2), 2)
        g = gf_ref[k1]
        xf = jnp.dot(g, af_ref[sl].reshape(2 * n2, c).astype(BF16), preferred_element_type=F32)
        xb = jnp.dot(g, ab_ref[sl].reshape(2 * n2, c).astype(BF16), preferred_element_type=F32)
        kr = xf[:n2] + xb[:n2]
        ki = xf[n2:] - xb[n2:]
        kf_ref[0, sl] = jnp.concatenate([kr, ki], axis=0).reshape(2, n2, c).astype(BF16)
        return carry

    lax.fori_loop(0, gf_ref.shape[0], body, 0, unroll=FFT_UNROLL)


def _hy_filter(h2, w3, b3, delta, m1f, gf):
    C = delta.shape[1]
    cb = LANES
    ncb = C // cb
    nh, n2 = FFT_N1 // 2, FFT_N2
    na = 2 * gf.shape[0]
    full = lambda a: pl.BlockSpec(a.shape, lambda o, j: (0,) * a.ndim)
    return pl.pallas_call(
        _hy_filter_kernel,
        grid=(2, ncb),
        in_specs=[full(h2),
                  pl.BlockSpec((w3.shape[0], cb), lambda o, j: (0, o * 2 * ncb + j)),
                  pl.BlockSpec((1, cb), lambda o, j: (0, o * 2 * ncb + j)),
                  pl.BlockSpec((w3.shape[0], cb), lambda o, j: (0, o * 2 * ncb + ncb + j)),
                  pl.BlockSpec((1, cb), lambda o, j: (0, o * 2 * ncb + ncb + j)),
                  pl.BlockSpec((1, cb), lambda o, j: (0, j)),
                  _const_spec(m1f.shape), _const_spec(gf.shape)],
        out_specs=pl.BlockSpec((1, na, n2, cb), lambda o, j: (o, 0, 0, j)),
        out_shape=jax.ShapeDtypeStruct((2, na, n2, C), BF16),
        scratch_shapes=[pltpu.VMEM((nh, n2, cb), F32), pltpu.VMEM((na, n2, cb), F32),
                        pltpu.VMEM((na, n2, cb), F32)],
        compiler_params=_cparams("arbitrary", "arbitrary"),
        name="hy_filter",
    )(h2, w3, b3, w3, b3, delta, m1f, gf)


def _hy_conv_kernel(conv_z, norm_out, z_ref, g_ref, cwz_ref, cbz_ref, cwg_ref, cbg_ref, skip_ref,
                    kf_ref, m1f_ref, gf_ref, gb_ref, m1i_ref, o_ref, zs_ref, ys_ref, a_ref):
    nh, n2, c = zs_ref.shape
    z = z_ref[0]
    if conv_z:
        z = _conv3(z, cwz_ref[...], cbz_ref[...])
    zs_ref[...] = z.reshape(nh, n2, c)
    _fft_block_fwd(zs_ref, a_ref, m1f_ref)

    def body(k1, carry):
        sl = pl.ds(pl.multiple_of(2 * k1, 2), 2)
        x = jnp.dot(gf_ref[k1], a_ref[sl].reshape(2 * n2, c).astype(BF16), preferred_element_type=F32)
        kf = kf_ref[0, sl].astype(F32)
        xr, xi, kr, ki = x[:n2], x[n2:], kf[0], kf[1]
        y = jnp.concatenate([xr * kr - xi * ki, xr * ki + xi * kr], axis=0).astype(BF16)
        a_ref[sl] = jnp.dot(gb_ref[k1], y, preferred_element_type=F32).reshape(2, n2, c)
        return carry

    lax.fori_loop(0, gf_ref.shape[0], body, 0, unroll=FFT_UNROLL)
    _fft_block_inv(a_ref, ys_ref, m1i_ref)
    y = ys_ref[...].reshape(nh * n2, c)
    z = zs_ref[...].reshape(nh * n2, c)
    gate = _conv3(g_ref[0], cwg_ref[...], cbg_ref[...])
    out = gate * (y + skip_ref[...] * z)
    if norm_out:
        out = _group_rms_2x64(out)
    o_ref[0] = out.astype(o_ref.dtype)


def _hy_conv(order, z, p_hy, conv_w, conv_b, skip, kf, tables):
    m1f, gf, gb, m1i = tables
    B, L, _ = p_hy.shape
    C = skip.shape[1]
    cb = LANES
    ncb = C // cb
    nh, n2 = FFT_N1 // 2, FFT_N2
    na = 2 * gf.shape[0]
    gcol = (order + 1) * ncb
    cw = lambda base: pl.BlockSpec((3, cb), lambda j, b: (0, base + j))
    cbs = lambda base: pl.BlockSpec((1, cb), lambda j, b: (0, base + j))
    return pl.pallas_call(
        functools.partial(_hy_conv_kernel, order == 0, order == 1),
        grid=(ncb, B),
        in_specs=[pl.BlockSpec((1, L, cb), lambda j, b: (b, 0, j)),
                  pl.BlockSpec((1, L, cb), lambda j, b: (b, 0, gcol + j)),
                  cw(0), cbs(0), cw(gcol), cbs(gcol),
                  pl.BlockSpec((1, cb), lambda j, b: (0, j)),
                  pl.BlockSpec((1, na, n2, cb), lambda j, b: (order, 0, 0, j)),
                  _const_spec(m1f.shape), _const_spec(gf.shape), _const_spec(gb.shape),
                  _const_spec(m1i.shape)],
        out_specs=pl.BlockSpec((1, L, cb), lambda j, b: (b, 0, j)),
        out_shape=jax.ShapeDtypeStruct((B, L, C), F32 if order == 0 else BF16),
        scratch_shapes=[pltpu.VMEM((nh, n2, cb), F32), pltpu.VMEM((nh, n2, cb), F32),
                        pltpu.VMEM((na, n2, cb), F32)],
        compiler_params=_cparams("arbitrary", "arbitrary"),
        name=f"hy_conv{order}",
    )(z, p_hy, conv_w, conv_b, conv_w, conv_b, skip[order:order + 1], kf, m1f, gf, gb, m1i)


def _hy_ctx_kernel(p_ref, h2_ref, w3_ref, b3_ref, delta_ref,
                   cw_ref, cb_ref, skip_ref, fd_ref, fi_ref, o_ref):
    lc = p_ref.shape[1]
    C = skip_ref.shape[1]
    h2 = h2_ref[...]
    u = _conv3(p_ref[0], cw_ref[...], cb_ref[...])
    z = u[:, 0:C]
    fd = fd_ref[...]
    fi = fi_ref[...]
    nf = fd.shape[0] // 2
    dot = lambda a, b: jnp.dot(a.astype(BF16), b.astype(BF16), preferred_element_type=F32)
    for o in range(2):
        base = 2 * o * C
        hf = _hy_taps(h2, w3_ref.at[:, base:base + C], b3_ref.at[:, base:base + C], delta_ref, False)
        hb = _hy_taps(h2, w3_ref.at[:, base + C:base + 2 * C], b3_ref.at[:, base + C:base + 2 * C],
                      delta_ref, True)
        inv = 1.0 / ((jnp.sum(jnp.abs(hf), axis=0, keepdims=True)
                      + jnp.sum(jnp.abs(hb), axis=0, keepdims=True)) * (2.0 * lc))
        xf = dot(fd, hf * inv)
        xb = dot(fd, hb * inv)
        kr = xf[:nf] + xb[:nf]
        ki = xf[nf:] - xb[nf:]
        x = dot(fd, z)
        xr, xi = x[:nf], x[nf:]
        y = dot(fi, jnp.concatenate([xr * kr - xi * ki, xr * ki + xi * kr], axis=0))
        z = u[:, (o + 1) * C:(o + 2) * C] * (y + skip_ref[o:o + 1, :] * z)
    for j in range(C // LANES):
        o_ref[0, :, j * LANES:(j + 1) * LANES] = _group_rms_2x64(z[:, j * LANES:(j + 1) * LANES]).astype(BF16)


def _dense_dft_tables(lc):
    n = 2 * lc
    k = np.arange(n)[:, None]
    t = np.arange(lc)[None, :]
    th = 2 * np.pi * ((k * t) % n) / n
    fd = np.concatenate([np.cos(th), -np.sin(th)], axis=0)
    fi = np.concatenate([np.cos(th).T, -np.sin(th).T], axis=1)
    return jnp.asarray(fd.astype(np.float32)), jnp.asarray(fi.astype(np.float32))


def _hy_ctx(p_hy, h2, w3, b3, delta, conv_w, conv_b, skip, fd, fi):
    B, lc, w = p_hy.shape
    C = skip.shape[1]
    full = lambda a: pl.BlockSpec(a.shape, lambda b: (0,) * a.ndim)
    args = (h2, w3, b3, delta, conv_w, conv_b, skip, fd, fi)
    return pl.pallas_call(
        _hy_ctx_kernel,
        grid=(B,),
        in_specs=[pl.BlockSpec((1, lc, w), lambda b: (b, 0, 0))] + [full(a) for a in args],
        out_specs=pl.BlockSpec((1, lc, C), lambda b: (b, 0, 0)),
        out_shape=jax.ShapeDtypeStruct((B, lc, C), BF16),
        compiler_params=_cparams("parallel"),
        name="hy_ctx",
    )(p_hy, *args)


def _attn_kernel(seg_lens, lam_init, q_ref, lam_ref, *refs):
    nseg = len(seg_lens)
    k_refs, v_refs = refs[:nseg], refs[nseg:2 * nseg]
    o_ref = refs[2 * nseg]
    n_s, n_p = 3, 2
    s_refs = refs[2 * nseg + 1:2 * nseg + 1 + n_s]
    p_flat = refs[2 * nseg + 1 + n_s:2 * nseg + 1 + n_s + 2 * n_p]
    p_refs = [p_flat[2 * i:2 * i + 2] for i in range(n_p)]
    tq = q_ref.shape[1]
    th = s_refs[0].shape[0]
    dl = lam_ref[...]
    lam = (jnp.exp(jnp.sum(dl[0:1] * dl[1:2], keepdims=True))
           - jnp.exp(jnp.sum(dl[2:3] * dl[3:4], keepdims=True))) + lam_init

    chunks = []
    off = 0
    for s, n in enumerate(seg_lens):
        for st in range(0, n, TK):
            w = min(TK, n - st)
            chunks.append((s, st, off, w))
            off += w

    nh = tq // th

    def lane_tiles(a):
        return [a[:, t * LANES:(t + 1) * LANES] for t in range(a.shape[1] // LANES)]

    def scores(h, m):
        q = q_ref[0, h * th:(h + 1) * th, :]
        lane = lax.broadcasted_iota(jnp.int32, q.shape, 1)
        qm = jnp.where((lane < HEAD_DIM) == (m == 0), q, jnp.zeros_like(q))
        s_ref = s_refs[(2 * h + m) % n_s]
        mxw = jnp.full((th, LANES), -jnp.inf, F32)
        for (s, st, o, w) in chunks:
            sc = lax.dot_general(qm, k_refs[s][0, st:st + w, :], (((1,), (1,)), ((), ())),
                                 preferred_element_type=F32)
            s_ref[:, o:o + w] = sc
            for t in lane_tiles(sc):
                mxw = jnp.maximum(mxw, t)
        return jnp.max(mxw, axis=-1, keepdims=True)

    def probs(h, m, mx):
        s_ref = s_refs[(2 * h + m) % n_s]
        p_ref = p_refs[h % n_p][m]
        lw = jnp.zeros((th, LANES), F32)
        for (s, st, o, w) in chunks:
            p = jnp.exp2(s_ref[:, o:o + w] - mx)
            for t in lane_tiles(p):
                lw = lw + t
            p_ref[:, o:o + w] = p.astype(BF16)
        return 1.0 / jnp.sum(lw, axis=-1, keepdims=True)

    def output(h, inv_l0, inv_l1):
        p0_ref, p1_ref = p_refs[h % n_p]
        r = jnp.broadcast_to(lam * inv_l1 / inv_l0, (th, LANES)).astype(BF16)
        acc = jnp.zeros((th, v_refs[0].shape[2]), F32)
        for (s, st, o, w) in chunks:
            rb = jnp.concatenate([r] * (w // LANES), axis=1)
            wgt = p0_ref[:, o:o + w] - p1_ref[:, o:o + w] * rb
            acc = acc + jnp.dot(wgt, v_refs[s][0, st:st + w, :], preferred_element_type=F32)
        o_ref[0, h * th:(h + 1) * th, :] = (_rms(acc * inv_l0) * (1.0 - lam_init)).astype(BF16)

    mx = {(0, 0): scores(0, 0), (0, 1): scores(0, 1)}
    for h in range(nh):
        il0 = probs(h, 0, mx[h, 0])
        if h + 1 < nh:
            mx[h + 1, 0] = scores(h + 1, 0)
        il1 = probs(h, 1, mx[h, 1])
        if h + 1 < nh:
            mx[h + 1, 1] = scores(h + 1, 1)
        output(h, il0, il1)


def _diff_attn(q, ks, vs, da_lambda, lam_init):
    B, lq, w = q.shape
    H = w // LANES
    tq = min(TQ, lq)
    th = min(TQ_SUB, tq // 2)
    seg_lens = tuple(k.shape[1] for k in ks)
    seg = lambda n: pl.BlockSpec((1, n, LANES), lambda b, h, i: (b, 0, h))
    return pl.pallas_call(
        functools.partial(_attn_kernel, seg_lens, lam_init),
        grid=(B, H, lq // tq),
        in_specs=[pl.BlockSpec((1, tq, LANES), lambda b, h, i: (b, i, h)),
                  pl.BlockSpec(da_lambda.shape, lambda b, h, i: (0, 0))]
                 + [seg(n) for n in seg_lens] + [seg(n) for n in seg_lens],
        out_specs=pl.BlockSpec((1, tq, LANES), lambda b, h, i: (b, i, h)),
        out_shape=jax.ShapeDtypeStruct((B, lq, w), BF16),
        scratch_shapes=[pltpu.VMEM((th, sum(seg_lens)), F32)] * 3
                       + [pltpu.VMEM((th, sum(seg_lens)), BF16)] * 4,
        compiler_params=_cparams("parallel", "parallel", "arbitrary"),
        name="diff_attn",
    )(q, da_lambda, *ks, *vs)


def _split3(x):
    hi = x.astype(BF16)
    r1 = x - hi.astype(F32)
    mid = r1.astype(BF16)
    return hi, mid, (r1 - mid.astype(F32)).astype(BF16)


def _mlstm_cumsums(gc, gr, tri_lo, tri_up):
    dot = lambda a, b: jnp.dot(a, b, preferred_element_type=F32)
    lf_c = _log_sigmoid(gc)
    lf_r = _log_sigmoid(gr)
    T = gc.shape[0]
    pre_c = sum(dot(tri_lo, t) for t in _split3(lf_c))
    pre_r = sum(dot(t, tri_up) for t in _split3(lf_r))
    suf_c = pre_c[T - 1:T, :] - pre_c + lf_c
    suf_r = pre_r[:, T - 1:T] - pre_r + lf_r
    fwd_lane = lax.broadcasted_iota(jnp.int32, gc.shape, 1) < 4
    fwd_row = lax.broadcasted_iota(jnp.int32, gr.shape, 0) < 4
    return jnp.where(fwd_lane, pre_c, suf_c), jnp.where(fwd_row, pre_r, suf_r)


def _mlstm_chunk(rev, q, k, vt, gc, gr, csum_c, csum_r, c_ref, n_ref, m_ref):
    T = q.shape[0]
    d = 6 if rev else 2
    ii = 4 if rev else 0
    lo = lax.broadcasted_iota(jnp.int32, (T, LANES), 1) < HEAD_DIM
    lo_row = lax.broadcasted_iota(jnp.int32, (1, LANES), 1) < HEAD_DIM
    sub = lax.broadcasted_iota(jnp.int32, (LANES, T), 0) < HEAD_DIM
    key_i = lax.broadcasted_iota(jnp.int32, (T, T), 0)
    qry_i = lax.broadcasted_iota(jnp.int32, (T, T), 1)
    mask = (key_i >= qry_i) if rev else (key_i <= qry_i)
    last = 0 if rev else T - 1
    qb = q.astype(BF16)
    kb = k.astype(BF16)
    vtb = vt.astype(BF16)
    zero = jnp.zeros_like(kb)
    nt = (((1,), (1,)), ((), ()))
    brow, igrow, mloc, numt, dsum = [], [], [], [], []
    for j in range(2):
        brow.append(csum_r[d + j:d + j + 1, :])
        igrow.append(gr[ii + j:ii + j + 1, :])
        ccol = gc[:, ii + j:ii + j + 1] - csum_c[:, d + j:d + j + 1]
        logd = jnp.where(mask, brow[j] + ccol, -jnp.inf)
        mloc.append(jnp.max(logd, axis=0, keepdims=True))
        kj = jnp.where(lo, kb, zero) if j == 0 else jnp.where(lo, zero, kb)
        st = lax.dot_general(kj, qb, nt, preferred_element_type=F32) * jnp.exp(logd - mloc[j])
        numt.append(jnp.dot(vtb, st.astype(BF16), preferred_element_type=F32))
        dsum.append(jnp.sum(st, axis=0, keepdims=True))
    num_loc = jnp.where(sub, numt[0], numt[1])

    n = n_ref[...]
    row8 = lax.broadcasted_iota(jnp.int32, (SUBLANES, LANES), 0)
    lo8 = lax.broadcasted_iota(jnp.int32, (SUBLANES, LANES), 1) < HEAD_DIM
    n8 = jnp.where(row8 == jnp.where(lo8, 0, 1), jnp.broadcast_to(n, (SUBLANES, LANES)), 0.0)
    qn = lax.dot_general(n8.astype(BF16), qb, nt, preferred_element_type=F32)
    hden, wrow, decay, interw, aloc, m_new = [], [], [], [], [], []
    for j in range(2):
        m_prev = m_ref[j:j + 1, 0:1]
        mrow = jnp.maximum(brow[j] + m_prev, mloc[j])
        a = jnp.exp(mloc[j] - mrow)
        iw = jnp.exp(brow[j] + m_prev - mrow)
        dn = a * dsum[j] + iw * qn[j:j + 1, :]
        hden.append(jnp.maximum(jnp.abs(dn), jnp.exp(-mrow)))
        aloc.append(a)
        interw.append(iw)
        mn = mrow[:, last:last + 1]
        btot = brow[j][:, last:last + 1]
        wrow.append(jnp.exp(btot - brow[j] + igrow[j] - mn))
        decay.append(jnp.exp(btot + m_prev - mn))
        m_new.append(mn)
    c = c_ref[...]
    intert = lax.dot_general(c.astype(BF16), qb, nt, preferred_element_type=F32)
    num = jnp.where(sub, aloc[0], aloc[1]) * num_loc + jnp.where(sub, interw[0], interw[1]) * intert
    ht = num / jnp.where(sub, hden[0], hden[1])
    wvt = (vt * jnp.where(sub, wrow[0], wrow[1])).astype(BF16)
    upd = jnp.dot(wvt, kb, preferred_element_type=F32)
    rr = lax.broadcasted_iota(jnp.int32, (LANES, LANES), 0) < HEAD_DIM
    cc = lax.broadcasted_iota(jnp.int32, (LANES, LANES), 1) < HEAD_DIM
    dcol = jnp.where(lax.broadcasted_iota(jnp.int32, (LANES, 1), 0) < HEAD_DIM, decay[0], decay[1])
    c_ref[...] = dcol * c + jnp.where(rr == cc, upd, 0.0)
    rowt = lax.broadcasted_iota(jnp.int32, (SUBLANES, T), 0)
    w8 = jnp.where(rowt == 0, wrow[0], jnp.where(rowt == 1, wrow[1], 0.0))
    r8 = jnp.dot(w8.astype(BF16), kb, preferred_element_type=F32)
    n_ref[...] = jnp.where(lo_row, decay[0] * n + r8[0:1, :], decay[1] * n + r8[1:2, :])
    m_ref[0:1, :] = jnp.broadcast_to(m_new[0], (1, LANES))
    m_ref[1:2, :] = jnp.broadcast_to(m_new[1], (1, LANES))
    return ht


def _mlstm_kernel(qc_ref, kc_ref, vc_ref, oc_ref, gcc_ref, grc_ref,
                  qx_ref, kx_ref, vx_ref, ox_ref, gcx_ref, grx_ref,
                  cwq_ref, cbq_ref, cwk_ref, cbk_ref, gbc_ref, gbr_ref,
                  yc_ref, yx_ref,
                  qs_ref, ks_ref, vt_ref, hf_ref, hb_ref, csc_ref, csr_ref, ct_ref, n_ref, m_ref):
    T = ML_T
    lx = qx_ref.shape[1]
    nx = lx // T
    r_i = lax.broadcasted_iota(jnp.int32, (T, T), 0)
    c_i = lax.broadcasted_iota(jnp.int32, (T, T), 1)
    tri_lo = jnp.where(c_i <= r_i, 1.0, 0.0).astype(BF16)
    tri_up = jnp.where(c_i >= r_i, 1.0, 0.0).astype(BF16)
    ct_ref[...] = jnp.zeros_like(ct_ref)
    n_ref[...] = jnp.zeros_like(n_ref)
    m_ref[...] = jnp.zeros_like(m_ref)
    scale = HEAD_DIM ** -0.5
    cwq, cbq, cwk, cbk = cwq_ref[...], cbq_ref[...], cwk_ref[...], cbk_ref[...]
    gbc, gbr = gbc_ref[...], gbr_ref[...]

    qc = _silu(_conv3(qc_ref[0], cwq, cbq))
    kc = _silu(_conv3(kc_ref[0], cwk, cbk)) * scale
    vct = vc_ref[0].T
    gcc = gcc_ref[0] + gbc
    grc = grc_ref[0] + gbr
    h_ct = None
    csc, csr = _mlstm_cumsums(gcc, grc, tri_lo, tri_up)
    for rev in (False, True):
        di = int(rev)
        ht = _mlstm_chunk(rev, qc, kc, vct, gcc, grc, csc, csr,
                          ct_ref.at[di], n_ref.at[di], m_ref.at[di])
        h_ct = ht if h_ct is None else h_ct + ht
    yc_ref[0] = _group_rms_2x64(h_ct.T * _sigmoid(oc_ref[0])).astype(BF16)

    qs_ref[...] = _silu(_conv3(qx_ref[0], cwq, cbq))
    ks_ref[...] = _silu(_conv3(kx_ref[0], cwk, cbk)) * scale
    for c in range(nx):
        rows = slice(c * T, (c + 1) * T)
        vt_ref[:, rows] = vx_ref[0, rows, :].T
        csc_ref[rows, :], csr_ref[:, rows] = _mlstm_cumsums(gcx_ref[0, rows, :] + gbc,
                                                            grx_ref[0, :, rows] + gbr, tri_lo, tri_up)

    def body(c, carry):
        for rev in (False, True):
            di = int(rev)
            cidx = (nx - 1 - c) if rev else c
            r0 = pl.multiple_of(cidx * T, T)
            rows = pl.ds(r0, T)
            ht = _mlstm_chunk(rev, qs_ref[rows, :], ks_ref[rows, :], vt_ref[:, rows],
                              gcx_ref[0, rows, :] + gbc, grx_ref[0, :, rows] + gbr,
                              csc_ref[rows, :], csr_ref[:, rows],
                              ct_ref.at[di], n_ref.at[di], m_ref.at[di])
            if rev:
                hb_ref[:, rows] = ht
            else:
                hf_ref[:, rows] = ht
        return carry

    lax.fori_loop(0, nx, body, 0, unroll=4)
    for c in range(nx):
        rows = slice(c * T, (c + 1) * T)
        h = (hf_ref[:, rows] + hb_ref[:, rows]).T
        yx_ref[0, rows, :] = _group_rms_2x64(h * _sigmoid(ox_ref[0, rows, :])).astype(BF16)


def _mlstm(seg_c, seg_x, conv_w, conv_b, gate_bc, gate_br):
    B, lx, _ = seg_x[1].shape
    lc = seg_c[1].shape[1]
    assert lc == ML_T and lx % ML_T == 0

    def seg_specs(L):
        col = lambda off: pl.BlockSpec((1, L, LANES), lambda b, p, off=off: (b, 0, off + p))
        return [col(0), col(2), col(0), col(0), col(0),
                pl.BlockSpec((1, SUBLANES, L), lambda b, p: (b, p, 0))]

    def seg_args(s):
        qk, v, o, g, gt = s
        return [qk, qk, v, o, g, gt]

    wspec = lambda rows, off: pl.BlockSpec((rows, LANES), lambda b, p, off=off: (0, off + p))
    return pl.pallas_call(
        _mlstm_kernel,
        grid=(B, 2),
        in_specs=seg_specs(lc) + seg_specs(lx)
                 + [wspec(3, 0), wspec(1, 0), wspec(3, 2), wspec(1, 2), wspec(1, 0),
                    pl.BlockSpec((SUBLANES, 1), lambda b, p: (p, 0))],
        out_specs=[pl.BlockSpec((1, lc, LANES), lambda b, p: (b, 0, p)),
                   pl.BlockSpec((1, lx, LANES), lambda b, p: (b, 0, p))],
        out_shape=[jax.ShapeDtypeStruct((B, lc, 2 * LANES), BF16),
                   jax.ShapeDtypeStruct((B, lx, 2 * LANES), BF16)],
        scratch_shapes=[pltpu.VMEM((lx, LANES), F32), pltpu.VMEM((lx, LANES), F32),
                        pltpu.VMEM((LANES, lx), F32), pltpu.VMEM((LANES, lx), F32),
                        pltpu.VMEM((LANES, lx), F32),
                        pltpu.VMEM((lx, LANES), F32), pltpu.VMEM((SUBLANES, lx), F32),
                        pltpu.VMEM((2, LANES, LANES), F32), pltpu.VMEM((2, 1, LANES), F32),
                        pltpu.VMEM((2, 2, LANES), F32)],
        compiler_params=_cparams("parallel", "arbitrary"),
        name="mlstm",
    )(*seg_args(seg_c), *seg_args(seg_x), conv_w, conv_b, conv_w, conv_b, gate_bc, gate_br)


def _outproj_kernel(x_ref, g_ref, hy_ref, da_ref, ml_ref, gain_ref, w_ref, o_ref):
    c_hy = hy_ref.shape[2]
    c_da = da_ref.shape[2]
    gain = gain_ref[...]
    acc = jnp.dot((hy_ref[0] * gain[:, 0:c_hy]).astype(BF16), w_ref[0:c_hy, :], preferred_element_type=F32)
    acc += jnp.dot((da_ref[0] * gain[:, c_hy:c_hy + c_da]).astype(BF16), w_ref[c_hy:c_hy + c_da, :],
                   preferred_element_type=F32)
    acc += jnp.dot((ml_ref[0] * gain[:, c_hy + c_da:]).astype(BF16), w_ref[c_hy + c_da:, :],
                   preferred_element_type=F32)
    o_ref[0] = x_ref[0] + g_ref[0] * acc


def _outproj(x, gate, y_hy, y_da, y_ml, gain, w_out):
    B, L, D = x.shape
    tm = min(TM, L)
    tok = lambda w: pl.BlockSpec((1, tm, w), lambda b, i: (b, i, 0))
    return pl.pallas_call(
        _outproj_kernel,
        grid=(B, L // tm),
        in_specs=[tok(D), pl.BlockSpec((1, 1, D), lambda b, i: (b, 0, 0)),
                  tok(y_hy.shape[2]), tok(y_da.shape[2]), tok(y_ml.shape[2]),
                  pl.BlockSpec(gain.shape, lambda b, i: (0, 0)), _const_spec(w_out.shape)],
        out_specs=tok(D),
        out_shape=jax.ShapeDtypeStruct((B, L, D), F32),
        compiler_params=_cparams("parallel", "parallel"),
        name="outproj",
    )(x, gate, y_hy, y_da, y_ml, gain, w_out)


def _ffn_kernel(final, x_ref, prev_ref, next_ref, sh_ref, sc_ref, g_ref, up_ref, cw_ref, cb_ref,
                down_ref, fw_ref, o_ref, act_ref):
    i = pl.program_id(1)
    last = pl.num_programs(1) - 1
    tm = x_ref.shape[1]
    d_ff = down_ref.shape[0]
    x = x_ref[0]
    mod = lambda a: _rms(a) * (1.0 + sc_ref[0]) + sh_ref[0]
    hp = jnp.where(i == 0, 0.0, mod(prev_ref[0]))
    hn = jnp.where(i == last, 0.0, mod(next_ref[0]))
    h = jnp.concatenate([hp, mod(x), hn], axis=0).astype(BF16)
    ext = tm + 2 * SUBLANES

    def conv_cols(lo, hi):
        u = jnp.dot(h, up_ref[:, lo:hi], preferred_element_type=F32)
        w = cw_ref[:, lo:hi]
        c = (pltpu.roll(u, 1, axis=0) * w[0:1] + u * w[1:2] + pltpu.roll(u, ext - 1, axis=0) * w[2:3]
             + cb_ref[:, lo:hi])
        return c[SUBLANES:SUBLANES + tm]

    for j in range(d_ff // FFN_TN):
        lo = j * FFN_TN
        a = conv_cols(lo, lo + FFN_TN)
        g = conv_cols(d_ff + lo, d_ff + lo + FFN_TN)
        act_ref[:, lo:lo + FFN_TN] = (_silu(g) * a).astype(BF16)
    y = x + g_ref[0] * jnp.dot(act_ref[...], down_ref[...], preferred_element_type=F32)
    if final:
        y = _rms(y) * fw_ref[...]
    o_ref[0] = y


def _ffn(x, shift, scale, gate, up, conv_w, conv_b, down, final_w, final):
    B, L, D = x.shape
    tm = min(TM, L)
    nb = tm // SUBLANES
    nrow = L // SUBLANES
    tok = pl.BlockSpec((1, tm, D), lambda b, i: (b, i, 0))
    mod = pl.BlockSpec((1, 1, D), lambda b, i: (b, 0, 0))
    full = lambda a: pl.BlockSpec(a.shape, lambda b, i: (0,) * a.ndim)
    return pl.pallas_call(
        functools.partial(_ffn_kernel, final),
        grid=(B, L // tm),
        in_specs=[tok,
                  pl.BlockSpec((1, SUBLANES, D), lambda b, i: (b, jnp.maximum(i * nb - 1, 0), 0)),
                  pl.BlockSpec((1, SUBLANES, D), lambda b, i: (b, jnp.minimum((i + 1) * nb, nrow - 1), 0)),
                  mod, mod, mod, _const_spec(up.shape), full(conv_w), full(conv_b),
                  _const_spec(down.shape), full(final_w)],
        out_specs=tok,
        out_shape=jax.ShapeDtypeStruct((B, L, D), F32),
        scratch_shapes=[pltpu.VMEM((tm, down.shape[0]), BF16)],
        compiler_params=_cparams("parallel", "parallel"),
        name="ffn_final" if final else "ffn",
    )(x, x, x, shift, scale, gate, up, conv_w, conv_b, down, final_w)


def _rope_tables(L):
    rows_n = L // GRID_W
    rows = jnp.repeat(jnp.arange(rows_n, dtype=F32), GRID_W)
    cols = jnp.tile(jnp.arange(GRID_W, dtype=F32), rows_n)
    nf = HEAD_DIM // 4
    inv = ROPE_BASE ** (-jnp.arange(nf, dtype=F32) / nf)
    cr, sr = jnp.cos(rows[:, None] * inv), jnp.sin(rows[:, None] * inv)
    cc, sc = jnp.cos(cols[:, None] * inv), jnp.sin(cols[:, None] * inv)
    cos64 = jnp.concatenate([cr, cr, cc, cc], axis=-1)
    sin64 = jnp.concatenate([-sr, sr, -sc, sc], axis=-1)
    return jnp.tile(cos64, (1, 2)), jnp.tile(sin64, (1, 2))


def _hy_features(L):
    t = jnp.linspace(0.0, 1.0, L, dtype=F32)
    pos = jnp.arange(L, dtype=F32)
    f = jnp.linspace(1e-4, HY_BANDS - 1, HY_BANDS, dtype=F32)
    ang = (2.0 * math.pi / L) * pos[:, None] * f
    z = jnp.concatenate([t[:, None], jnp.cos(ang), jnp.sin(ang)], axis=-1)
    return jnp.pad(z, ((0, 0), (0, HY_POS_PAD - z.shape[1])))


def _gate_layout(a):
    g = a.reshape(a.shape[:-1] + (4, 2, 2))
    g = jnp.moveaxis(g, -2, -3).reshape(a.shape[:-1] + (2, 8))
    g = jnp.pad(g, [(0, 0)] * (g.ndim - 1) + [(0, LANES - 8)])
    return g.reshape(a.shape[:-1] + (2 * LANES,))


def _gates_t(g):
    return jnp.swapaxes(jnp.concatenate([g[..., 0:8], g[..., LANES:LANES + 8]], axis=-1), 1, 2)


def kernel(x, c, ctx, c_ctx, ada_w, ada_b, w_in, w_out, hy_conv_w, hy_conv_b, hy_w1, hy_b1, hy_w2, hy_b2,
           hy_w3, hy_b3, hy_skip, da_lambda, ml_conv_w, ml_conv_b, ml_gate_b, mix_norm_w, ffn_up,
           ffn_conv_w, ffn_conv_b, ffn_down, final_norm_w):
    B, L, D = x.shape
    lc = ctx.shape[1]
    depth = ada_w.shape[0]
    hy_w = hy_skip.shape[2]
    n_hy = 3 * hy_w
    da_w = 2 * hy_w
    ml_w = hy_w

    rows = ((B + 1 + SUBLANES - 1) // SUBLANES) * SUBLANES
    cc = jnp.zeros((rows, D), F32).at[:B].set(c).at[B].set(c_ctx)
    mods = _adaln(cc, ada_w, ada_b)

    cos_x, sin_x = _rope_tables(L)
    cos_c, sin_c = jnp.ones((lc, LANES), F32), jnp.zeros((lc, LANES), F32)
    tables = _dft_tables(L)
    fd_c, fi_c = _dense_dft_tables(lc)
    zf_x, zf_c = _hy_features(L), _hy_features(lc)
    delta = jnp.abs(jnp.linspace(math.log(HY_DECAY_TARGET) / HY_FAST_DECAY,
                                 math.log(HY_DECAY_TARGET) / HY_SLOW_DECAY, hy_w, dtype=F32))[None, :]
    swap = np.arange(da_w) ^ (HEAD_DIM // 4)
    fw = final_norm_w[None, :]

    for l in range(depth):
        lam_init = 0.8 - 0.6 * math.exp(-0.3 * l)
        update_ctx = l < depth - 1
        mx = [m[:, None, :] for m in jnp.split(mods[l, :B], 6, axis=-1)]
        mc = [jnp.broadcast_to(m[None], (B, 1, D)) for m in jnp.split(mods[l, B:B + 1], 6, axis=-1)]

        w = w_in[l]
        o = n_hy
        wq, wk, wv = w[:, o:o + da_w], w[:, o + da_w:o + 2 * da_w], w[:, o + 2 * da_w:o + 3 * da_w]
        o += 3 * da_w
        wmqk, wmv, wmo = w[:, o:o + 2 * ml_w], w[:, o + 2 * ml_w:o + 3 * ml_w], w[:, o + 3 * ml_w:o + 4 * ml_w]
        wg = _gate_layout(w[:, o + 4 * ml_w:])
        w_ext = jnp.concatenate([w[:, :n_hy], wq, wq[:, swap], wk, wk[:, swap], wv, wmqk, wmv, wmo, wg],
                                axis=1).astype(BF16)
        gate_b = _gate_layout(ml_gate_b[l])
        gate_bc = gate_b[None, :]
        gate_br = _gates_t(gate_b[None, None, :])[0]

        px = _inproj(x, mx[0], mx[1], w_ext, cos_x, sin_x)
        pc = _inproj(ctx, mc[0], mc[1], w_ext, cos_c, sin_c)
        hy_x, q_x, k_x, v_x, mqk_x, mv_x, mo_x, g_x = px
        hy_c, q_c, k_c, v_c, mqk_c, mv_c, mo_c, g_c = pc

        w1p = jnp.pad(hy_w1[l], ((0, HY_POS_PAD - hy_w1.shape[1]), (0, 0)))
        b1, b2, b3 = hy_b1[l][None, :], hy_b2[l][None, :], hy_b3[l][None, :]
        cbias = hy_conv_b[l][None, :]
        kf = _hy_filter(_hy_hidden(zf_x, w1p, b1, hy_w2[l], b2), hy_w3[l], b3, delta, tables[0], tables[1])
        z1 = _hy_conv(0, hy_x, hy_x, hy_conv_w[l], cbias, hy_skip[l], kf, tables)
        y_hy_x = _hy_conv(1, z1, hy_x, hy_conv_w[l], cbias, hy_skip[l], kf, tables)

        y_da_x = _diff_attn(q_x, [k_c, k_x], [v_c, v_x], da_lambda[l], lam_init)

        seg_c = (mqk_c, mv_c, mo_c, g_c, _gates_t(g_c))
        seg_x = (mqk_x, mv_x, mo_x, g_x, _gates_t(g_x))
        y_ml_c, y_ml_x = _mlstm(seg_c, seg_x, ml_conv_w[l], ml_conv_b[l][None, :], gate_bc, gate_br)

        gain = mix_norm_w[l][None, :]
        wo = w_out[l].astype(BF16)
        up = ffn_up[l].astype(BF16)
        down = ffn_down[l].astype(BF16)
        fcb = ffn_conv_b[l][None, :]
        x = _outproj(x, mx[2], y_hy_x, y_da_x, y_ml_x, gain, wo)
        x = _ffn(x, mx[3], mx[4], mx[5], up, ffn_conv_w[l], fcb, down, fw, final=not update_ctx)

        if update_ctx:
            y_hy_c = _hy_ctx(hy_c, _hy_hidden(zf_c, w1p, b1, hy_w2[l], b2), hy_w3[l], b3, delta,
                             hy_conv_w[l], cbias, hy_skip[l], fd_c, fi_c)
            y_da_c = _diff_attn(q_c, [k_c], [v_c], da_lambda[l], lam_init)
            ctx = _outproj(ctx, mc[2], y_hy_c, y_da_c, y_ml_c, gain, wo)
            ctx = _ffn(ctx, mc[3], mc[4], mc[5], up, ffn_conv_w[l], fcb, down, fw, final=False)
    return x
```

```python
import functools
import math

import numpy as np
import jax
import jax.numpy as jnp
from jax import lax
from jax.experimental import pallas as pl
from jax.experimental.pallas import tpu as pltpu

F32 = jnp.float32
BF16 = jnp.bfloat16
HI = lax.Precision.HIGHEST

EPS = 1e-6
HEAD_DIM = 64
GRID_W = 64
ROPE_BASE = 10000.0
HY_BANDS = 16
HY_POS_PAD = 64
HY_DECAY_TARGET = 1e-2
HY_FAST_DECAY = 0.3
HY_SLOW_DECAY = 1.5
LANES = 128
SUBLANES = 8
VMEM_LIMIT = 56 * 1024 * 1024

FFT_N1 = 64
FFT_N2 = 128
FFT_UNROLL = 8
ML_T = 256
TQ = 1024
TQ_SUB = 256
TK = 512
TM = 1024
FFN_TN = 256


def _cparams(*sem):
    return pltpu.CompilerParams(dimension_semantics=sem, vmem_limit_bytes=VMEM_LIMIT)


def _const_spec(shape):
    n = len(shape)
    return pl.BlockSpec(shape, lambda *_: (0,) * n, pipeline_mode=pl.Buffered(1))


def _silu(x):
    return x / (1.0 + jnp.exp(-x))


def _sigmoid(x):
    return 1.0 / (1.0 + jnp.exp(-x))


def _log_sigmoid(x):
    return jnp.minimum(x, 0.0) - jnp.log(1.0 + jnp.exp(-jnp.abs(x)))


def _rms(x):
    return x * lax.rsqrt(jnp.mean(x * x, axis=-1, keepdims=True) + EPS)


def _group_rms_2x64(y):
    lo = lax.broadcasted_iota(jnp.int32, y.shape, 1) < 64
    y2 = y * y
    s_lo = jnp.sum(jnp.where(lo, y2, 0.0), axis=-1, keepdims=True)
    s_hi = jnp.sum(jnp.where(lo, 0.0, y2), axis=-1, keepdims=True)
    r = jnp.where(lo, lax.rsqrt(s_lo * (1.0 / 64) + EPS), lax.rsqrt(s_hi * (1.0 / 64) + EPS))
    return y * r


def _conv3(x, w, b):
    L = x.shape[0]
    row = lax.broadcasted_iota(jnp.int32, x.shape, 0)
    xm = jnp.where(row == 0, 0.0, pltpu.roll(x, 1, axis=0))
    xp = jnp.where(row == L - 1, 0.0, pltpu.roll(x, L - 1, axis=0))
    return xm * w[0:1] + x * w[1:2] + xp * w[2:3] + b


def _adaln_kernel(c_ref, w_ref, b_ref, o_ref):
    a = _silu(c_ref[...])
    o_ref[0] = jnp.dot(a, w_ref[0], precision=HI, preferred_element_type=F32) + b_ref[0]


def _adaln(cc, ada_w, ada_b):
    depth, d, n6 = ada_w.shape
    tn = 1536
    return pl.pallas_call(
        _adaln_kernel,
        grid=(depth, n6 // tn),
        in_specs=[pl.BlockSpec(cc.shape, lambda l, j: (0, 0)),
                  pl.BlockSpec((1, d, tn), lambda l, j: (l, 0, j)),
                  pl.BlockSpec((1, 1, tn), lambda l, j: (l, 0, j))],
        out_specs=pl.BlockSpec((1, cc.shape[0], tn), lambda l, j: (l, 0, j)),
        out_shape=jax.ShapeDtypeStruct((depth, cc.shape[0], n6), F32),
        compiler_params=_cparams("parallel", "parallel"),
        name="adaln",
    )(cc, ada_w, ada_b.reshape(depth, 1, n6))


_C_HY, _C_Q, _C_K, _C_V, _C_MQK, _C_MV, _C_MO, _C_G, _C_END = (
    0, 768, 1280, 1792, 2304, 2816, 3072, 3328, 3584)


def _inproj_kernel(x_ref, sh_ref, sc_ref, w_ref, cos_ref, sin_ref, perm_ref,
                   hy_ref, q_ref, k_ref, v_ref, mqk_ref, mv_ref, mo_ref, g_ref):
    h = (_rms(x_ref[0]) * (1.0 + sc_ref[0]) + sh_ref[0]).astype(BF16)

    def proj(lo, hi):
        return jnp.dot(h, w_ref[:, lo:hi], preferred_element_type=F32)

    hy_ref[0] = proj(_C_HY, _C_Q)
    cos = cos_ref[...]
    sin = sin_ref[...]
    perm = perm_ref[...]

    def rope(c0, c1, out_ref, scale):
        a = proj(c0, c1)
        for j in range((c1 - c0) // LANES):
            sl = slice(LANES * j, LANES * (j + 1))
            asw = jnp.dot(a[:, sl].astype(BF16), perm, preferred_element_type=F32)
            out_ref[0, :, sl] = ((a[:, sl] * cos + asw * sin) * scale).astype(BF16)

    rope(_C_Q, _C_K, q_ref, (HEAD_DIM ** -0.5) * math.log2(math.e))
    rope(_C_K, _C_V, k_ref, 1.0)
    v_ref[0] = proj(_C_V, _C_MQK).astype(BF16)
    mqk_ref[0] = proj(_C_MQK, _C_MV)
    mv_ref[0] = proj(_C_MV, _C_MO)
    mo_ref[0] = proj(_C_MO, _C_G)
    g_ref[0] = proj(_C_G, _C_END)


def _inproj(x, shift, scale, w_ext, cos, sin):
    B, L, D = x.shape
    tm = min(TM, L)
    idx = np.arange(LANES)
    perm = jnp.asarray((idx[:, None] == (idx[None, :] ^ (HEAD_DIM // 4))).astype(np.float32)).astype(BF16)
    tok = lambda w: pl.BlockSpec((1, tm, w), lambda b, i: (b, i, 0))
    mod = pl.BlockSpec((1, 1, D), lambda b, i: (b, 0, 0))
    tab = pl.BlockSpec((tm, LANES), lambda b, i: (i, 0))
    widths = (768, 512, 512, 512, 512, 256, 256, 256)
    dtypes = (F32, BF16, BF16, BF16, F32, F32, F32, F32)
    return pl.pallas_call(
        _inproj_kernel,
        grid=(B, L // tm),
        in_specs=[tok(D), mod, mod, _const_spec(w_ext.shape), tab, tab,
                  pl.BlockSpec(perm.shape, lambda b, i: (0, 0))],
        out_specs=[tok(w) for w in widths],
        out_shape=[jax.ShapeDtypeStruct((B, L, w), dt) for w, dt in zip(widths, dtypes)],
        compiler_params=_cparams("parallel", "parallel"),
        name="inproj",
    )(x, shift, scale, w_ext, cos, sin, perm)


def _dft_tables(L):
    n1, n2, g = FFT_N1, FFT_N2, SUBLANES
    N = n1 * n2
    assert N == 2 * L
    nk = n1 // 2 + 1
    k1 = np.arange(nk)[:, None]
    m1 = np.arange(n1 // 2)[None, :]
    th = 2 * np.pi * ((k1 * m1) % n1) / n1
    f = np.stack([np.cos(th), -np.sin(th)], axis=1)
    fold = np.where((k1 == 0) | (k1 == n1 // 2), 1.0, 2.0)[:, :, None]
    eye = np.eye(g)
    m1f = np.einsum('krn,gh->krgnh', f, eye).reshape(2 * nk * g, (n1 // 2) * g)
    m1i = np.einsum('krn,gh->ngkrh', f * fold, eye).reshape((n1 // 2) * g, 2 * nk * g)
    a = np.arange(n2)
    th2 = 2 * np.pi * ((a[:, None] * a[None, :]) % n2) / n2
    fr, fi = np.cos(th2), -np.sin(th2)
    tw = 2 * np.pi * ((np.arange(nk)[:, None] * a[None, :]) % N) / N
    tr, ti = np.cos(tw)[:, None, :], -np.sin(tw)[:, None, :]
    gr = fr[None] * tr - fi[None] * ti
    gi = fr[None] * ti + fi[None] * tr
    gf = np.concatenate([np.concatenate([gr, -gi], axis=2),
                         np.concatenate([gi, gr], axis=2)], axis=1)
    gb = np.transpose(gf, (0, 2, 1))
    cvt = lambda m: jnp.asarray(m.astype(np.float32)).astype(BF16)
    return cvt(m1f), cvt(gf), cvt(gb), cvt(m1i)


def _fft_block_fwd(src3_ref, a_ref, m1f_ref):
    nh, n2, c = src3_ref.shape
    na = a_ref.shape[0]
    m1f = m1f_ref[...]
    group = lambda g: slice(SUBLANES * g, SUBLANES * (g + 1))
    for g in range(0, n2 // SUBLANES, 2):
        xg = jnp.concatenate([src3_ref[:, group(g + i), :].reshape(nh * SUBLANES, c) for i in range(2)],
                             axis=1).astype(BF16)
        out = jnp.dot(m1f, xg, preferred_element_type=F32)
        for i in range(2):
            a_ref[:, group(g + i), :] = out[:, c * i:c * (i + 1)].reshape(na, SUBLANES, c)


def _fft_block_inv(a_ref, dst3_ref, m1i_ref):
    nh, n2, c = dst3_ref.shape
    na = a_ref.shape[0]
    m1i = m1i_ref[...]
    group = lambda g: slice(SUBLANES * g, SUBLANES * (g + 1))
    for g in range(0, n2 // SUBLANES, 2):
        ag = jnp.concatenate([a_ref[:, group(g + i), :].reshape(na * SUBLANES, c) for i in range(2)],
                             axis=1).astype(BF16)
        out = jnp.dot(m1i, ag, preferred_element_type=F32)
        for i in range(2):
            dst3_ref[:, group(g + i), :] = out[:, c * i:c * (i + 1)].reshape(nh, SUBLANES, c)


def _hy_mlp(z_ref, w1_ref, b1_ref, w2_ref, b2_ref):
    h = jnp.sin(jnp.dot(z_ref[...], w1_ref[...], precision=HI, preferred_element_type=F32) + b1_ref[...])
    return jnp.sin(jnp.dot(h, w2_ref[...], precision=HI, preferred_element_type=F32) + b2_ref[...])


def _hy_taps(h2, w3_ref, b3_ref, delta_ref, zero_first):
    L = h2.shape[0]
    h = jnp.dot(h2, w3_ref[...], precision=HI, preferred_element_type=F32) + b3_ref[...]
    row = lax.broadcasted_iota(jnp.int32, h.shape, 0)
    t = row.astype(F32) * (1.0 / (L - 1))
    h = h * jnp.exp(-t * delta_ref[...])
    if zero_first:
        h = jnp.where(row == 0, 0.0, h)
    return h


def _hy_mlp_kernel(z_ref, w1_ref, b1_ref, w2_ref, b2_ref, o_ref):
    o_ref[...] = _hy_mlp(z_ref, w1_ref, b1_ref, w2_ref, b2_ref)


def _hy_hidden(zfeat, w1, b1, w2, b2):
    args = (zfeat, w1, b1, w2, b2)
    return pl.pallas_call(
        _hy_mlp_kernel,
        grid=(1,),
        in_specs=[pl.BlockSpec(a.shape, lambda i: (0, 0)) for a in args],
        out_specs=pl.BlockSpec((zfeat.shape[0], w2.shape[1]), lambda i: (0, 0)),
        out_shape=jax.ShapeDtypeStruct((zfeat.shape[0], w2.shape[1]), F32),
        compiler_params=_cparams("arbitrary"),
        name="hy_mlp",
    )(*args)


def _hy_filter_kernel(h2_ref, w3f_ref, b3f_ref, w3b_ref, b3b_ref,
                      delta_ref, m1f_ref, gf_ref, kf_ref, src_ref, af_ref, ab_ref):
    h2 = h2_ref[...]
    hf = _hy_taps(h2, w3f_ref, b3f_ref, delta_ref, False)
    hb = _hy_taps(h2, w3b_ref, b3b_ref, delta_ref, True)
    L, c = hf.shape
    n_total = 2.0 * L
    inv = 1.0 / ((jnp.sum(jnp.abs(hf), axis=0, keepdims=True)
                  + jnp.sum(jnp.abs(hb), axis=0, keepdims=True)) * n_total)
    nh, n2 = src_ref.shape[0], src_ref.shape[1]
    src_ref[...] = (hf * inv).reshape(nh, n2, c)
    _fft_block_fwd(src_ref, af_ref, m1f_ref)
    src_ref[...] = (hb * inv).reshape(nh, n2, c)
    _fft_block_fwd(src_ref, ab_ref, m1f_ref)

    def body(k1, carry):
        sl = pl.ds(pl.multiple_of(2 * k1, 2), 2)
        g = gf_ref[k1]
        xf = jnp.dot(g, af_ref[sl].reshape(2 * n2, c).astype(BF16), preferred_element_type=F32)
        xb = jnp.dot(g, ab_ref[sl].reshape(2 * n2, c).astype(BF16), preferred_element_type=F32)
        kr = xf[:n2] + xb[:n2]
        ki = xf[n2:] - xb[n2:]
        kf_ref[0, sl] = jnp.concatenate([kr, ki], axis=0).reshape(2, n2, c).astype(BF16)
        return carry

    lax.fori_loop(0, gf_ref.shape[0], body, 0, unroll=FFT_UNROLL)


def _hy_filter(h2, w3, b3, delta, m1f, gf):
    C = delta.shape[1]
    cb = LANES
    ncb = C // cb
    nh, n2 = FFT_N1 // 2, FFT_N2
    na = 2 * gf.shape[0]
    full = lambda a: pl.BlockSpec(a.shape, lambda o, j: (0,) * a.ndim)
    return pl.pallas_call(
        _hy_filter_kernel,
        grid=(2, ncb),
        in_specs=[full(h2),
                  pl.BlockSpec((w3.shape[0], cb), lambda o, j: (0, o * 2 * ncb + j)),
                  pl.BlockSpec((1, cb), lambda o, j: (0, o * 2 * ncb + j)),
                  pl.BlockSpec((w3.shape[0], cb), lambda o, j: (0, o * 2 * ncb + ncb + j)),
                  pl.BlockSpec((1, cb), lambda o, j: (0, o * 2 * ncb + ncb + j)),
                  pl.BlockSpec((1, cb), lambda o, j: (0, j)),
                  _const_spec(m1f.shape), _const_spec(gf.shape)],
        out_specs=pl.BlockSpec((1, na, n2, cb), lambda o, j: (o, 0, 0, j)),
        out_shape=jax.ShapeDtypeStruct((2, na, n2, C), BF16),
        scratch_shapes=[pltpu.VMEM((nh, n2, cb), F32), pltpu.VMEM((na, n2, cb), F32),
                        pltpu.VMEM((na, n2, cb), F32)],
        compiler_params=_cparams("arbitrary", "arbitrary"),
        name="hy_filter",
    )(h2, w3, b3, w3, b3, delta, m1f, gf)


def _hy_conv_kernel(conv_z, norm_out, z_ref, g_ref, cwz_ref, cbz_ref, cwg_ref, cbg_ref, skip_ref,
                    kf_ref, m1f_ref, gf_ref, gb_ref, m1i_ref, o_ref, zs_ref, ys_ref, a_ref):
    nh, n2, c = zs_ref.shape
    z = z_ref[0]
    if conv_z:
        z = _conv3(z, cwz_ref[...], cbz_ref[...])
    zs_ref[...] = z.reshape(nh, n2, c)
    _fft_block_fwd(zs_ref, a_ref, m1f_ref)

    def body(k1, carry):
        sl = pl.ds(pl.multiple_of(2 * k1, 2), 2)
        x = jnp.dot(gf_ref[k1], a_ref[sl].reshape(2 * n2, c).astype(BF16), preferred_element_type=F32)
        kf = kf_ref[0, sl].astype(F32)
        xr, xi, kr, ki = x[:n2], x[n2:], kf[0], kf[1]
        y = jnp.concatenate([xr * kr - xi * ki, xr * ki + xi * kr], axis=0).astype(BF16)
        a_ref[sl] = jnp.dot(gb_ref[k1], y, preferred_element_type=F32).reshape(2, n2, c)
        return carry

    lax.fori_loop(0, gf_ref.shape[0], body, 0, unroll=FFT_UNROLL)
    _fft_block_inv(a_ref, ys_ref, m1i_ref)
    y = ys_ref[...].reshape(nh * n2, c)
    z = zs_ref[...].reshape(nh * n2, c)
    gate = _conv3(g_ref[0], cwg_ref[...], cbg_ref[...])
    out = gate * (y + skip_ref[...] * z)
    if norm_out:
        out = _group_rms_2x64(out)
    o_ref[0] = out.astype(o_ref.dtype)


def _hy_conv(order, z, p_hy, conv_w, conv_b, skip, kf, tables):
    m1f, gf, gb, m1i = tables
    B, L, _ = p_hy.shape
    C = skip.shape[1]
    cb = LANES
    ncb = C // cb
    nh, n2 = FFT_N1 // 2, FFT_N2
    na = 2 * gf.shape[0]
    gcol = (order + 1) * ncb
    cw = lambda base: pl.BlockSpec((3, cb), lambda j, b: (0, base + j))
    cbs = lambda base: pl.BlockSpec((1, cb), lambda j, b: (0, base + j))
    return pl.pallas_call(
        functools.partial(_hy_conv_kernel, order == 0, order == 1),
        grid=(ncb, B),
        in_specs=[pl.BlockSpec((1, L, cb), lambda j, b: (b, 0, j)),
                  pl.BlockSpec((1, L, cb), lambda j, b: (b, 0, gcol + j)),
                  cw(0), cbs(0), cw(gcol), cbs(gcol),
                  pl.BlockSpec((1, cb), lambda j, b: (0, j)),
                  pl.BlockSpec((1, na, n2, cb), lambda j, b: (order, 0, 0, j)),
                  _const_spec(m1f.shape), _const_spec(gf.shape), _const_spec(gb.shape),
                  _const_spec(m1i.shape)],
        out_specs=pl.BlockSpec((1, L, cb), lambda j, b: (b, 0, j)),
        out_shape=jax.ShapeDtypeStruct((B, L, C), F32 if order == 0 else BF16),
        scratch_shapes=[pltpu.VMEM((nh, n2, cb), F32), pltpu.VMEM((nh, n2, cb), F32),
                        pltpu.VMEM((na, n2, cb), F32)],
        compiler_params=_cparams("arbitrary", "arbitrary"),
        name=f"hy_conv{order}",
    )(z, p_hy, conv_w, conv_b, conv_w, conv_b, skip[order:order + 1], kf, m1f, gf, gb, m1i)


def _hy_ctx_kernel(p_ref, h2_ref, w3_ref, b3_ref, delta_ref,
                   cw_ref, cb_ref, skip_ref, fd_ref, fi_ref, o_ref):
    lc = p_ref.shape[1]
    C = skip_ref.shape[1]
    h2 = h2_ref[...]
    u = _conv3(p_ref[0], cw_ref[...], cb_ref[...])
    z = u[:, 0:C]
    fd = fd_ref[...]
    fi = fi_ref[...]
    nf = fd.shape[0] // 2
    dot = lambda a, b: jnp.dot(a.astype(BF16), b.astype(BF16), preferred_element_type=F32)
    for o in range(2):
        base = 2 * o * C
        hf = _hy_taps(h2, w3_ref.at[:, base:base + C], b3_ref.at[:, base:base + C], delta_ref, False)
        hb = _hy_taps(h2, w3_ref.at[:, base + C:base + 2 * C], b3_ref.at[:, base + C:base + 2 * C],
                      delta_ref, True)
        inv = 1.0 / ((jnp.sum(jnp.abs(hf), axis=0, keepdims=True)
                      + jnp.sum(jnp.abs(hb), axis=0, keepdims=True)) * (2.0 * lc))
        xf = dot(fd, hf * inv)
        xb = dot(fd, hb * inv)
        kr = xf[:nf] + xb[:nf]
        ki = xf[nf:] - xb[nf:]
        x = dot(fd, z)
        xr, xi = x[:nf], x[nf:]
        y = dot(fi, jnp.concatenate([xr * kr - xi * ki, xr * ki + xi * kr], axis=0))
        z = u[:, (o + 1) * C:(o + 2) * C] * (y + skip_ref[o:o + 1, :] * z)
    for j in range(C // LANES):
        o_ref[0, :, j * LANES:(j + 1) * LANES] = _group_rms_2x64(z[:, j * LANES:(j + 1) * LANES]).astype(BF16)


def _dense_dft_tables(lc):
    n = 2 * lc
    k = np.arange(n)[:, None]
    t = np.arange(lc)[None, :]
    th = 2 * np.pi * ((k * t) % n) / n
    fd = np.concatenate([np.cos(th), -np.sin(th)], axis=0)
    fi = np.concatenate([np.cos(th).T, -np.sin(th).T], axis=1)
    return jnp.asarray(fd.astype(np.float32)), jnp.asarray(fi.astype(np.float32))


def _hy_ctx(p_hy, h2, w3, b3, delta, conv_w, conv_b, skip, fd, fi):
    B, lc, w = p_hy.shape
    C = skip.shape[1]
    full = lambda a: pl.BlockSpec(a.shape, lambda b: (0,) * a.ndim)
    args = (h2, w3, b3, delta, conv_w, conv_b, skip, fd, fi)
    return pl.pallas_call(
        _hy_ctx_kernel,
        grid=(B,),
        in_specs=[pl.BlockSpec((1, lc, w), lambda b: (b, 0, 0))] + [full(a) for a in args],
        out_specs=pl.BlockSpec((1, lc, C), lambda b: (b, 0, 0)),
        out_shape=jax.ShapeDtypeStruct((B, lc, C), BF16),
        compiler_params=_cparams("parallel"),
        name="hy_ctx",
    )(p_hy, *args)


def _attn_kernel(seg_lens, lam_init, q_ref, lam_ref, *refs):
    nseg = len(seg_lens)
    k_refs, v_refs = refs[:nseg], refs[nseg:2 * nseg]
    o_ref = refs[2 * nseg]
    n_s, n_p = 3, 2
    s_refs = refs[2 * nseg + 1:2 * nseg + 1 + n_s]
    p_flat = refs[2 * nseg + 1 + n_s:2 * nseg + 1 + n_s + 2 * n_p]
    p_refs = [p_flat[2 * i:2 * i + 2] for i in range(n_p)]
    tq = q_ref.shape[1]
    th = s_refs[0].shape[0]
    dl = lam_ref[...]
    lam = (jnp.exp(jnp.sum(dl[0:1] * dl[1:2], keepdims=True))
           - jnp.exp(jnp.sum(dl[2:3] * dl[3:4], keepdims=True))) + lam_init

    chunks = []
    off = 0
    for s, n in enumerate(seg_lens):
        for st in range(0, n, TK):
            w = min(TK, n - st)
            chunks.append((s, st, off, w))
            off += w

    nh = tq // th

    def lane_tiles(a):
        return [a[:, t * LANES:(t + 1) * LANES] for t in range(a.shape[1] // LANES)]

    def scores(h, m):
        q = q_ref[0, h * th:(h + 1) * th, :]
        lane = lax.broadcasted_iota(jnp.int32, q.shape, 1)
        qm = jnp.where((lane < HEAD_DIM) == (m == 0), q, jnp.zeros_like(q))
        s_ref = s_refs[(2 * h + m) % n_s]
        mxw = jnp.full((th, LANES), -jnp.inf, F32)
        for (s, st, o, w) in chunks:
            sc = lax.dot_general(qm, k_refs[s][0, st:st + w, :], (((1,), (1,)), ((), ())),
                                 preferred_element_type=F32)
            s_ref[:, o:o + w] = sc
            for t in lane_tiles(sc):
                mxw = jnp.maximum(mxw, t)
        return jnp.max(mxw, axis=-1, keepdims=True)

    def probs(h, m, mx):
        s_ref = s_refs[(2 * h + m) % n_s]
        p_ref = p_refs[h % n_p][m]
        lw = jnp.zeros((th, LANES), F32)
        for (s, st, o, w) in chunks:
            p = jnp.exp2(s_ref[:, o:o + w] - mx)
            for t in lane_tiles(p):
                lw = lw + t
            p_ref[:, o:o + w] = p.astype(BF16)
        return 1.0 / jnp.sum(lw, axis=-1, keepdims=True)

    def output(h, inv_l0, inv_l1):
        p0_ref, p1_ref = p_refs[h % n_p]
        r = jnp.broadcast_to(lam * inv_l1 / inv_l0, (th, LANES)).astype(BF16)
        acc = jnp.zeros((th, v_refs[0].shape[2]), F32)
        for (s, st, o, w) in chunks:
            rb = jnp.concatenate([r] * (w // LANES), axis=1)
            wgt = p0_ref[:, o:o + w] - p1_ref[:, o:o + w] * rb
            acc = acc + jnp.dot(wgt, v_refs[s][0, st:st + w, :], preferred_element_type=F32)
        o_ref[0, h * th:(h + 1) * th, :] = (_rms(acc * inv_l0) * (1.0 - lam_init)).astype(BF16)

    mx = {(0, 0): scores(0, 0), (0, 1): scores(0, 1)}
    for h in range(nh):
        il0 = probs(h, 0, mx[h, 0])
        if h + 1 < nh:
            mx[h + 1, 0] = scores(h + 1, 0)
        il1 = probs(h, 1, mx[h, 1])
        if h + 1 < nh:
            mx[h + 1, 1] = scores(h + 1, 1)
        output(h, il0, il1)


def _diff_attn(q, ks, vs, da_lambda, lam_init):
    B, lq, w = q.shape
    H = w // LANES
    tq = min(TQ, lq)
    th = min(TQ_SUB, tq // 2)
    seg_lens = tuple(k.shape[1] for k in ks)
    seg = lambda n: pl.BlockSpec((1, n, LANES), lambda b, h, i: (b, 0, h))
    return pl.pallas_call(
        functools.partial(_attn_kernel, seg_lens, lam_init),
        grid=(B, H, lq // tq),
        in_specs=[pl.BlockSpec((1, tq, LANES), lambda b, h, i: (b, i, h)),
                  pl.BlockSpec(da_lambda.shape, lambda b, h, i: (0, 0))]
                 + [seg(n) for n in seg_lens] + [seg(n) for n in seg_lens],
        out_specs=pl.BlockSpec((1, tq, LANES), lambda b, h, i: (b, i, h)),
        out_shape=jax.ShapeDtypeStruct((B, lq, w), BF16),
        scratch_shapes=[pltpu.VMEM((th, sum(seg_lens)), F32)] * 3
                       + [pltpu.VMEM((th, sum(seg_lens)), BF16)] * 4,
        compiler_params=_cparams("parallel", "parallel", "arbitrary"),
        name="diff_attn",
    )(q, da_lambda, *ks, *vs)


def _split3(x):
    hi = x.astype(BF16)
    r1 = x - hi.astype(F32)
    mid = r1.astype(BF16)
    return hi, mid, (r1 - mid.astype(F32)).astype(BF16)


def _mlstm_cumsums(gc, gr, tri_lo, tri_up):
    dot = lambda a, b: jnp.dot(a, b, preferred_element_type=F32)
    lf_c = _log_sigmoid(gc)
    lf_r = _log_sigmoid(gr)
    T = gc.shape[0]
    pre_c = sum(dot(tri_lo, t) for t in _split3(lf_c))
    pre_r = sum(dot(t, tri_up) for t in _split3(lf_r))
    suf_c = pre_c[T - 1:T, :] - pre_c + lf_c
    suf_r = pre_r[:, T - 1:T] - pre_r + lf_r
    fwd_lane = lax.broadcasted_iota(jnp.int32, gc.shape, 1) < 4
    fwd_row = lax.broadcasted_iota(jnp.int32, gr.shape, 0) < 4
    return jnp.where(fwd_lane, pre_c, suf_c), jnp.where(fwd_row, pre_r, suf_r)


def _mlstm_chunk(rev, q, k, vt, gc, gr, csum_c, csum_r, c_ref, n_ref, m_ref):
    T = q.shape[0]
    d = 6 if rev else 2
    ii = 4 if rev else 0
    lo = lax.broadcasted_iota(jnp.int32, (T, LANES), 1) < HEAD_DIM
    lo_row = lax.broadcasted_iota(jnp.int32, (1, LANES), 1) < HEAD_DIM
    sub = lax.broadcasted_iota(jnp.int32, (LANES, T), 0) < HEAD_DIM
    key_i = lax.broadcasted_iota(jnp.int32, (T, T), 0)
    qry_i = lax.broadcasted_iota(jnp.int32, (T, T), 1)
    mask = (key_i >= qry_i) if rev else (key_i <= qry_i)
    last = 0 if rev else T - 1
    qb = q.astype(BF16)
    kb = k.astype(BF16)
    vtb = vt.astype(BF16)
    zero = jnp.zeros_like(kb)
    nt = (((1,), (1,)), ((), ()))
    brow, igrow, mloc, numt, dsum = [], [], [], [], []
    for j in range(2):
        brow.append(csum_r[d + j:d + j + 1, :])
        igrow.append(gr[ii + j:ii + j + 1, :])
        ccol = gc[:, ii + j:ii + j + 1] - csum_c[:, d + j:d + j + 1]
        logd = jnp.where(mask, brow[j] + ccol, -jnp.inf)
        mloc.append(jnp.max(logd, axis=0, keepdims=True))
        kj = jnp.where(lo, kb, zero) if j == 0 else jnp.where(lo, zero, kb)
        st = lax.dot_general(kj, qb, nt, preferred_element_type=F32) * jnp.exp(logd - mloc[j])
        numt.append(jnp.dot(vtb, st.astype(BF16), preferred_element_type=F32))
        dsum.append(jnp.sum(st, axis=0, keepdims=True))
    num_loc = jnp.where(sub, numt[0], numt[1])

    n = n_ref[...]
    row8 = lax.broadcasted_iota(jnp.int32, (SUBLANES, LANES), 0)
    lo8 = lax.broadcasted_iota(jnp.int32, (SUBLANES, LANES), 1) < HEAD_DIM
    n8 = jnp.where(row8 == jnp.where(lo8, 0, 1), jnp.broadcast_to(n, (SUBLANES, LANES)), 0.0)
    qn = lax.dot_general(n8.astype(BF16), qb, nt, preferred_element_type=F32)
    hden, wrow, decay, interw, aloc, m_new = [], [], [], [], [], []
    for j in range(2):
        m_prev = m_ref[j:j + 1, 0:1]
        mrow = jnp.maximum(brow[j] + m_prev, mloc[j])
        a = jnp.exp(mloc[j] - mrow)
        iw = jnp.exp(brow[j] + m_prev - mrow)
        dn = a * dsum[j] + iw * qn[j:j + 1, :]
        hden.append(jnp.maximum(jnp.abs(dn), jnp.exp(-mrow)))
        aloc.append(a)
        interw.append(iw)
        mn = mrow[:, last:last + 1]
        btot = brow[j][:, last:last + 1]
        wrow.append(jnp.exp(btot - brow[j] + igrow[j] - mn))
        decay.append(jnp.exp(btot + m_prev - mn))
        m_new.append(mn)
    c = c_ref[...]
    intert = lax.dot_general(c.astype(BF16), qb, nt, preferred_element_type=F32)
    num = jnp.where(sub, aloc[0], aloc[1]) * num_loc + jnp.where(sub, interw[0], interw[1]) * intert
    ht = num / jnp.where(sub, hden[0], hden[1])
    wvt = (vt * jnp.where(sub, wrow[0], wrow[1])).astype(BF16)
    upd = jnp.dot(wvt, kb, preferred_element_type=F32)
    rr = lax.broadcasted_iota(jnp.int32, (LANES, LANES), 0) < HEAD_DIM
    cc = lax.broadcasted_iota(jnp.int32, (LANES, LANES), 1) < HEAD_DIM
    dcol = jnp.where(lax.broadcasted_iota(jnp.int32, (LANES, 1), 0) < HEAD_DIM, decay[0], decay[1])
    c_ref[...] = dcol * c + jnp.where(rr == cc, upd, 0.0)
    rowt = lax.broadcasted_iota(jnp.int32, (SUBLANES, T), 0)
    w8 = jnp.where(rowt == 0, wrow[0], jnp.where(rowt == 1, wrow[1], 0.0))
    r8 = jnp.dot(w8.astype(BF16), kb, preferred_element_type=F32)
    n_ref[...] = jnp.where(lo_row, decay[0] * n + r8[0:1, :], decay[1] * n + r8[1:2, :])
    m_ref[0:1, :] = jnp.broadcast_to(m_new[0], (1, LANES))
    m_ref[1:2, :] = jnp.broadcast_to(m_new[1], (1, LANES))
    return ht


def _mlstm_kernel(qc_ref, kc_ref, vc_ref, oc_ref, gcc_ref, grc_ref,
                  qx_ref, kx_ref, vx_ref, ox_ref, gcx_ref, grx_ref,
                  cwq_ref, cbq_ref, cwk_ref, cbk_ref, gbc_ref, gbr_ref,
                  yc_ref, yx_ref,
                  qs_ref, ks_ref, vt_ref, hf_ref, hb_ref, csc_ref, csr_ref, ct_ref, n_ref, m_ref):
    T = ML_T
    lx = qx_ref.shape[1]
    nx = lx // T
    r_i = lax.broadcasted_iota(jnp.int32, (T, T), 0)
    c_i = lax.broadcasted_iota(jnp.int32, (T, T), 1)
    tri_lo = jnp.where(c_i <= r_i, 1.0, 0.0).astype(BF16)
    tri_up = jnp.where(c_i >= r_i, 1.0, 0.0).astype(BF16)
    ct_ref[...] = jnp.zeros_like(ct_ref)
    n_ref[...] = jnp.zeros_like(n_ref)
    m_ref[...] = jnp.zeros_like(m_ref)
    scale = HEAD_DIM ** -0.5
    cwq, cbq, cwk, cbk = cwq_ref[...], cbq_ref[...], cwk_ref[...], cbk_ref[...]
    gbc, gbr = gbc_ref[...], gbr_ref[...]

    qc = _silu(_conv3(qc_ref[0], cwq, cbq))
    kc = _silu(_conv3(kc_ref[0], cwk, cbk)) * scale
    vct = vc_ref[0].T
    gcc = gcc_ref[0] + gbc
    grc = grc_ref[0] + gbr
    h_ct = None
    csc, csr = _mlstm_cumsums(gcc, grc, tri_lo, tri_up)
    for rev in (False, True):
        di = int(rev)
        ht = _mlstm_chunk(rev, qc, kc, vct, gcc, grc, csc, csr,
                          ct_ref.at[di], n_ref.at[di], m_ref.at[di])
        h_ct = ht if h_ct is None else h_ct + ht
    yc_ref[0] = _group_rms_2x64(h_ct.T * _sigmoid(oc_ref[0])).astype(BF16)

    qs_ref[...] = _silu(_conv3(qx_ref[0], cwq, cbq))
    ks_ref[...] = _silu(_conv3(kx_ref[0], cwk, cbk)) * scale
    for c in range(nx):
        rows = slice(c * T, (c + 1) * T)
        vt_ref[:, rows] = vx_ref[0, rows, :].T
        csc_ref[rows, :], csr_ref[:, rows] = _mlstm_cumsums(gcx_ref[0, rows, :] + gbc,
                                                            grx_ref[0, :, rows] + gbr, tri_lo, tri_up)

    def body(c, carry):
        for rev in (False, True):
            di = int(rev)
            cidx = (nx - 1 - c) if rev else c
            r0 = pl.multiple_of(cidx * T, T)
            rows = pl.ds(r0, T)
            ht = _mlstm_chunk(rev, qs_ref[rows, :], ks_ref[rows, :], vt_ref[:, rows],
                              gcx_ref[0, rows, :] + gbc, grx_ref[0, :, rows] + gbr,
                              csc_ref[rows, :], csr_ref[:, rows],
                              ct_ref.at[di], n_ref.at[di], m_ref.at[di])
            if rev:
                hb_ref[:, rows] = ht
            else:
                hf_ref[:, rows] = ht
        return carry

    lax.fori_loop(0, nx, body, 0, unroll=4)
    for c in range(nx):
        rows = slice(c * T, (c + 1) * T)
        h = (hf_ref[:, rows] + hb_ref[:, rows]).T
        yx_ref[0, rows, :] = _group_rms_2x64(h * _sigmoid(ox_ref[0, rows, :])).astype(BF16)


def _mlstm(seg_c, seg_x, conv_w, conv_b, gate_bc, gate_br):
    B, lx, _ = seg_x[1].shape
    lc = seg_c[1].shape[1]
    assert lc == ML_T and lx % ML_T == 0

    def seg_specs(L):
        col = lambda off: pl.BlockSpec((1, L, LANES), lambda b, p, off=off: (b, 0, off + p))
        return [col(0), col(2), col(0), col(0), col(0),
                pl.BlockSpec((1, SUBLANES, L), lambda b, p: (b, p, 0))]

    def seg_args(s):
        qk, v, o, g, gt = s
        return [qk, qk, v, o, g, gt]

    wspec = lambda rows, off: pl.BlockSpec((rows, LANES), lambda b, p, off=off: (0, off + p))
    return pl.pallas_call(
        _mlstm_kernel,
        grid=(B, 2),
        in_specs=seg_specs(lc) + seg_specs(lx)
                 + [wspec(3, 0), wspec(1, 0), wspec(3, 2), wspec(1, 2), wspec(1, 0),
                    pl.BlockSpec((SUBLANES, 1), lambda b, p: (p, 0))],
        out_specs=[pl.BlockSpec((1, lc, LANES), lambda b, p: (b, 0, p)),
                   pl.BlockSpec((1, lx, LANES), lambda b, p: (b, 0, p))],
        out_shape=[jax.ShapeDtypeStruct((B, lc, 2 * LANES), BF16),
                   jax.ShapeDtypeStruct((B, lx, 2 * LANES), BF16)],
        scratch_shapes=[pltpu.VMEM((lx, LANES), F32), pltpu.VMEM((lx, LANES), F32),
                        pltpu.VMEM((LANES, lx), F32), pltpu.VMEM((LANES, lx), F32),
                        pltpu.VMEM((LANES, lx), F32),
                        pltpu.VMEM((lx, LANES), F32), pltpu.VMEM((SUBLANES, lx), F32),
                        pltpu.VMEM((2, LANES, LANES), F32), pltpu.VMEM((2, 1, LANES), F32),
                        pltpu.VMEM((2, 2, LANES), F32)],
        compiler_params=_cparams("parallel", "arbitrary"),
        name="mlstm",
    )(*seg_args(seg_c), *seg_args(seg_x), conv_w, conv_b, conv_w, conv_b, gate_bc, gate_br)


def _outproj_kernel(x_ref, g_ref, hy_ref, da_ref, ml_ref, gain_ref, w_ref, o_ref):
    c_hy = hy_ref.shape[2]
    c_da = da_ref.shape[2]
    gain = gain_ref[...]
    acc = jnp.dot((hy_ref[0] * gain[:, 0:c_hy]).astype(BF16), w_ref[0:c_hy, :], preferred_element_type=F32)
    acc += jnp.dot((da_ref[0] * gain[:, c_hy:c_hy + c_da]).astype(BF16), w_ref[c_hy:c_hy + c_da, :],
                   preferred_element_type=F32)
    acc += jnp.dot((ml_ref[0] * gain[:, c_hy + c_da:]).astype(BF16), w_ref[c_hy + c_da:, :],
                   preferred_element_type=F32)
    o_ref[0] = x_ref[0] + g_ref[0] * acc


def _outproj(x, gate, y_hy, y_da, y_ml, gain, w_out):
    B, L, D = x.shape
    tm = min(TM, L)
    tok = lambda w: pl.BlockSpec((1, tm, w), lambda b, i: (b, i, 0))
    return pl.pallas_call(
        _outproj_kernel,
        grid=(B, L // tm),
        in_specs=[tok(D), pl.BlockSpec((1, 1, D), lambda b, i: (b, 0, 0)),
                  tok(y_hy.shape[2]), tok(y_da.shape[2]), tok(y_ml.shape[2]),
                  pl.BlockSpec(gain.shape, lambda b, i: (0, 0)), _const_spec(w_out.shape)],
        out_specs=tok(D),
        out_shape=jax.ShapeDtypeStruct((B, L, D), F32),
        compiler_params=_cparams("parallel", "parallel"),
        name="outproj",
    )(x, gate, y_hy, y_da, y_ml, gain, w_out)


def _ffn_kernel(final, x_ref, prev_ref, next_ref, sh_ref, sc_ref, g_ref, up_ref, cw_ref, cb_ref,
                down_ref, fw_ref, o_ref, act_ref):
    i = pl.program_id(1)
    last = pl.num_programs(1) - 1
    tm = x_ref.shape[1]
    d_ff = down_ref.shape[0]
    x = x_ref[0]
    mod = lambda a: _rms(a) * (1.0 + sc_ref[0]) + sh_ref[0]
    hp = jnp.where(i == 0, 0.0, mod(prev_ref[0]))
    hn = jnp.where(i == last, 0.0, mod(next_ref[0]))
    h = jnp.concatenate([hp, mod(x), hn], axis=0).astype(BF16)
    ext = tm + 2 * SUBLANES

    def conv_cols(lo, hi):
        u = jnp.dot(h, up_ref[:, lo:hi], preferred_element_type=F32)
        w = cw_ref[:, lo:hi]
        c = (pltpu.roll(u, 1, axis=0) * w[0:1] + u * w[1:2] + pltpu.roll(u, ext - 1, axis=0) * w[2:3]
             + cb_ref[:, lo:hi])
        return c[SUBLANES:SUBLANES + tm]

    for j in range(d_ff // FFN_TN):
        lo = j * FFN_TN
        a = conv_cols(lo, lo + FFN_TN)
        g = conv_cols(d_ff + lo, d_ff + lo + FFN_TN)
        act_ref[:, lo:lo + FFN_TN] = (_silu(g) * a).astype(BF16)
    y = x + g_ref[0] * jnp.dot(act_ref[...], down_ref[...], preferred_element_type=F32)
    if final:
        y = _rms(y) * fw_ref[...]
    o_ref[0] = y


def _ffn(x, shift, scale, gate, up, conv_w, conv_b, down, final_w, final):
    B, L, D = x.shape
    tm = min(TM, L)
    nb = tm // SUBLANES
    nrow = L // SUBLANES
    tok = pl.BlockSpec((1, tm, D), lambda b, i: (b, i, 0))
    mod = pl.BlockSpec((1, 1, D), lambda b, i: (b, 0, 0))
    full = lambda a: pl.BlockSpec(a.shape, lambda b, i: (0,) * a.ndim)
    return pl.pallas_call(
        functools.partial(_ffn_kernel, final),
        grid=(B, L // tm),
        in_specs=[tok,
                  pl.BlockSpec((1, SUBLANES, D), lambda b, i: (b, jnp.maximum(i * nb - 1, 0), 0)),
                  pl.BlockSpec((1, SUBLANES, D), lambda b, i: (b, jnp.minimum((i + 1) * nb, nrow - 1), 0)),
                  mod, mod, mod, _const_spec(up.shape), full(conv_w), full(conv_b),
                  _const_spec(down.shape), full(final_w)],
        out_specs=tok,
        out_shape=jax.ShapeDtypeStruct((B, L, D), F32),
        scratch_shapes=[pltpu.VMEM((tm, down.shape[0]), BF16)],
        compiler_params=_cparams("parallel", "parallel"),
        name="ffn_final" if final else "ffn",
    )(x, x, x, shift, scale, gate, up, conv_w, conv_b, down, final_w)


def _rope_tables(L):
    rows_n = L // GRID_W
    rows = jnp.repeat(jnp.arange(rows_n, dtype=F32), GRID_W)
    cols = jnp.tile(jnp.arange(GRID_W, dtype=F32), rows_n)
    nf = HEAD_DIM // 4
    inv = ROPE_BASE ** (-jnp.arange(nf, dtype=F32) / nf)
    cr, sr = jnp.cos(rows[:, None] * inv), jnp.sin(rows[:, None] * inv)
    cc, sc = jnp.cos(cols[:, None] * inv), jnp.sin(cols[:, None] * inv)
    cos64 = jnp.concatenate([cr, cr, cc, cc], axis=-1)
    sin64 = jnp.concatenate([-sr, sr, -sc, sc], axis=-1)
    return jnp.tile(cos64, (1, 2)), jnp.tile(sin64, (1, 2))


def _hy_features(L):
    t = jnp.linspace(0.0, 1.0, L, dtype=F32)
    pos = jnp.arange(L, dtype=F32)
    f = jnp.linspace(1e-4, HY_BANDS - 1, HY_BANDS, dtype=F32)
    ang = (2.0 * math.pi / L) * pos[:, None] * f
    z = jnp.concatenate([t[:, None], jnp.cos(ang), jnp.sin(ang)], axis=-1)
    return jnp.pad(z, ((0, 0), (0, HY_POS_PAD - z.shape[1])))


def _gate_layout(a):
    g = a.reshape(a.shape[:-1] + (4, 2, 2))
    g = jnp.moveaxis(g, -2, -3).reshape(a.shape[:-1] + (2, 8))
    g = jnp.pad(g, [(0, 0)] * (g.ndim - 1) + [(0, LANES - 8)])
    return g.reshape(a.shape[:-1] + (2 * LANES,))


def _gates_t(g):
    return jnp.swapaxes(jnp.concatenate([g[..., 0:8], g[..., LANES:LANES + 8]], axis=-1), 1, 2)


def kernel(x, c, ctx, c_ctx, ada_w, ada_b, w_in, w_out, hy_conv_w, hy_conv_b, hy_w1, hy_b1, hy_w2, hy_b2,
           hy_w3, hy_b3, hy_skip, da_lambda, ml_conv_w, ml_conv_b, ml_gate_b, mix_norm_w, ffn_up,
           ffn_conv_w, ffn_conv_b, ffn_down, final_norm_w):
    B, L, D = x.shape
    lc = ctx.shape[1]
    depth = ada_w.shape[0]
    hy_w = hy_skip.shape[2]
    n_hy = 3 * hy_w
    da_w = 2 * hy_w
    ml_w = hy_w

    rows = ((B + 1 + SUBLANES - 1) // SUBLANES) * SUBLANES
    cc = jnp.zeros((rows, D), F32).at[:B].set(c).at[B].set(c_ctx)
    mods = _adaln(cc, ada_w, ada_b)

    cos_x, sin_x = _rope_tables(L)
    cos_c, sin_c = jnp.ones((lc, LANES), F32), jnp.zeros((lc, LANES), F32)
    tables = _dft_tables(L)
    fd_c, fi_c = _dense_dft_tables(lc)
    zf_x, zf_c = _hy_features(L), _hy_features(lc)
    delta = jnp.abs(jnp.linspace(math.log(HY_DECAY_TARGET) / HY_FAST_DECAY,
                                 math.log(HY_DECAY_TARGET) / HY_SLOW_DECAY, hy_w, dtype=F32))[None, :]
    fw = final_norm_w[None, :]

    for l in range(depth):
        lam_init = 0.8 - 0.6 * math.exp(-0.3 * l)
        update_ctx = l < depth - 1
        mx = [m[:, None, :] for m in jnp.split(mods[l, :B], 6, axis=-1)]
        mc = [jnp.broadcast_to(m[None], (B, 1, D)) for m in jnp.split(mods[l, B:B + 1], 6, axis=-1)]

        w = w_in[l]
        n_main = n_hy + 3 * da_w + 4 * ml_w
        w_ext = jnp.concatenate([w[:, :n_main], _gate_layout(w[:, n_main:])], axis=1).astype(BF16)
        gate_b = _gate_layout(ml_gate_b[l])
        gate_bc = gate_b[None, :]
        gate_br = _gates_t(gate_b[None, None, :])[0]

        px = _inproj(x, mx[0], mx[1], w_ext, cos_x, sin_x)
        pc = _inproj(ctx, mc[0], mc[1], w_ext, cos_c, sin_c)
        hy_x, q_x, k_x, v_x, mqk_x, mv_x, mo_x, g_x = px
        hy_c, q_c, k_c, v_c, mqk_c, mv_c, mo_c, g_c = pc

        w1p = jnp.pad(hy_w1[l], ((0, HY_POS_PAD - hy_w1.shape[1]), (0, 0)))
        b1, b2, b3 = hy_b1[l][None, :], hy_b2[l][None, :], hy_b3[l][None, :]
        cbias = hy_conv_b[l][None, :]
        kf = _hy_filter(_hy_hidden(zf_x, w1p, b1, hy_w2[l], b2), hy_w3[l], b3, delta, tables[0], tables[1])
        z1 = _hy_conv(0, hy_x, hy_x, hy_conv_w[l], cbias, hy_skip[l], kf, tables)
        y_hy_x = _hy_conv(1, z1, hy_x, hy_conv_w[l], cbias, hy_skip[l], kf, tables)

        y_da_x = _diff_attn(q_x, [k_c, k_x], [v_c, v_x], da_lambda[l], lam_init)

        seg_c = (mqk_c, mv_c, mo_c, g_c, _gates_t(g_c))
        seg_x = (mqk_x, mv_x, mo_x, g_x, _gates_t(g_x))
        y_ml_c, y_ml_x = _mlstm(seg_c, seg_x, ml_conv_w[l], ml_conv_b[l][None, :], gate_bc, gate_br)

        gain = mix_norm_w[l][None, :]
        wo = w_out[l].astype(BF16)
        up = ffn_up[l].astype(BF16)
        down = ffn_down[l].astype(BF16)
        fcb = ffn_conv_b[l][None, :]
        x = _outproj(x, mx[2], y_hy_x, y_da_x, y_ml_x, gain, wo)
        x = _ffn(x, mx[3], mx[4], mx[5], up, ffn_conv_w[l], fcb, down, fw, final=not update_ctx)

        if update_ctx:
            y_hy_c = _hy_ctx(hy_c, _hy_hidden(zf_c, w1p, b1, hy_w2[l], b2), hy_w3[l], b3, delta,
                             hy_conv_w[l], cbias, hy_skip[l], fd_c, fi_c)
            y_da_c = _diff_attn(q_c, [k_c], [v_c], da_lambda[l], lam_init)
            ctx = _outproj(ctx, mc[2], y_hy_c, y_da_c, y_ml_c, gain, wo)
            ctx = _ffn(ctx, mc[3], mc[4], mc[5], up, ffn_conv_w[l], fcb, down, fw, final=False)
    return x
```

```python
import functools
import math

import numpy as np
import jax
import jax.numpy as jnp
from jax import lax
from jax.experimental import pallas as pl
from jax.experimental.pallas import tpu as pltpu

F32 = jnp.float32
BF16 = jnp.bfloat16
HI = lax.Precision.HIGHEST

EPS = 1e-6
HEAD_DIM = 64
GRID_W = 64
ROPE_BASE = 10000.0
HY_BANDS = 16
HY_POS_PAD = 64
HY_DECAY_TARGET = 1e-2
HY_FAST_DECAY = 0.3
HY_SLOW_DECAY = 1.5
LANES = 128
SUBLANES = 8
VMEM_LIMIT = 56 * 1024 * 1024

FFT_N1 = 64
FFT_N2 = 128
FFT_UNROLL = 8
ML_T = 256
TQ = 2048
TQ_SUB = 512
TK = 512
TM = 1024
FFN_TN = 256


def _cparams(*sem):
    return pltpu.CompilerParams(dimension_semantics=sem, vmem_limit_bytes=VMEM_LIMIT)


def _const_spec(shape):
    n = len(shape)
    return pl.BlockSpec(shape, lambda *_: (0,) * n, pipeline_mode=pl.Buffered(1))


def _silu(x):
    return x / (1.0 + jnp.exp(-x))


def _sigmoid(x):
    return 1.0 / (1.0 + jnp.exp(-x))


def _log_sigmoid(x):
    return jnp.minimum(x, 0.0) - jnp.log(1.0 + jnp.exp(-jnp.abs(x)))


def _rms(x):
    return x * lax.rsqrt(jnp.mean(x * x, axis=-1, keepdims=True) + EPS)


def _group_rms_2x64(y):
    lo = lax.broadcasted_iota(jnp.int32, y.shape, 1) < 64
    y2 = y * y
    s_lo = jnp.sum(jnp.where(lo, y2, 0.0), axis=-1, keepdims=True)
    s_hi = jnp.sum(jnp.where(lo, 0.0, y2), axis=-1, keepdims=True)
    r = jnp.where(lo, lax.rsqrt(s_lo * (1.0 / 64) + EPS), lax.rsqrt(s_hi * (1.0 / 64) + EPS))
    return y * r


def _conv3(x, w, b):
    L = x.shape[0]
    row = lax.broadcasted_iota(jnp.int32, x.shape, 0)
    xm = jnp.where(row == 0, 0.0, pltpu.roll(x, 1, axis=0))
    xp = jnp.where(row == L - 1, 0.0, pltpu.roll(x, L - 1, axis=0))
    return xm * w[0:1] + x * w[1:2] + xp * w[2:3] + b


def _adaln_kernel(c_ref, w_ref, b_ref, o_ref):
    a = _silu(c_ref[...])
    o_ref[0] = jnp.dot(a, w_ref[0], precision=HI, preferred_element_type=F32) + b_ref[0]


def _adaln(cc, ada_w, ada_b):
    depth, d, n6 = ada_w.shape
    tn = 1536
    return pl.pallas_call(
        _adaln_kernel,
        grid=(depth, n6 // tn),
        in_specs=[pl.BlockSpec(cc.shape, lambda l, j: (0, 0)),
                  pl.BlockSpec((1, d, tn), lambda l, j: (l, 0, j)),
                  pl.BlockSpec((1, 1, tn), lambda l, j: (l, 0, j))],
        out_specs=pl.BlockSpec((1, cc.shape[0], tn), lambda l, j: (l, 0, j)),
        out_shape=jax.ShapeDtypeStruct((depth, cc.shape[0], n6), F32),
        compiler_params=_cparams("parallel", "parallel"),
        name="adaln",
    )(cc, ada_w, ada_b.reshape(depth, 1, n6))


_C_HY, _C_Q, _C_K, _C_V, _C_MQK, _C_MV, _C_MO, _C_G, _C_END = (
    0, 768, 1280, 1792, 2304, 2816, 3072, 3328, 3584)


def _inproj_kernel(x_ref, sh_ref, sc_ref, w_ref, cos_ref, sin_ref, perm_ref,
                   hy_ref, q_ref, k_ref, v_ref, mqk_ref, mv_ref, mo_ref, g_ref):
    h = (_rms(x_ref[0]) * (1.0 + sc_ref[0]) + sh_ref[0]).astype(BF16)

    def proj(lo, hi):
        return jnp.dot(h, w_ref[:, lo:hi], preferred_element_type=F32)

    hy_ref[0] = proj(_C_HY, _C_Q)
    cos = cos_ref[...]
    sin = sin_ref[...]
    perm = perm_ref[...]

    def rope(c0, c1, out_ref, scale):
        a = proj(c0, c1)
        for j in range((c1 - c0) // LANES):
            sl = slice(LANES * j, LANES * (j + 1))
            asw = jnp.dot(a[:, sl].astype(BF16), perm, preferred_element_type=F32)
            out_ref[0, :, sl] = ((a[:, sl] * cos + asw * sin) * scale).astype(BF16)

    rope(_C_Q, _C_K, q_ref, (HEAD_DIM ** -0.5) * math.log2(math.e))
    rope(_C_K, _C_V, k_ref, 1.0)
    v_ref[0] = proj(_C_V, _C_MQK).astype(BF16)
    mqk_ref[0] = proj(_C_MQK, _C_MV)
    mv_ref[0] = proj(_C_MV, _C_MO)
    mo_ref[0] = proj(_C_MO, _C_G)
    g_ref[0] = proj(_C_G, _C_END)


def _inproj(x, shift, scale, w_ext, cos, sin):
    B, L, D = x.shape
    tm = min(TM, L)
    idx = np.arange(LANES)
    perm = jnp.asarray((idx[:, None] == (idx[None, :] ^ (HEAD_DIM // 4))).astype(np.float32)).astype(BF16)
    tok = lambda w: pl.BlockSpec((1, tm, w), lambda b, i: (b, i, 0))
    mod = pl.BlockSpec((1, 1, D), lambda b, i: (b, 0, 0))
    tab = pl.BlockSpec((tm, LANES), lambda b, i: (i, 0))
    widths = (768, 512, 512, 512, 512, 256, 256, 256)
    dtypes = (F32, BF16, BF16, BF16, F32, F32, F32, F32)
    return pl.pallas_call(
        _inproj_kernel,
        grid=(B, L // tm),
        in_specs=[tok(D), mod, mod, _const_spec(w_ext.shape), tab, tab,
                  pl.BlockSpec(perm.shape, lambda b, i: (0, 0))],
        out_specs=[tok(w) for w in widths],
        out_shape=[jax.ShapeDtypeStruct((B, L, w), dt) for w, dt in zip(widths, dtypes)],
        compiler_params=_cparams("parallel", "parallel"),
        name="inproj",
    )(x, shift, scale, w_ext, cos, sin, perm)


def _dft_tables(L):
    n1, n2, g = FFT_N1, FFT_N2, SUBLANES
    N = n1 * n2
    assert N == 2 * L
    nk = n1 // 2 + 1
    k1 = np.arange(nk)[:, None]
    m1 = np.arange(n1 // 2)[None, :]
    th = 2 * np.pi * ((k1 * m1) % n1) / n1
    f = np.stack([np.cos(th), -np.sin(th)], axis=1)
    fold = np.where((k1 == 0) | (k1 == n1 // 2), 1.0, 2.0)[:, :, None]
    eye = np.eye(g)
    m1f = np.einsum('krn,gh->krgnh', f, eye).reshape(2 * nk * g, (n1 // 2) * g)
    m1i = np.einsum('krn,gh->ngkrh', f * fold, eye).reshape((n1 // 2) * g, 2 * nk * g)
    a = np.arange(n2)
    th2 = 2 * np.pi * ((a[:, None] * a[None, :]) % n2) / n2
    fr, fi = np.cos(th2), -np.sin(th2)
    tw = 2 * np.pi * ((np.arange(nk)[:, None] * a[None, :]) % N) / N
    tr, ti = np.cos(tw)[:, None, :], -np.sin(tw)[:, None, :]
    gr = fr[None] * tr - fi[None] * ti
    gi = fr[None] * ti + fi[None] * tr
    gf = np.concatenate([np.concatenate([gr, -gi], axis=2),
                         np.concatenate([gi, gr], axis=2)], axis=1)
    gb = np.transpose(gf, (0, 2, 1))
    cvt = lambda m: jnp.asarray(m.astype(np.float32)).astype(BF16)
    return cvt(m1f), cvt(gf), cvt(gb), cvt(m1i)


def _fft_block_fwd(src3_ref, a_ref, m1f_ref):
    nh, n2, c = src3_ref.shape
    na = a_ref.shape[0]
    m1f = m1f_ref[...]
    group = lambda g: slice(SUBLANES * g, SUBLANES * (g + 1))
    for g in range(0, n2 // SUBLANES, 2):
        xg = jnp.concatenate([src3_ref[:, group(g + i), :].reshape(nh * SUBLANES, c) for i in range(2)],
                             axis=1).astype(BF16)
        out = jnp.dot(m1f, xg, preferred_element_type=F32)
        for i in range(2):
            a_ref[:, group(g + i), :] = out[:, c * i:c * (i + 1)].reshape(na, SUBLANES, c)


def _fft_block_inv(a_ref, dst3_ref, m1i_ref):
    nh, n2, c = dst3_ref.shape
    na = a_ref.shape[0]
    m1i = m1i_ref[...]
    group = lambda g: slice(SUBLANES * g, SUBLANES * (g + 1))
    for g in range(0, n2 // SUBLANES, 2):
        ag = jnp.concatenate([a_ref[:, group(g + i), :].reshape(na * SUBLANES, c) for i in range(2)],
                             axis=1).astype(BF16)
        out = jnp.dot(m1i, ag, preferred_element_type=F32)
        for i in range(2):
            dst3_ref[:, group(g + i), :] = out[:, c * i:c * (i + 1)].reshape(nh, SUBLANES, c)


def _hy_mlp(z_ref, w1_ref, b1_ref, w2_ref, b2_ref):
    h = jnp.sin(jnp.dot(z_ref[...], w1_ref[...], precision=HI, preferred_element_type=F32) + b1_ref[...])
    return jnp.sin(jnp.dot(h, w2_ref[...], precision=HI, preferred_element_type=F32) + b2_ref[...])


def _hy_taps(h2, w3_ref, b3_ref, delta_ref, zero_first):
    L = h2.shape[0]
    h = jnp.dot(h2, w3_ref[...], precision=HI, preferred_element_type=F32) + b3_ref[...]
    row = lax.broadcasted_iota(jnp.int32, h.shape, 0)
    t = row.astype(F32) * (1.0 / (L - 1))
    h = h * jnp.exp(-t * delta_ref[...])
    if zero_first:
        h = jnp.where(row == 0, 0.0, h)
    return h


def _hy_mlp_kernel(z_ref, w1_ref, b1_ref, w2_ref, b2_ref, o_ref):
    o_ref[...] = _hy_mlp(z_ref, w1_ref, b1_ref, w2_ref, b2_ref)


def _hy_hidden(zfeat, w1, b1, w2, b2):
    args = (zfeat, w1, b1, w2, b2)
    return pl.pallas_call(
        _hy_mlp_kernel,
        grid=(1,),
        in_specs=[pl.BlockSpec(a.shape, lambda i: (0, 0)) for a in args],
        out_specs=pl.BlockSpec((zfeat.shape[0], w2.shape[1]), lambda i: (0, 0)),
        out_shape=jax.ShapeDtypeStruct((zfeat.shape[0], w2.shape[1]), F32),
        compiler_params=_cparams("arbitrary"),
        name="hy_mlp",
    )(*args)


def _hy_filter_kernel(h2_ref, w3f_ref, b3f_ref, w3b_ref, b3b_ref,
                      delta_ref, m1f_ref, gf_ref, kf_ref, src_ref, af_ref, ab_ref):
    h2 = h2_ref[...]
    hf = _hy_taps(h2, w3f_ref, b3f_ref, delta_ref, False)
    hb = _hy_taps(h2, w3b_ref, b3b_ref, delta_ref, True)
    L, c = hf.shape
    n_total = 2.0 * L
    inv = 1.0 / ((jnp.sum(jnp.abs(hf), axis=0, keepdims=True)
                  + jnp.sum(jnp.abs(hb), axis=0, keepdims=True)) * n_total)
    nh, n2 = src_ref.shape[0], src_ref.shape[1]
    src_ref[...] = (hf * inv).reshape(nh, n2, c)
    _fft_block_fwd(src_ref, af_ref, m1f_ref)
    src_ref[...] = (hb * inv).reshape(nh, n2, c)
    _fft_block_fwd(src_ref, ab_ref, m1f_ref)

    def body(k1, carry):
        sl = pl.ds(pl.multiple_of(2 * k1, 2), 2)
        g = gf_ref[k1]
        xf = jnp.dot(g, af_ref[sl].reshape(2 * n2, c).astype(BF16), preferred_element_type=F32)
        xb = jnp.dot(g, ab_ref[sl].reshape(2 * n2, c).astype(BF16), preferred_element_type=F32)
        kr = xf[:n2] + xb[:n2]
        ki = xf[n2:] - xb[n2:]
        kf_ref[0, sl] = jnp.concatenate([kr, ki], axis=0).reshape(2, n2, c).astype(BF16)
        return carry

    lax.fori_loop(0, gf_ref.shape[0], body, 0, unroll=FFT_UNROLL)


def _hy_filter(h2, w3, b3, delta, m1f, gf):
    C = delta.shape[1]
    cb = LANES
    ncb = C // cb
    nh, n2 = FFT_N1 // 2, FFT_N2
    na = 2 * gf.shape[0]
    full = lambda a: pl.BlockSpec(a.shape, lambda o, j: (0,) * a.ndim)
    return pl.pallas_call(
        _hy_filter_kernel,
        grid=(2, ncb),
        in_specs=[full(h2),
                  pl.BlockSpec((w3.shape[0], cb), lambda o, j: (0, o * 2 * ncb + j)),
                  pl.BlockSpec((1, cb), lambda o, j: (0, o * 2 * ncb + j)),
                  pl.BlockSpec((w3.shape[0], cb), lambda o, j: (0, o * 2 * ncb + ncb + j)),
                  pl.BlockSpec((1, cb), lambda o, j: (0, o * 2 * ncb + ncb + j)),
                  pl.BlockSpec((1, cb), lambda o, j: (0, j)),
                  _const_spec(m1f.shape), _const_spec(gf.shape)],
        out_specs=pl.BlockSpec((1, na, n2, cb), lambda o, j: (o, 0, 0, j)),
        out_shape=jax.ShapeDtypeStruct((2, na, n2, C), BF16),
        scratch_shapes=[pltpu.VMEM((nh, n2, cb), F32), pltpu.VMEM((na, n2, cb), F32),
                        pltpu.VMEM((na, n2, cb), F32)],
        compiler_params=_cparams("arbitrary", "arbitrary"),
        name="hy_filter",
    )(h2, w3, b3, w3, b3, delta, m1f, gf)


def _hy_conv_kernel(conv_z, norm_out, z_ref, g_ref, cwz_ref, cbz_ref, cwg_ref, cbg_ref, skip_ref,
                    kf_ref, m1f_ref, gf_ref, gb_ref, m1i_ref, o_ref, zs_ref, ys_ref, a_ref):
    nh, n2, c = zs_ref.shape
    z = z_ref[0]
    if conv_z:
        z = _conv3(z, cwz_ref[...], cbz_ref[...])
    zs_ref[...] = z.reshape(nh, n2, c)
    _fft_block_fwd(zs_ref, a_ref, m1f_ref)

    def body(k1, carry):
        sl = pl.ds(pl.multiple_of(2 * k1, 2), 2)
        x = jnp.dot(gf_ref[k1], a_ref[sl].reshape(2 * n2, c).astype(BF16), preferred_element_type=F32)
        kf = kf_ref[0, sl].astype(F32)
        xr, xi, kr, ki = x[:n2], x[n2:], kf[0], kf[1]
        y = jnp.concatenate([xr * kr - xi * ki, xr * ki + xi * kr], axis=0).astype(BF16)
        a_ref[sl] = jnp.dot(gb_ref[k1], y, preferred_element_type=F32).reshape(2, n2, c)
        return carry

    lax.fori_loop(0, gf_ref.shape[0], body, 0, unroll=FFT_UNROLL)
    _fft_block_inv(a_ref, ys_ref, m1i_ref)
    y = ys_ref[...].reshape(nh * n2, c)
    z = zs_ref[...].reshape(nh * n2, c)
    gate = _conv3(g_ref[0], cwg_ref[...], cbg_ref[...])
    out = gate * (y + skip_ref[...] * z)
    if norm_out:
        out = _group_rms_2x64(out)
    o_ref[0] = out.astype(o_ref.dtype)


def _hy_conv(order, z, p_hy, conv_w, conv_b, skip, kf, tables):
    m1f, gf, gb, m1i = tables
    B, L, _ = p_hy.shape
    C = skip.shape[1]
    cb = LANES
    ncb = C // cb
    nh, n2 = FFT_N1 // 2, FFT_N2
    na = 2 * gf.shape[0]
    gcol = (order + 1) * ncb
    cw = lambda base: pl.BlockSpec((3, cb), lambda j, b: (0, base + j))
    cbs = lambda base: pl.BlockSpec((1, cb), lambda j, b: (0, base + j))
    return pl.pallas_call(
        functools.partial(_hy_conv_kernel, order == 0, order == 1),
        grid=(ncb, B),
        in_specs=[pl.BlockSpec((1, L, cb), lambda j, b: (b, 0, j)),
                  pl.BlockSpec((1, L, cb), lambda j, b: (b, 0, gcol + j)),
                  cw(0), cbs(0), cw(gcol), cbs(gcol),
                  pl.BlockSpec((1, cb), lambda j, b: (0, j)),
                  pl.BlockSpec((1, na, n2, cb), lambda j, b: (order, 0, 0, j)),
                  _const_spec(m1f.shape), _const_spec(gf.shape), _const_spec(gb.shape),
                  _const_spec(m1i.shape)],
        out_specs=pl.BlockSpec((1, L, cb), lambda j, b: (b, 0, j)),
        out_shape=jax.ShapeDtypeStruct((B, L, C), F32 if order == 0 else BF16),
        scratch_shapes=[pltpu.VMEM((nh, n2, cb), F32), pltpu.VMEM((nh, n2, cb), F32),
                        pltpu.VMEM((na, n2, cb), F32)],
        compiler_params=_cparams("arbitrary", "arbitrary"),
        name=f"hy_conv{order}",
    )(z, p_hy, conv_w, conv_b, conv_w, conv_b, skip[order:order + 1], kf, m1f, gf, gb, m1i)


def _hy_ctx_kernel(p_ref, h2_ref, w3_ref, b3_ref, delta_ref,
                   cw_ref, cb_ref, skip_ref, fd_ref, fi_ref, o_ref):
    lc = p_ref.shape[1]
    C = skip_ref.shape[1]
    h2 = h2_ref[...]
    u = _conv3(p_ref[0], cw_ref[...], cb_ref[...])
    z = u[:, 0:C]
    fd = fd_ref[...]
    fi = fi_ref[...]
    nf = fd.shape[0] // 2
    dot = lambda a, b: jnp.dot(a.astype(BF16), b.astype(BF16), preferred_element_type=F32)
    for o in range(2):
        base = 2 * o * C
        hf = _hy_taps(h2, w3_ref.at[:, base:base + C], b3_ref.at[:, base:base + C], delta_ref, False)
        hb = _hy_taps(h2, w3_ref.at[:, base + C:base + 2 * C], b3_ref.at[:, base + C:base + 2 * C],
                      delta_ref, True)
        inv = 1.0 / ((jnp.sum(jnp.abs(hf), axis=0, keepdims=True)
                      + jnp.sum(jnp.abs(hb), axis=0, keepdims=True)) * (2.0 * lc))
        xf = dot(fd, hf * inv)
        xb = dot(fd, hb * inv)
        kr = xf[:nf] + xb[:nf]
        ki = xf[nf:] - xb[nf:]
        x = dot(fd, z)
        xr, xi = x[:nf], x[nf:]
        y = dot(fi, jnp.concatenate([xr * kr - xi * ki, xr * ki + xi * kr], axis=0))
        z = u[:, (o + 1) * C:(o + 2) * C] * (y + skip_ref[o:o + 1, :] * z)
    for j in range(C // LANES):
        o_ref[0, :, j * LANES:(j + 1) * LANES] = _group_rms_2x64(z[:, j * LANES:(j + 1) * LANES]).astype(BF16)


def _dense_dft_tables(lc):
    n = 2 * lc
    k = np.arange(n)[:, None]
    t = np.arange(lc)[None, :]
    th = 2 * np.pi * ((k * t) % n) / n
    fd = np.concatenate([np.cos(th), -np.sin(th)], axis=0)
    fi = np.concatenate([np.cos(th).T, -np.sin(th).T], axis=1)
    return jnp.asarray(fd.astype(np.float32)), jnp.asarray(fi.astype(np.float32))


def _hy_ctx(p_hy, h2, w3, b3, delta, conv_w, conv_b, skip, fd, fi):
    B, lc, w = p_hy.shape
    C = skip.shape[1]
    full = lambda a: pl.BlockSpec(a.shape, lambda b: (0,) * a.ndim)
    args = (h2, w3, b3, delta, conv_w, conv_b, skip, fd, fi)
    return pl.pallas_call(
        _hy_ctx_kernel,
        grid=(B,),
        in_specs=[pl.BlockSpec((1, lc, w), lambda b: (b, 0, 0))] + [full(a) for a in args],
        out_specs=pl.BlockSpec((1, lc, C), lambda b: (b, 0, 0)),
        out_shape=jax.ShapeDtypeStruct((B, lc, C), BF16),
        compiler_params=_cparams("parallel"),
        name="hy_ctx",
    )(p_hy, *args)


def _attn_kernel(seg_lens, lam_init, q_ref, lam_ref, *refs):
    nseg = len(seg_lens)
    k_refs, v_refs = refs[:nseg], refs[nseg:2 * nseg]
    o_ref = refs[2 * nseg]
    n_s, n_p = 3, 2
    s_refs = refs[2 * nseg + 1:2 * nseg + 1 + n_s]
    p_flat = refs[2 * nseg + 1 + n_s:2 * nseg + 1 + n_s + 2 * n_p]
    p_refs = [p_flat[2 * i:2 * i + 2] for i in range(n_p)]
    tq = q_ref.shape[1]
    th = s_refs[0].shape[0]
    dl = lam_ref[...]
    lam = (jnp.exp(jnp.sum(dl[0:1] * dl[1:2], keepdims=True))
           - jnp.exp(jnp.sum(dl[2:3] * dl[3:4], keepdims=True))) + lam_init

    chunks = []
    off = 0
    for s, n in enumerate(seg_lens):
        for st in range(0, n, TK):
            w = min(TK, n - st)
            chunks.append((s, st, off, w))
            off += w

    nh = tq // th

    def lane_tiles(a):
        return [a[:, t * LANES:(t + 1) * LANES] for t in range(a.shape[1] // LANES)]

    def scores(h, m):
        q = q_ref[0, h * th:(h + 1) * th, :]
        lane = lax.broadcasted_iota(jnp.int32, q.shape, 1)
        qm = jnp.where((lane < HEAD_DIM) == (m == 0), q, jnp.zeros_like(q))
        s_ref = s_refs[(2 * h + m) % n_s]
        mxw = jnp.full((th, LANES), -jnp.inf, F32)
        for (s, st, o, w) in chunks:
            sc = lax.dot_general(qm, k_refs[s][0, st:st + w, :], (((1,), (1,)), ((), ())),
                                 preferred_element_type=F32)
            s_ref[:, o:o + w] = sc
            for t in lane_tiles(sc):
                mxw = jnp.maximum(mxw, t)
        return jnp.max(mxw, axis=-1, keepdims=True)

    def probs(h, m, mx):
        s_ref = s_refs[(2 * h + m) % n_s]
        p_ref = p_refs[h % n_p][m]
        lw = jnp.zeros((th, LANES), F32)
        for (s, st, o, w) in chunks:
            p = jnp.exp2(s_ref[:, o:o + w] - mx)
            for t in lane_tiles(p):
                lw = lw + t
            p_ref[:, o:o + w] = p.astype(BF16)
        return 1.0 / jnp.sum(lw, axis=-1, keepdims=True)

    def output(h, inv_l0, inv_l1):
        p0_ref, p1_ref = p_refs[h % n_p]
        r = jnp.broadcast_to(lam * inv_l1 / inv_l0, (th, LANES)).astype(BF16)
        acc = jnp.zeros((th, v_refs[0].shape[2]), F32)
        for (s, st, o, w) in chunks:
            rb = jnp.concatenate([r] * (w // LANES), axis=1)
            wgt = p0_ref[:, o:o + w] - p1_ref[:, o:o + w] * rb
            acc = acc + jnp.dot(wgt, v_refs[s][0, st:st + w, :], preferred_element_type=F32)
        o_ref[0, h * th:(h + 1) * th, :] = (_rms(acc * inv_l0) * (1.0 - lam_init)).astype(BF16)

    mx = {(0, 0): scores(0, 0), (0, 1): scores(0, 1)}
    for h in range(nh):
        il0 = probs(h, 0, mx[h, 0])
        if h + 1 < nh:
            mx[h + 1, 0] = scores(h + 1, 0)
        il1 = probs(h, 1, mx[h, 1])
        if h + 1 < nh:
            mx[h + 1, 1] = scores(h + 1, 1)
        output(h, il0, il1)


def _diff_attn(q, ks, vs, da_lambda, lam_init):
    B, lq, w = q.shape
    H = w // LANES
    tq = min(TQ, lq)
    th = min(TQ_SUB, tq // 2)
    seg_lens = tuple(k.shape[1] for k in ks)
    seg = lambda n: pl.BlockSpec((1, n, LANES), lambda b, h, i: (b, 0, h))
    return pl.pallas_call(
        functools.partial(_attn_kernel, seg_lens, lam_init),
        grid=(B, H, lq // tq),
        in_specs=[pl.BlockSpec((1, tq, LANES), lambda b, h, i: (b, i, h)),
                  pl.BlockSpec(da_lambda.shape, lambda b, h, i: (0, 0))]
                 + [seg(n) for n in seg_lens] + [seg(n) for n in seg_lens],
        out_specs=pl.BlockSpec((1, tq, LANES), lambda b, h, i: (b, i, h)),
        out_shape=jax.ShapeDtypeStruct((B, lq, w), BF16),
        scratch_shapes=[pltpu.VMEM((th, sum(seg_lens)), F32)] * 3
                       + [pltpu.VMEM((th, sum(seg_lens)), BF16)] * 4,
        compiler_params=_cparams("parallel", "parallel", "arbitrary"),
        name="diff_attn",
    )(q, da_lambda, *ks, *vs)


def _split3(x):
    hi = x.astype(BF16)
    r1 = x - hi.astype(F32)
    mid = r1.astype(BF16)
    return hi, mid, (r1 - mid.astype(F32)).astype(BF16)


def _mlstm_cumsums(gc, gr, tri_lo, tri_up):
    dot = lambda a, b: jnp.dot(a, b, preferred_element_type=F32)
    lf_c = _log_sigmoid(gc)
    lf_r = _log_sigmoid(gr)
    T = gc.shape[0]
    pre_c = sum(dot(tri_lo, t) for t in _split3(lf_c))
    pre_r = sum(dot(t, tri_up) for t in _split3(lf_r))
    suf_c = pre_c[T - 1:T, :] - pre_c + lf_c
    suf_r = pre_r[:, T - 1:T] - pre_r + lf_r
    fwd_lane = lax.broadcasted_iota(jnp.int32, gc.shape, 1) < 4
    fwd_row = lax.broadcasted_iota(jnp.int32, gr.shape, 0) < 4
    return jnp.where(fwd_lane, pre_c, suf_c), jnp.where(fwd_row, pre_r, suf_r)


def _mlstm_chunk(rev, q, k, vt, gc, gr, csum_c, csum_r, c_ref, n_ref, m_ref):
    T = q.shape[0]
    d = 6 if rev else 2
    ii = 4 if rev else 0
    lo = lax.broadcasted_iota(jnp.int32, (T, LANES), 1) < HEAD_DIM
    lo_row = lax.broadcasted_iota(jnp.int32, (1, LANES), 1) < HEAD_DIM
    sub = lax.broadcasted_iota(jnp.int32, (LANES, T), 0) < HEAD_DIM
    key_i = lax.broadcasted_iota(jnp.int32, (T, T), 0)
    qry_i = lax.broadcasted_iota(jnp.int32, (T, T), 1)
    mask = (key_i >= qry_i) if rev else (key_i <= qry_i)
    last = 0 if rev else T - 1
    qb = q.astype(BF16)
    kb = k.astype(BF16)
    vtb = vt.astype(BF16)
    zero = jnp.zeros_like(kb)
    nt = (((1,), (1,)), ((), ()))
    brow, igrow, mloc, numt, dsum = [], [], [], [], []
    for j in range(2):
        brow.append(csum_r[d + j:d + j + 1, :])
        igrow.append(gr[ii + j:ii + j + 1, :])
        ccol = gc[:, ii + j:ii + j + 1] - csum_c[:, d + j:d + j + 1]
        logd = jnp.where(mask, brow[j] + ccol, -jnp.inf)
        mloc.append(jnp.max(logd, axis=0, keepdims=True))
        kj = jnp.where(lo, kb, zero) if j == 0 else jnp.where(lo, zero, kb)
        st = lax.dot_general(kj, qb, nt, preferred_element_type=F32) * jnp.exp(logd - mloc[j])
        numt.append(jnp.dot(vtb, st.astype(BF16), preferred_element_type=F32))
        dsum.append(jnp.sum(st, axis=0, keepdims=True))
    num_loc = jnp.where(sub, numt[0], numt[1])

    n = n_ref[...]
    row8 = lax.broadcasted_iota(jnp.int32, (SUBLANES, LANES), 0)
    lo8 = lax.broadcasted_iota(jnp.int32, (SUBLANES, LANES), 1) < HEAD_DIM
    n8 = jnp.where(row8 == jnp.where(lo8, 0, 1), jnp.broadcast_to(n, (SUBLANES, LANES)), 0.0)
    qn = lax.dot_general(n8.astype(BF16), qb, nt, preferred_element_type=F32)
    hden, wrow, decay, interw, aloc, m_new = [], [], [], [], [], []
    for j in range(2):
        m_prev = m_ref[j:j + 1, 0:1]
        mrow = jnp.maximum(brow[j] + m_prev, mloc[j])
        a = jnp.exp(mloc[j] - mrow)
        iw = jnp.exp(brow[j] + m_prev - mrow)
        dn = a * dsum[j] + iw * qn[j:j + 1, :]
        hden.append(jnp.maximum(jnp.abs(dn), jnp.exp(-mrow)))
        aloc.append(a)
        interw.append(iw)
        mn = mrow[:, last:last + 1]
        btot = brow[j][:, last:last + 1]
        wrow.append(jnp.exp(btot - brow[j] + igrow[j] - mn))
        decay.append(jnp.exp(btot + m_prev - mn))
        m_new.append(mn)
    c = c_ref[...]
    intert = lax.dot_general(c.astype(BF16), qb, nt, preferred_element_type=F32)
    num = jnp.where(sub, aloc[0], aloc[1]) * num_loc + jnp.where(sub, interw[0], interw[1]) * intert
    ht = num / jnp.where(sub, hden[0], hden[1])
    wvt = (vt * jnp.where(sub, wrow[0], wrow[1])).astype(BF16)
    upd = jnp.dot(wvt, kb, preferred_element_type=F32)
    rr = lax.broadcasted_iota(jnp.int32, (LANES, LANES), 0) < HEAD_DIM
    cc = lax.broadcasted_iota(jnp.int32, (LANES, LANES), 1) < HEAD_DIM
    dcol = jnp.where(lax.broadcasted_iota(jnp.int32, (LANES, 1), 0) < HEAD_DIM, decay[0], decay[1])
    c_ref[...] = dcol * c + jnp.where(rr == cc, upd, 0.0)
    rowt = lax.broadcasted_iota(jnp.int32, (SUBLANES, T), 0)
    w8 = jnp.where(rowt == 0, wrow[0], jnp.where(rowt == 1, wrow[1], 0.0))
    r8 = jnp.dot(w8.astype(BF16), kb, preferred_element_type=F32)
    n_ref[...] = jnp.where(lo_row, decay[0] * n + r8[0:1, :], decay[1] * n + r8[1:2, :])
    m_ref[0:1, :] = jnp.broadcast_to(m_new[0], (1, LANES))
    m_ref[1:2, :] = jnp.broadcast_to(m_new[1], (1, LANES))
    return ht


def _mlstm_kernel(qc_ref, kc_ref, vc_ref, oc_ref, gcc_ref, grc_ref,
                  qx_ref, kx_ref, vx_ref, ox_ref, gcx_ref, grx_ref,
                  cwq_ref, cbq_ref, cwk_ref, cbk_ref, gbc_ref, gbr_ref,
                  yc_ref, yx_ref,
                  qs_ref, ks_ref, vt_ref, hf_ref, hb_ref, csc_ref, csr_ref, ct_ref, n_ref, m_ref):
    T = ML_T
    lx = qx_ref.shape[1]
    nx = lx // T
    r_i = lax.broadcasted_iota(jnp.int32, (T, T), 0)
    c_i = lax.broadcasted_iota(jnp.int32, (T, T), 1)
    tri_lo = jnp.where(c_i <= r_i, 1.0, 0.0).astype(BF16)
    tri_up = jnp.where(c_i >= r_i, 1.0, 0.0).astype(BF16)
    ct_ref[...] = jnp.zeros_like(ct_ref)
    n_ref[...] = jnp.zeros_like(n_ref)
    m_ref[...] = jnp.zeros_like(m_ref)
    scale = HEAD_DIM ** -0.5
    cwq, cbq, cwk, cbk = cwq_ref[...], cbq_ref[...], cwk_ref[...], cbk_ref[...]
    gbc, gbr = gbc_ref[...], gbr_ref[...]

    qc = _silu(_conv3(qc_ref[0], cwq, cbq))
    kc = _silu(_conv3(kc_ref[0], cwk, cbk)) * scale
    vct = vc_ref[0].T
    gcc = gcc_ref[0] + gbc
    grc = grc_ref[0] + gbr
    h_ct = None
    csc, csr = _mlstm_cumsums(gcc, grc, tri_lo, tri_up)
    for rev in (False, True):
        di = int(rev)
        ht = _mlstm_chunk(rev, qc, kc, vct, gcc, grc, csc, csr,
                          ct_ref.at[di], n_ref.at[di], m_ref.at[di])
        h_ct = ht if h_ct is None else h_ct + ht
    yc_ref[0] = _group_rms_2x64(h_ct.T * _sigmoid(oc_ref[0])).astype(BF16)

    qs_ref[...] = _silu(_conv3(qx_ref[0], cwq, cbq))
    ks_ref[...] = _silu(_conv3(kx_ref[0], cwk, cbk)) * scale
    for c in range(nx):
        rows = slice(c * T, (c + 1) * T)
        vt_ref[:, rows] = vx_ref[0, rows, :].T
        csc_ref[rows, :], csr_ref[:, rows] = _mlstm_cumsums(gcx_ref[0, rows, :] + gbc,
                                                            grx_ref[0, :, rows] + gbr, tri_lo, tri_up)

    def body(c, carry):
        for rev in (False, True):
            di = int(rev)
            cidx = (nx - 1 - c) if rev else c
            r0 = pl.multiple_of(cidx * T, T)
            rows = pl.ds(r0, T)
            ht = _mlstm_chunk(rev, qs_ref[rows, :], ks_ref[rows, :], vt_ref[:, rows],
                              gcx_ref[0, rows, :] + gbc, grx_ref[0, :, rows] + gbr,
                              csc_ref[rows, :], csr_ref[:, rows],
                              ct_ref.at[di], n_ref.at[di], m_ref.at[di])
            if rev:
                hb_ref[:, rows] = ht
            else:
                hf_ref[:, rows] = ht
        return carry

    lax.fori_loop(0, nx, body, 0, unroll=4)
    for c in range(nx):
        rows = slice(c * T, (c + 1) * T)
        h = (hf_ref[:, rows] + hb_ref[:, rows]).T
        yx_ref[0, rows, :] = _group_rms_2x64(h * _sigmoid(ox_ref[0, rows, :])).astype(BF16)


def _mlstm(seg_c, seg_x, conv_w, conv_b, gate_bc, gate_br):
    B, lx, _ = seg_x[1].shape
    lc = seg_c[1].shape[1]
    assert lc == ML_T and lx % ML_T == 0

    def seg_specs(L):
        col = lambda off: pl.BlockSpec((1, L, LANES), lambda b, p, off=off: (b, 0, off + p))
        return [col(0), col(2), col(0), col(0), col(0),
                pl.BlockSpec((1, SUBLANES, L), lambda b, p: (b, p, 0))]

    def seg_args(s):
        qk, v, o, g, gt = s
        return [qk, qk, v, o, g, gt]

    wspec = lambda rows, off: pl.BlockSpec((rows, LANES), lambda b, p, off=off: (0, off + p))
    return pl.pallas_call(
        _mlstm_kernel,
        grid=(B, 2),
        in_specs=seg_specs(lc) + seg_specs(lx)
                 + [wspec(3, 0), wspec(1, 0), wspec(3, 2), wspec(1, 2), wspec(1, 0),
                    pl.BlockSpec((SUBLANES, 1), lambda b, p: (p, 0))],
        out_specs=[pl.BlockSpec((1, lc, LANES), lambda b, p: (b, 0, p)),
                   pl.BlockSpec((1, lx, LANES), lambda b, p: (b, 0, p))],
        out_shape=[jax.ShapeDtypeStruct((B, lc, 2 * LANES), BF16),
                   jax.ShapeDtypeStruct((B, lx, 2 * LANES), BF16)],
        scratch_shapes=[pltpu.VMEM((lx, LANES), F32), pltpu.VMEM((lx, LANES), F32),
                        pltpu.VMEM((LANES, lx), F32), pltpu.VMEM((LANES, lx), F32),
                        pltpu.VMEM((LANES, lx), F32),
                        pltpu.VMEM((lx, LANES), F32), pltpu.VMEM((SUBLANES, lx), F32),
                        pltpu.VMEM((2, LANES, LANES), F32), pltpu.VMEM((2, 1, LANES), F32),
                        pltpu.VMEM((2, 2, LANES), F32)],
        compiler_params=_cparams("parallel", "arbitrary"),
        name="mlstm",
    )(*seg_args(seg_c), *seg_args(seg_x), conv_w, conv_b, conv_w, conv_b, gate_bc, gate_br)


def _outproj_kernel(x_ref, g_ref, hy_ref, da_ref, ml_ref, gain_ref, w_ref, o_ref):
    c_hy = hy_ref.shape[2]
    c_da = da_ref.shape[2]
    gain = gain_ref[...]
    acc = jnp.dot((hy_ref[0] * gain[:, 0:c_hy]).astype(BF16), w_ref[0:c_hy, :], preferred_element_type=F32)
    acc += jnp.dot((da_ref[0] * gain[:, c_hy:c_hy + c_da]).astype(BF16), w_ref[c_hy:c_hy + c_da, :],
                   preferred_element_type=F32)
    acc += jnp.dot((ml_ref[0] * gain[:, c_hy + c_da:]).astype(BF16), w_ref[c_hy + c_da:, :],
                   preferred_element_type=F32)
    o_ref[0] = x_ref[0] + g_ref[0] * acc


def _outproj(x, gate, y_hy, y_da, y_ml, gain, w_out):
    B, L, D = x.shape
    tm = min(TM, L)
    tok = lambda w: pl.BlockSpec((1, tm, w), lambda b, i: (b, i, 0))
    return pl.pallas_call(
        _outproj_kernel,
        grid=(B, L // tm),
        in_specs=[tok(D), pl.BlockSpec((1, 1, D), lambda b, i: (b, 0, 0)),
                  tok(y_hy.shape[2]), tok(y_da.shape[2]), tok(y_ml.shape[2]),
                  pl.BlockSpec(gain.shape, lambda b, i: (0, 0)), _const_spec(w_out.shape)],
        out_specs=tok(D),
        out_shape=jax.ShapeDtypeStruct((B, L, D), F32),
        compiler_params=_cparams("parallel", "parallel"),
        name="outproj",
    )(x, gate, y_hy, y_da, y_ml, gain, w_out)


def _ffn_kernel(final, x_ref, prev_ref, next_ref, sh_ref, sc_ref, g_ref, up_ref, cw_ref, cb_ref,
                down_ref, fw_ref, o_ref, act_ref):
    i = pl.program_id(1)
    last = pl.num_programs(1) - 1
    tm = x_ref.shape[1]
    d_ff = down_ref.shape[0]
    x = x_ref[0]
    mod = lambda a: _rms(a) * (1.0 + sc_ref[0]) + sh_ref[0]
    hp = jnp.where(i == 0, 0.0, mod(prev_ref[0]))
    hn = jnp.where(i == last, 0.0, mod(next_ref[0]))
    h = jnp.concatenate([hp, mod(x), hn], axis=0).astype(BF16)
    ext = tm + 2 * SUBLANES

    def conv_cols(lo, hi):
        u = jnp.dot(h, up_ref[:, lo:hi], preferred_element_type=F32)
        w = cw_ref[:, lo:hi]
        c = (pltpu.roll(u, 1, axis=0) * w[0:1] + u * w[1:2] + pltpu.roll(u, ext - 1, axis=0) * w[2:3]
             + cb_ref[:, lo:hi])
        return c[SUBLANES:SUBLANES + tm]

    for j in range(d_ff // FFN_TN):
        lo = j * FFN_TN
        a = conv_cols(lo, lo + FFN_TN)
        g = conv_cols(d_ff + lo, d_ff + lo + FFN_TN)
        act_ref[:, lo:lo + FFN_TN] = (_silu(g) * a).astype(BF16)
    y = x + g_ref[0] * jnp.dot(act_ref[...], down_ref[...], preferred_element_type=F32)
    if final:
        y = _rms(y) * fw_ref[...]
    o_ref[0] = y


def _ffn(x, shift, scale, gate, up, conv_w, conv_b, down, final_w, final):
    B, L, D = x.shape
    tm = min(TM, L)
    nb = tm // SUBLANES
    nrow = L // SUBLANES
    tok = pl.BlockSpec((1, tm, D), lambda b, i: (b, i, 0))
    mod = pl.BlockSpec((1, 1, D), lambda b, i: (b, 0, 0))
    full = lambda a: pl.BlockSpec(a.shape, lambda b, i: (0,) * a.ndim)
    return pl.pallas_call(
        functools.partial(_ffn_kernel, final),
        grid=(B, L // tm),
        in_specs=[tok,
                  pl.BlockSpec((1, SUBLANES, D), lambda b, i: (b, jnp.maximum(i * nb - 1, 0), 0)),
                  pl.BlockSpec((1, SUBLANES, D), lambda b, i: (b, jnp.minimum((i + 1) * nb, nrow - 1), 0)),
                  mod, mod, mod, _const_spec(up.shape), full(conv_w), full(conv_b),
                  _const_spec(down.shape), full(final_w)],
        out_specs=tok,
        out_shape=jax.ShapeDtypeStruct((B, L, D), F32),
        scratch_shapes=[pltpu.VMEM((tm, down.shape[0]), BF16)],
        compiler_params=_cparams("parallel", "parallel"),
        name="ffn_final" if final else "ffn",
    )(x, x, x, shift, scale, gate, up, conv_w, conv_b, down, final_w)


def _rope_tables(L):
    rows_n = L // GRID_W
    rows = jnp.repeat(jnp.arange(rows_n, dtype=F32), GRID_W)
    cols = jnp.tile(jnp.arange(GRID_W, dtype=F32), rows_n)
    nf = HEAD_DIM // 4
    inv = ROPE_BASE ** (-jnp.arange(nf, dtype=F32) / nf)
    cr, sr = jnp.cos(rows[:, None] * inv), jnp.sin(rows[:, None] * inv)
    cc, sc = jnp.cos(cols[:, None] * inv), jnp.sin(cols[:, None] * inv)
    cos64 = jnp.concatenate([cr, cr, cc, cc], axis=-1)
    sin64 = jnp.concatenate([-sr, sr, -sc, sc], axis=-1)
    return jnp.tile(cos64, (1, 2)), jnp.tile(sin64, (1, 2))


def _hy_features(L):
    t = jnp.linspace(0.0, 1.0, L, dtype=F32)
    pos = jnp.arange(L, dtype=F32)
    f = jnp.linspace(1e-4, HY_BANDS - 1, HY_BANDS, dtype=F32)
    ang = (2.0 * math.pi / L) * pos[:, None] * f
    z = jnp.concatenate([t[:, None], jnp.cos(ang), jnp.sin(ang)], axis=-1)
    return jnp.pad(z, ((0, 0), (0, HY_POS_PAD - z.shape[1])))


def _gate_layout(a):
    g = a.reshape(a.shape[:-1] + (4, 2, 2))
    g = jnp.moveaxis(g, -2, -3).reshape(a.shape[:-1] + (2, 8))
    g = jnp.pad(g, [(0, 0)] * (g.ndim - 1) + [(0, LANES - 8)])
    return g.reshape(a.shape[:-1] + (2 * LANES,))


def _gates_t(g):
    return jnp.swapaxes(jnp.concatenate([g[..., 0:8], g[..., LANES:LANES + 8]], axis=-1), 1, 2)


def kernel(x, c, ctx, c_ctx, ada_w, ada_b, w_in, w_out, hy_conv_w, hy_conv_b, hy_w1, hy_b1, hy_w2, hy_b2,
           hy_w3, hy_b3, hy_skip, da_lambda, ml_conv_w, ml_conv_b, ml_gate_b, mix_norm_w, ffn_up,
           ffn_conv_w, ffn_conv_b, ffn_down, final_norm_w):
    B, L, D = x.shape
    lc = ctx.shape[1]
    depth = ada_w.shape[0]
    hy_w = hy_skip.shape[2]
    n_hy = 3 * hy_w
    da_w = 2 * hy_w
    ml_w = hy_w

    rows = ((B + 1 + SUBLANES - 1) // SUBLANES) * SUBLANES
    cc = jnp.zeros((rows, D), F32).at[:B].set(c).at[B].set(c_ctx)
    mods = _adaln(cc, ada_w, ada_b)

    cos_x, sin_x = _rope_tables(L)
    cos_c, sin_c = jnp.ones((lc, LANES), F32), jnp.zeros((lc, LANES), F32)
    tables = _dft_tables(L)
    fd_c, fi_c = _dense_dft_tables(lc)
    zf_x, zf_c = _hy_features(L), _hy_features(lc)
    delta = jnp.abs(jnp.linspace(math.log(HY_DECAY_TARGET) / HY_FAST_DECAY,
                                 math.log(HY_DECAY_TARGET) / HY_SLOW_DECAY, hy_w, dtype=F32))[None, :]
    fw = final_norm_w[None, :]

    for l in range(depth):
        lam_init = 0.8 - 0.6 * math.exp(-0.3 * l)
        update_ctx = l < depth - 1
        mx = [m[:, None, :] for m in jnp.split(mods[l, :B], 6, axis=-1)]
        mc = [jnp.broadcast_to(m[None], (B, 1, D)) for m in jnp.split(mods[l, B:B + 1], 6, axis=-1)]

        w = w_in[l]
        n_main = n_hy + 3 * da_w + 4 * ml_w
        w_ext = jnp.concatenate([w[:, :n_main], _gate_layout(w[:, n_main:])], axis=1).astype(BF16)
        gate_b = _gate_layout(ml_gate_b[l])
        gate_bc = gate_b[None, :]
        gate_br = _gates_t(gate_b[None, None, :])[0]

        px = _inproj(x, mx[0], mx[1], w_ext, cos_x, sin_x)
        pc = _inproj(ctx, mc[0], mc[1], w_ext, cos_c, sin_c)
        hy_x, q_x, k_x, v_x, mqk_x, mv_x, mo_x, g_x = px
        hy_c, q_c, k_c, v_c, mqk_c, mv_c, mo_c, g_c = pc

        w1p = jnp.pad(hy_w1[l], ((0, HY_POS_PAD - hy_w1.shape[1]), (0, 0)))
        b1, b2, b3 = hy_b1[l][None, :], hy_b2[l][None, :], hy_b3[l][None, :]
        cbias = hy_conv_b[l][None, :]
        kf = _hy_filter(_hy_hidden(zf_x, w1p, b1, hy_w2[l], b2), hy_w3[l], b3, delta, tables[0], tables[1])
        z1 = _hy_conv(0, hy_x, hy_x, hy_conv_w[l], cbias, hy_skip[l], kf, tables)
        y_hy_x = _hy_conv(1, z1, hy_x, hy_conv_w[l], cbias, hy_skip[l], kf, tables)

        y_da_x = _diff_attn(q_x, [k_c, k_x], [v_c, v_x], da_lambda[l], lam_init)

        seg_c = (mqk_c, mv_c, mo_c, g_c, _gates_t(g_c))
        seg_x = (mqk_x, mv_x, mo_x, g_x, _gates_t(g_x))
        y_ml_c, y_ml_x = _mlstm(seg_c, seg_x, ml_conv_w[l], ml_conv_b[l][None, :], gate_bc, gate_br)

        gain = mix_norm_w[l][None, :]
        wo = w_out[l].astype(BF16)
        up = ffn_up[l].astype(BF16)
        down = ffn_down[l].astype(BF16)
        fcb = ffn_conv_b[l][None, :]
        x = _outproj(x, mx[2], y_hy_x, y_da_x, y_ml_x, gain, wo)
        x = _ffn(x, mx[3], mx[4], mx[5], up, ffn_conv_w[l], fcb, down, fw, final=not update_ctx)

        if update_ctx:
            y_hy_c = _hy_ctx(hy_c, _hy_hidden(zf_c, w1p, b1, hy_w2[l], b2), hy_w3[l], b3, delta,
                             hy_conv_w[l], cbias, hy_skip[l], fd_c, fi_c)
            y_da_c = _diff_attn(q_c, [k_c], [v_c], da_lambda[l], lam_init)
            ctx = _outproj(ctx, mc[2], y_hy_c, y_da_c, y_ml_c, gain, wo)
            ctx = _ffn(ctx, mc[3], mc[4], mc[5], up, ffn_conv_w[l], fcb, down, fw, final=False)
    return x
```

```python
import functools
import math

import numpy as np
import jax
import jax.numpy as jnp
from jax import lax
from jax.experimental import pallas as pl
from jax.experimental.pallas import tpu as pltpu

F32 = jnp.float32
BF16 = jnp.bfloat16
HI = lax.Precision.HIGHEST

EPS = 1e-6
HEAD_DIM = 64
GRID_W = 64
ROPE_BASE = 10000.0
HY_BANDS = 16
HY_POS_PAD = 64
HY_DECAY_TARGET = 1e-2
HY_FAST_DECAY = 0.3
HY_SLOW_DECAY = 1.5
LANES = 128
SUBLANES = 8
VMEM_LIMIT = 56 * 1024 * 1024

FFT_N1 = 64
FFT_N2 = 128
FFT_UNROLL = 8
ML_T = 256
TQ = 2048
TQ_SUB = 512
TK = 512
TM = 1024
FFN_TN = 256


def _cparams(*sem):
    return pltpu.CompilerParams(dimension_semantics=sem, vmem_limit_bytes=VMEM_LIMIT)


def _const_spec(shape):
    n = len(shape)
    return pl.BlockSpec(shape, lambda *_: (0,) * n, pipeline_mode=pl.Buffered(1))


def _silu(x):
    return x / (1.0 + jnp.exp(-x))


def _sigmoid(x):
    return 1.0 / (1.0 + jnp.exp(-x))


def _log_sigmoid(x):
    return jnp.minimum(x, 0.0) - jnp.log(1.0 + jnp.exp(-jnp.abs(x)))


def _rms(x):
    return x * lax.rsqrt(jnp.mean(x * x, axis=-1, keepdims=True) + EPS)


def _group_rms_2x64(y):
    lo = lax.broadcasted_iota(jnp.int32, y.shape, 1) < 64
    y2 = y * y
    s_lo = jnp.sum(jnp.where(lo, y2, 0.0), axis=-1, keepdims=True)
    s_hi = jnp.sum(jnp.where(lo, 0.0, y2), axis=-1, keepdims=True)
    r = jnp.where(lo, lax.rsqrt(s_lo * (1.0 / 64) + EPS), lax.rsqrt(s_hi * (1.0 / 64) + EPS))
    return y * r


def _conv3(x, w, b):
    L = x.shape[0]
    row = lax.broadcasted_iota(jnp.int32, x.shape, 0)
    xm = jnp.where(row == 0, 0.0, pltpu.roll(x, 1, axis=0))
    xp = jnp.where(row == L - 1, 0.0, pltpu.roll(x, L - 1, axis=0))
    return xm * w[0:1] + x * w[1:2] + xp * w[2:3] + b


def _adaln_kernel(c_ref, w_ref, b_ref, o_ref):
    a = _silu(c_ref[...])
    o_ref[0] = jnp.dot(a, w_ref[0], precision=HI, preferred_element_type=F32) + b_ref[0]


def _adaln(cc, ada_w, ada_b):
    depth, d, n6 = ada_w.shape
    tn = 1536
    return pl.pallas_call(
        _adaln_kernel,
        grid=(depth, n6 // tn),
        in_specs=[pl.BlockSpec(cc.shape, lambda l, j: (0, 0)),
                  pl.BlockSpec((1, d, tn), lambda l, j: (l, 0, j)),
                  pl.BlockSpec((1, 1, tn), lambda l, j: (l, 0, j))],
        out_specs=pl.BlockSpec((1, cc.shape[0], tn), lambda l, j: (l, 0, j)),
        out_shape=jax.ShapeDtypeStruct((depth, cc.shape[0], n6), F32),
        compiler_params=_cparams("parallel", "parallel"),
        name="adaln",
    )(cc, ada_w, ada_b.reshape(depth, 1, n6))


_C_HY, _C_Q, _C_K, _C_V, _C_MQK, _C_MV, _C_MO, _C_G, _C_END = (
    0, 768, 1280, 1792, 2304, 2816, 3072, 3328, 3584)


def _inproj_kernel(x_ref, sh_ref, sc_ref, w_ref, cos_ref, sin_ref, perm_ref,
                   hy_ref, q_ref, k_ref, v_ref, mqk_ref, mv_ref, mo_ref, g_ref):
    h = (_rms(x_ref[0]) * (1.0 + sc_ref[0]) + sh_ref[0]).astype(BF16)

    def proj(lo, hi):
        return jnp.dot(h, w_ref[:, lo:hi], preferred_element_type=F32)

    hy_ref[0] = proj(_C_HY, _C_Q)
    cos = cos_ref[...]
    sin = sin_ref[...]
    perm = perm_ref[...]

    def rope(c0, c1, out_ref, scale):
        a = proj(c0, c1)
        for j in range((c1 - c0) // LANES):
            sl = slice(LANES * j, LANES * (j + 1))
            asw = jnp.dot(a[:, sl].astype(BF16), perm, preferred_element_type=F32)
            out_ref[0, :, sl] = ((a[:, sl] * cos + asw * sin) * scale).astype(BF16)

    rope(_C_Q, _C_K, q_ref, (HEAD_DIM ** -0.5) * math.log2(math.e))
    rope(_C_K, _C_V, k_ref, 1.0)
    v_ref[0] = proj(_C_V, _C_MQK).astype(BF16)
    mqk_ref[0] = proj(_C_MQK, _C_MV)
    mv_ref[0] = proj(_C_MV, _C_MO)
    mo_ref[0] = proj(_C_MO, _C_G)
    g_ref[0] = proj(_C_G, _C_END)


def _inproj(x, shift, scale, w_ext, cos, sin):
    B, L, D = x.shape
    tm = min(TM, L)
    idx = np.arange(LANES)
    perm = jnp.asarray((idx[:, None] == (idx[None, :] ^ (HEAD_DIM // 4))).astype(np.float32)).astype(BF16)
    tok = lambda w: pl.BlockSpec((1, tm, w), lambda b, i: (b, i, 0))
    mod = pl.BlockSpec((1, 1, D), lambda b, i: (b, 0, 0))
    tab = pl.BlockSpec((tm, LANES), lambda b, i: (i, 0))
    widths = (768, 512, 512, 512, 512, 256, 256, 256)
    dtypes = (F32, BF16, BF16, BF16, F32, F32, F32, F32)
    return pl.pallas_call(
        _inproj_kernel,
        grid=(B, L // tm),
        in_specs=[tok(D), mod, mod, _const_spec(w_ext.shape), tab, tab,
                  pl.BlockSpec(perm.shape, lambda b, i: (0, 0))],
        out_specs=[tok(w) for w in widths],
        out_shape=[jax.ShapeDtypeStruct((B, L, w), dt) for w, dt in zip(widths, dtypes)],
        compiler_params=_cparams("parallel", "parallel"),
        name="inproj",
    )(x, shift, scale, w_ext, cos, sin, perm)


def _dft_tables(L):
    n1, n2, g = FFT_N1, FFT_N2, SUBLANES
    N = n1 * n2
    assert N == 2 * L
    nk = n1 // 2 + 1
    k1 = np.arange(nk)[:, None]
    m1 = np.arange(n1 // 2)[None, :]
    th = 2 * np.pi * ((k1 * m1) % n1) / n1
    f = np.stack([np.cos(th), -np.sin(th)], axis=1)
    fold = np.where((k1 == 0) | (k1 == n1 // 2), 1.0, 2.0)[:, :, None]
    eye = np.eye(g)
    m1f = np.einsum('krn,gh->krgnh', f, eye).reshape(2 * nk * g, (n1 // 2) * g)
    m1i = np.einsum('krn,gh->ngkrh', f * fold, eye).reshape((n1 // 2) * g, 2 * nk * g)
    a = np.arange(n2)
    th2 = 2 * np.pi * ((a[:, None] * a[None, :]) % n2) / n2
    fr, fi = np.cos(th2), -np.sin(th2)
    tw = 2 * np.pi * ((np.arange(nk)[:, None] * a[None, :]) % N) / N
    tr, ti = np.cos(tw)[:, None, :], -np.sin(tw)[:, None, :]
    gr = fr[None] * tr - fi[None] * ti
    gi = fr[None] * ti + fi[None] * tr
    gf = np.concatenate([np.concatenate([gr, -gi], axis=2),
                         np.concatenate([gi, gr], axis=2)], axis=1)
    gb = np.transpose(gf, (0, 2, 1))
    cvt = lambda m: jnp.asarray(m.astype(np.float32)).astype(BF16)
    return cvt(m1f), cvt(gf), cvt(gb), cvt(m1i)


def _fft_block_fwd(src3_ref, a_ref, m1f_ref):
    nh, n2, c = src3_ref.shape
    na = a_ref.shape[0]
    m1f = m1f_ref[...]
    group = lambda g: slice(SUBLANES * g, SUBLANES * (g + 1))
    for g in range(0, n2 // SUBLANES, 2):
        xg = jnp.concatenate([src3_ref[:, group(g + i), :].reshape(nh * SUBLANES, c) for i in range(2)],
                             axis=1).astype(BF16)
        out = jnp.dot(m1f, xg, preferred_element_type=F32)
        for i in range(2):
            a_ref[:, group(g + i), :] = out[:, c * i:c * (i + 1)].reshape(na, SUBLANES, c)


def _fft_block_inv(a_ref, dst3_ref, m1i_ref):
    nh, n2, c = dst3_ref.shape
    na = a_ref.shape[0]
    m1i = m1i_ref[...]
    group = lambda g: slice(SUBLANES * g, SUBLANES * (g + 1))
    for g in range(0, n2 // SUBLANES, 2):
        ag = jnp.concatenate([a_ref[:, group(g + i), :].reshape(na * SUBLANES, c) for i in range(2)],
                             axis=1).astype(BF16)
        out = jnp.dot(m1i, ag, preferred_element_type=F32)
        for i in range(2):
            dst3_ref[:, group(g + i), :] = out[:, c * i:c * (i + 1)].reshape(nh, SUBLANES, c)


def _hy_mlp(z_ref, w1_ref, b1_ref, w2_ref, b2_ref):
    h = jnp.sin(jnp.dot(z_ref[...], w1_ref[...], precision=HI, preferred_element_type=F32) + b1_ref[...])
    return jnp.sin(jnp.dot(h, w2_ref[...], precision=HI, preferred_element_type=F32) + b2_ref[...])


def _hy_taps(h2, w3_ref, b3_ref, delta_ref, zero_first):
    L = h2.shape[0]
    h = jnp.dot(h2, w3_ref[...], precision=HI, preferred_element_type=F32) + b3_ref[...]
    row = lax.broadcasted_iota(jnp.int32, h.shape, 0)
    t = row.astype(F32) * (1.0 / (L - 1))
    h = h * jnp.exp(-t * delta_ref[...])
    if zero_first:
        h = jnp.where(row == 0, 0.0, h)
    return h


def _hy_mlp_kernel(z_ref, w1_ref, b1_ref, w2_ref, b2_ref, o_ref):
    o_ref[...] = _hy_mlp(z_ref, w1_ref, b1_ref, w2_ref, b2_ref)


def _hy_hidden(zfeat, w1, b1, w2, b2):
    args = (zfeat, w1, b1, w2, b2)
    return pl.pallas_call(
        _hy_mlp_kernel,
        grid=(1,),
        in_specs=[pl.BlockSpec(a.shape, lambda i: (0, 0)) for a in args],
        out_specs=pl.BlockSpec((zfeat.shape[0], w2.shape[1]), lambda i: (0, 0)),
        out_shape=jax.ShapeDtypeStruct((zfeat.shape[0], w2.shape[1]), F32),
        compiler_params=_cparams("arbitrary"),
        name="hy_mlp",
    )(*args)


def _hy_filter_kernel(h2_ref, w3f_ref, b3f_ref, w3b_ref, b3b_ref,
                      delta_ref, m1f_ref, gf_ref, kf_ref, src_ref, af_ref, ab_ref):
    h2 = h2_ref[...]
    hf = _hy_taps(h2, w3f_ref, b3f_ref, delta_ref, False)
    hb = _hy_taps(h2, w3b_ref, b3b_ref, delta_ref, True)
    L, c = hf.shape
    n_total = 2.0 * L
    inv = 1.0 / ((jnp.sum(jnp.abs(hf), axis=0, keepdims=True)
                  + jnp.sum(jnp.abs(hb), axis=0, keepdims=True)) * n_total)
    nh, n2 = src_ref.shape[0], src_ref.shape[1]
    src_ref[...] = (hf * inv).reshape(nh, n2, c)
    _fft_block_fwd(src_ref, af_ref, m1f_ref)
    src_ref[...] = (hb * inv).reshape(nh, n2, c)
    _fft_block_fwd(src_ref, ab_ref, m1f_ref)

    def body(k1, carry):
        sl = pl.ds(pl.multiple_of(2 * k1, 2), 2)
        g = gf_ref[k1]
        xf = jnp.dot(g, af_ref[sl].reshape(2 * n2, c).astype(BF16), preferred_element_type=F32)
        xb = jnp.dot(g, ab_ref[sl].reshape(2 * n2, c).astype(BF16), preferred_element_type=F32)
        kr = xf[:n2] + xb[:n2]
        ki = xf[n2:] - xb[n2:]
        kf_ref[0, sl] = jnp.concatenate([kr, ki], axis=0).reshape(2, n2, c).astype(BF16)
        return carry

    lax.fori_loop(0, gf_ref.shape[0], body, 0, unroll=FFT_UNROLL)


def _hy_filter(h2, w3, b3, delta, m1f, gf):
    C = delta.shape[1]
    cb = LANES
    ncb = C // cb
    nh, n2 = FFT_N1 // 2, FFT_N2
    na = 2 * gf.shape[0]
    full = lambda a: pl.BlockSpec(a.shape, lambda o, j: (0,) * a.ndim)
    return pl.pallas_call(
        _hy_filter_kernel,
        grid=(2, ncb),
        in_specs=[full(h2),
                  pl.BlockSpec((w3.shape[0], cb), lambda o, j: (0, o * 2 * ncb + j)),
                  pl.BlockSpec((1, cb), lambda o, j: (0, o * 2 * ncb + j)),
                  pl.BlockSpec((w3.shape[0], cb), lambda o, j: (0, o * 2 * ncb + ncb + j)),
                  pl.BlockSpec((1, cb), lambda o, j: (0, o * 2 * ncb + ncb + j)),
                  pl.BlockSpec((1, cb), lambda o, j: (0, j)),
                  _const_spec(m1f.shape), _const_spec(gf.shape)],
        out_specs=pl.BlockSpec((1, na, n2, cb), lambda o, j: (o, 0, 0, j)),
        out_shape=jax.ShapeDtypeStruct((2, na, n2, C), BF16),
        scratch_shapes=[pltpu.VMEM((nh, n2, cb), F32), pltpu.VMEM((na, n2, cb), F32),
                        pltpu.VMEM((na, n2, cb), F32)],
        compiler_params=_cparams("arbitrary", "arbitrary"),
        name="hy_filter",
    )(h2, w3, b3, w3, b3, delta, m1f, gf)


def _hy_conv_kernel(conv_z, norm_out, z_ref, g_ref, cwz_ref, cbz_ref, cwg_ref, cbg_ref, skip_ref,
                    kf_ref, m1f_ref, gf_ref, gb_ref, m1i_ref, o_ref, zs_ref, ys_ref, a_ref):
    nh, n2, c = zs_ref.shape
    z = z_ref[0]
    if conv_z:
        z = _conv3(z, cwz_ref[...], cbz_ref[...])
    zs_ref[...] = z.reshape(nh, n2, c)
    _fft_block_fwd(zs_ref, a_ref, m1f_ref)

    def body(k1, carry):
        sl = pl.ds(pl.multiple_of(2 * k1, 2), 2)
        x = jnp.dot(gf_ref[k1], a_ref[sl].reshape(2 * n2, c).astype(BF16), preferred_element_type=F32)
        kf = kf_ref[0, sl].astype(F32)
        xr, xi, kr, ki = x[:n2], x[n2:], kf[0], kf[1]
        y = jnp.concatenate([xr * kr - xi * ki, xr * ki + xi * kr], axis=0).astype(BF16)
        a_ref[sl] = jnp.dot(gb_ref[k1], y, preferred_element_type=F32).reshape(2, n2, c)
        return carry

    lax.fori_loop(0, gf_ref.shape[0], body, 0, unroll=FFT_UNROLL)
    _fft_block_inv(a_ref, ys_ref, m1i_ref)
    y = ys_ref[...].reshape(nh * n2, c)
    z = zs_ref[...].reshape(nh * n2, c)
    gate = _conv3(g_ref[0], cwg_ref[...], cbg_ref[...])
    out = gate * (y + skip_ref[...] * z)
    if norm_out:
        out = _group_rms_2x64(out)
    o_ref[0] = out.astype(o_ref.dtype)


def _hy_conv(order, z, p_hy, conv_w, conv_b, skip, kf, tables):
    m1f, gf, gb, m1i = tables
    B, L, _ = p_hy.shape
    C = skip.shape[1]
    cb = LANES
    ncb = C // cb
    nh, n2 = FFT_N1 // 2, FFT_N2
    na = 2 * gf.shape[0]
    gcol = (order + 1) * ncb
    cw = lambda base: pl.BlockSpec((3, cb), lambda j, b: (0, base + j))
    cbs = lambda base: pl.BlockSpec((1, cb), lambda j, b: (0, base + j))
    return pl.pallas_call(
        functools.partial(_hy_conv_kernel, order == 0, order == 1),
        grid=(ncb, B),
        in_specs=[pl.BlockSpec((1, L, cb), lambda j, b: (b, 0, j)),
                  pl.BlockSpec((1, L, cb), lambda j, b: (b, 0, gcol + j)),
                  cw(0), cbs(0), cw(gcol), cbs(gcol),
                  pl.BlockSpec((1, cb), lambda j, b: (0, j)),
                  pl.BlockSpec((1, na, n2, cb), lambda j, b: (order, 0, 0, j)),
                  _const_spec(m1f.shape), _const_spec(gf.shape), _const_spec(gb.shape),
                  _const_spec(m1i.shape)],
        out_specs=pl.BlockSpec((1, L, cb), lambda j, b: (b, 0, j)),
        out_shape=jax.ShapeDtypeStruct((B, L, C), F32 if order == 0 else BF16),
        scratch_shapes=[pltpu.VMEM((nh, n2, cb), F32), pltpu.VMEM((nh, n2, cb), F32),
                        pltpu.VMEM((na, n2, cb), F32)],
        compiler_params=_cparams("arbitrary", "arbitrary"),
        name=f"hy_conv{order}",
    )(z, p_hy, conv_w, conv_b, conv_w, conv_b, skip[order:order + 1], kf, m1f, gf, gb, m1i)


def _hy_ctx_kernel(p_ref, h2_ref, w3_ref, b3_ref, delta_ref,
                   cw_ref, cb_ref, skip_ref, fd_ref, fi_ref, o_ref):
    lc = p_ref.shape[1]
    C = skip_ref.shape[1]
    h2 = h2_ref[...]
    u = _conv3(p_ref[0], cw_ref[...], cb_ref[...])
    z = u[:, 0:C]
    fd = fd_ref[...]
    fi = fi_ref[...]
    nf = fd.shape[0] // 2
    dot = lambda a, b: jnp.dot(a.astype(BF16), b.astype(BF16), preferred_element_type=F32)
    for o in range(2):
        base = 2 * o * C
        hf = _hy_taps(h2, w3_ref.at[:, base:base + C], b3_ref.at[:, base:base + C], delta_ref, False)
        hb = _hy_taps(h2, w3_ref.at[:, base + C:base + 2 * C], b3_ref.at[:, base + C:base + 2 * C],
                      delta_ref, True)
        inv = 1.0 / ((jnp.sum(jnp.abs(hf), axis=0, keepdims=True)
                      + jnp.sum(jnp.abs(hb), axis=0, keepdims=True)) * (2.0 * lc))
        xf = dot(fd, hf * inv)
        xb = dot(fd, hb * inv)
        kr = xf[:nf] + xb[:nf]
        ki = xf[nf:] - xb[nf:]
        x = dot(fd, z)
        xr, xi = x[:nf], x[nf:]
        y = dot(fi, jnp.concatenate([xr * kr - xi * ki, xr * ki + xi * kr], axis=0))
        z = u[:, (o + 1) * C:(o + 2) * C] * (y + skip_ref[o:o + 1, :] * z)
    for j in range(C // LANES):
        o_ref[0, :, j * LANES:(j + 1) * LANES] = _group_rms_2x64(z[:, j * LANES:(j + 1) * LANES]).astype(BF16)


def _dense_dft_tables(lc):
    n = 2 * lc
    k = np.arange(n)[:, None]
    t = np.arange(lc)[None, :]
    th = 2 * np.pi * ((k * t) % n) / n
    fd = np.concatenate([np.cos(th), -np.sin(th)], axis=0)
    fi = np.concatenate([np.cos(th).T, -np.sin(th).T], axis=1)
    return jnp.asarray(fd.astype(np.float32)), jnp.asarray(fi.astype(np.float32))


def _hy_ctx(p_hy, h2, w3, b3, delta, conv_w, conv_b, skip, fd, fi):
    B, lc, w = p_hy.shape
    C = skip.shape[1]
    full = lambda a: pl.BlockSpec(a.shape, lambda b: (0,) * a.ndim)
    args = (h2, w3, b3, delta, conv_w, conv_b, skip, fd, fi)
    return pl.pallas_call(
        _hy_ctx_kernel,
        grid=(B,),
        in_specs=[pl.BlockSpec((1, lc, w), lambda b: (b, 0, 0))] + [full(a) for a in args],
        out_specs=pl.BlockSpec((1, lc, C), lambda b: (b, 0, 0)),
        out_shape=jax.ShapeDtypeStruct((B, lc, C), BF16),
        compiler_params=_cparams("parallel"),
        name="hy_ctx",
    )(p_hy, *args)


def _attn_kernel(seg_lens, lam_init, q_ref, lam_ref, *refs):
    nseg = len(seg_lens)
    k_refs, v_refs = refs[:nseg], refs[nseg:2 * nseg]
    o_ref = refs[2 * nseg]
    n_s, n_p = 3, 2
    s_refs = refs[2 * nseg + 1:2 * nseg + 1 + n_s]
    p_flat = refs[2 * nseg + 1 + n_s:2 * nseg + 1 + n_s + 2 * n_p]
    p_refs = [p_flat[2 * i:2 * i + 2] for i in range(n_p)]
    tq = q_ref.shape[1]
    th = s_refs[0].shape[0]
    dl = lam_ref[...]
    lam = (jnp.exp(jnp.sum(dl[0:1] * dl[1:2], keepdims=True))
           - jnp.exp(jnp.sum(dl[2:3] * dl[3:4], keepdims=True))) + lam_init

    chunks = []
    off = 0
    for s, n in enumerate(seg_lens):
        for st in range(0, n, TK):
            w = min(TK, n - st)
            chunks.append((s, st, off, w))
            off += w

    nh = tq // th

    def lane_tiles(a):
        return [a[:, t * LANES:(t + 1) * LANES] for t in range(a.shape[1] // LANES)]

    def scores(h, m):
        q = q_ref[0, h * th:(h + 1) * th, :]
        lane = lax.broadcasted_iota(jnp.int32, q.shape, 1)
        qm = jnp.where((lane < HEAD_DIM) == (m == 0), q, jnp.zeros_like(q))
        s_ref = s_refs[(2 * h + m) % n_s]
        mxw = jnp.full((th, LANES), -jnp.inf, F32)
        for (s, st, o, w) in chunks:
            sc = lax.dot_general(qm, k_refs[s][0, st:st + w, :], (((1,), (1,)), ((), ())),
                                 preferred_element_type=F32)
            s_ref[:, o:o + w] = sc
            for t in lane_tiles(sc):
                mxw = jnp.maximum(mxw, t)
        return jnp.max(mxw, axis=-1, keepdims=True)

    def probs(h, m, mx):
        s_ref = s_refs[(2 * h + m) % n_s]
        p_ref = p_refs[h % n_p][m]
        def chunk(o, w, lw):
            p = jnp.exp2(s_ref[:, pl.ds(o, w)] - mx)
            for t in lane_tiles(p):
                lw = lw + t
            p_ref[:, pl.ds(o, w)] = p.astype(BF16)
            return lw

        lw = jnp.zeros((th, LANES), F32)
        i = 0
        while i < len(chunks):
            o0, w = chunks[i][2], chunks[i][3]
            n = 1
            while i + n < len(chunks) and chunks[i + n][3] == w and chunks[i + n][2] == o0 + n * w:
                n += 1
            if n == 1:
                lw = chunk(o0, w, lw)
            else:
                lw = lax.fori_loop(0, n, lambda c, a, o0=o0, w=w: chunk(pl.multiple_of(o0 + c * w, LANES), w, a), lw)
            i += n
        return 1.0 / jnp.sum(lw, axis=-1, keepdims=True)

    def output(h, inv_l0, inv_l1):
        p0_ref, p1_ref = p_refs[h % n_p]
        r = jnp.broadcast_to(lam * inv_l1 / inv_l0, (th, LANES)).astype(BF16)
        acc = jnp.zeros((th, v_refs[0].shape[2]), F32)
        for (s, st, o, w) in chunks:
            rb = jnp.concatenate([r] * (w // LANES), axis=1)
            wgt = p0_ref[:, o:o + w] - p1_ref[:, o:o + w] * rb
            acc = acc + jnp.dot(wgt, v_refs[s][0, st:st + w, :], preferred_element_type=F32)
        o_ref[0, h * th:(h + 1) * th, :] = (_rms(acc * inv_l0) * (1.0 - lam_init)).astype(BF16)

    mx = {(0, 0): scores(0, 0), (0, 1): scores(0, 1)}
    for h in range(nh):
        il0 = probs(h, 0, mx[h, 0])
        if h + 1 < nh:
            mx[h + 1, 0] = scores(h + 1, 0)
        il1 = probs(h, 1, mx[h, 1])
        if h + 1 < nh:
            mx[h + 1, 1] = scores(h + 1, 1)
        output(h, il0, il1)


def _diff_attn(q, ks, vs, da_lambda, lam_init):
    B, lq, w = q.shape
    H = w // LANES
    tq = min(TQ, lq)
    th = min(TQ_SUB, tq // 2)
    seg_lens = tuple(k.shape[1] for k in ks)
    seg = lambda n: pl.BlockSpec((1, n, LANES), lambda b, h, i: (b, 0, h))
    return pl.pallas_call(
        functools.partial(_attn_kernel, seg_lens, lam_init),
        grid=(B, H, lq // tq),
        in_specs=[pl.BlockSpec((1, tq, LANES), lambda b, h, i: (b, i, h)),
                  pl.BlockSpec(da_lambda.shape, lambda b, h, i: (0, 0))]
                 + [seg(n) for n in seg_lens] + [seg(n) for n in seg_lens],
        out_specs=pl.BlockSpec((1, tq, LANES), lambda b, h, i: (b, i, h)),
        out_shape=jax.ShapeDtypeStruct((B, lq, w), BF16),
        scratch_shapes=[pltpu.VMEM((th, sum(seg_lens)), F32)] * 3
                       + [pltpu.VMEM((th, sum(seg_lens)), BF16)] * 4,
        compiler_params=_cparams("parallel", "parallel", "arbitrary"),
        name="diff_attn",
    )(q, da_lambda, *ks, *vs)


def _split3(x):
    hi = x.astype(BF16)
    r1 = x - hi.astype(F32)
    mid = r1.astype(BF16)
    return hi, mid, (r1 - mid.astype(F32)).astype(BF16)


def _mlstm_cumsums(gc, gr, tri_lo, tri_up):
    dot = lambda a, b: jnp.dot(a, b, preferred_element_type=F32)
    lf_c = _log_sigmoid(gc)
    lf_r = _log_sigmoid(gr)
    T = gc.shape[0]
    pre_c = sum(dot(tri_lo, t) for t in _split3(lf_c))
    pre_r = sum(dot(t, tri_up) for t in _split3(lf_r))
    suf_c = pre_c[T - 1:T, :] - pre_c + lf_c
    suf_r = pre_r[:, T - 1:T] - pre_r + lf_r
    fwd_lane = lax.broadcasted_iota(jnp.int32, gc.shape, 1) < 4
    fwd_row = lax.broadcasted_iota(jnp.int32, gr.shape, 0) < 4
    return jnp.where(fwd_lane, pre_c, suf_c), jnp.where(fwd_row, pre_r, suf_r)


def _mlstm_chunk(rev, q, k, vt, gc, gr, csum_c, csum_r, c_ref, n_ref, m_ref):
    T = q.shape[0]
    d = 6 if rev else 2
    ii = 4 if rev else 0
    lo = lax.broadcasted_iota(jnp.int32, (T, LANES), 1) < HEAD_DIM
    lo_row = lax.broadcasted_iota(jnp.int32, (1, LANES), 1) < HEAD_DIM
    sub = lax.broadcasted_iota(jnp.int32, (LANES, T), 0) < HEAD_DIM
    key_i = lax.broadcasted_iota(jnp.int32, (T, T), 0)
    qry_i = lax.broadcasted_iota(jnp.int32, (T, T), 1)
    mask = (key_i >= qry_i) if rev else (key_i <= qry_i)
    last = 0 if rev else T - 1
    qb = q.astype(BF16)
    kb = k.astype(BF16)
    vtb = vt.astype(BF16)
    zero = jnp.zeros_like(kb)
    nt = (((1,), (1,)), ((), ()))
    brow, igrow, mloc, numt, dsum = [], [], [], [], []
    for j in range(2):
        brow.append(csum_r[d + j:d + j + 1, :])
        igrow.append(gr[ii + j:ii + j + 1, :])
        ccol = gc[:, ii + j:ii + j + 1] - csum_c[:, d + j:d + j + 1]
        logd = jnp.where(mask, brow[j] + ccol, -jnp.inf)
        mloc.append(jnp.max(logd, axis=0, keepdims=True))
        kj = jnp.where(lo, kb, zero) if j == 0 else jnp.where(lo, zero, kb)
        st = lax.dot_general(kj, qb, nt, preferred_element_type=F32) * jnp.exp(logd - mloc[j])
        numt.append(jnp.dot(vtb, st.astype(BF16), preferred_element_type=F32))
        dsum.append(jnp.sum(st, axis=0, keepdims=True))
    num_loc = jnp.where(sub, numt[0], numt[1])

    n = n_ref[...]
    row8 = lax.broadcasted_iota(jnp.int32, (SUBLANES, LANES), 0)
    lo8 = lax.broadcasted_iota(jnp.int32, (SUBLANES, LANES), 1) < HEAD_DIM
    n8 = jnp.where(row8 == jnp.where(lo8, 0, 1), jnp.broadcast_to(n, (SUBLANES, LANES)), 0.0)
    qn = lax.dot_general(n8.astype(BF16), qb, nt, preferred_element_type=F32)
    hden, wrow, decay, interw, aloc, m_new = [], [], [], [], [], []
    for j in range(2):
        m_prev = m_ref[j:j + 1, 0:1]
        mrow = jnp.maximum(brow[j] + m_prev, mloc[j])
        a = jnp.exp(mloc[j] - mrow)
        iw = jnp.exp(brow[j] + m_prev - mrow)
        dn = a * dsum[j] + iw * qn[j:j + 1, :]
        hden.append(jnp.maximum(jnp.abs(dn), jnp.exp(-mrow)))
        aloc.append(a)
        interw.append(iw)
        mn = mrow[:, last:last + 1]
        btot = brow[j][:, last:last + 1]
        wrow.append(jnp.exp(btot - brow[j] + igrow[j] - mn))
        decay.append(jnp.exp(btot + m_prev - mn))
        m_new.append(mn)
    c = c_ref[...]
    intert = lax.dot_general(c.astype(BF16), qb, nt, preferred_element_type=F32)
    num = jnp.where(sub, aloc[0], aloc[1]) * num_loc + jnp.where(sub, interw[0], interw[1]) * intert
    ht = num / jnp.where(sub, hden[0], hden[1])
    wvt = (vt * jnp.where(sub, wrow[0], wrow[1])).astype(BF16)
    upd = jnp.dot(wvt, kb, preferred_element_type=F32)
    rr = lax.broadcasted_iota(jnp.int32, (LANES, LANES), 0) < HEAD_DIM
    cc = lax.broadcasted_iota(jnp.int32, (LANES, LANES), 1) < HEAD_DIM
    dcol = jnp.where(lax.broadcasted_iota(jnp.int32, (LANES, 1), 0) < HEAD_DIM, decay[0], decay[1])
    c_ref[...] = dcol * c + jnp.where(rr == cc, upd, 0.0)
    rowt = lax.broadcasted_iota(jnp.int32, (SUBLANES, T), 0)
    w8 = jnp.where(rowt == 0, wrow[0], jnp.where(rowt == 1, wrow[1], 0.0))
    r8 = jnp.dot(w8.astype(BF16), kb, preferred_element_type=F32)
    n_ref[...] = jnp.where(lo_row, decay[0] * n + r8[0:1, :], decay[1] * n + r8[1:2, :])
    m_ref[0:1, :] = jnp.broadcast_to(m_new[0], (1, LANES))
    m_ref[1:2, :] = jnp.broadcast_to(m_new[1], (1, LANES))
    return ht


def _mlstm_kernel(qc_ref, kc_ref, vc_ref, oc_ref, gcc_ref, grc_ref,
                  qx_ref, kx_ref, vx_ref, ox_ref, gcx_ref, grx_ref,
                  cwq_ref, cbq_ref, cwk_ref, cbk_ref, gbc_ref, gbr_ref,
                  yc_ref, yx_ref,
                  qs_ref, ks_ref, vt_ref, hf_ref, hb_ref, csc_ref, csr_ref, ct_ref, n_ref, m_ref):
    T = ML_T
    lx = qx_ref.shape[1]
    nx = lx // T
    r_i = lax.broadcasted_iota(jnp.int32, (T, T), 0)
    c_i = lax.broadcasted_iota(jnp.int32, (T, T), 1)
    tri_lo = jnp.where(c_i <= r_i, 1.0, 0.0).astype(BF16)
    tri_up = jnp.where(c_i >= r_i, 1.0, 0.0).astype(BF16)
    ct_ref[...] = jnp.zeros_like(ct_ref)
    n_ref[...] = jnp.zeros_like(n_ref)
    m_ref[...] = jnp.zeros_like(m_ref)
    scale = HEAD_DIM ** -0.5
    cwq, cbq, cwk, cbk = cwq_ref[...], cbq_ref[...], cwk_ref[...], cbk_ref[...]
    gbc, gbr = gbc_ref[...], gbr_ref[...]

    qc = _silu(_conv3(qc_ref[0], cwq, cbq))
    kc = _silu(_conv3(kc_ref[0], cwk, cbk)) * scale
    vct = vc_ref[0].T
    gcc = gcc_ref[0] + gbc
    grc = grc_ref[0] + gbr
    h_ct = None
    csc, csr = _mlstm_cumsums(gcc, grc, tri_lo, tri_up)
    for rev in (False, True):
        di = int(rev)
        ht = _mlstm_chunk(rev, qc, kc, vct, gcc, grc, csc, csr,
                          ct_ref.at[di], n_ref.at[di], m_ref.at[di])
        h_ct = ht if h_ct is None else h_ct + ht
    yc_ref[0] = _group_rms_2x64(h_ct.T * _sigmoid(oc_ref[0])).astype(BF16)

    qs_ref[...] = _silu(_conv3(qx_ref[0], cwq, cbq))
    ks_ref[...] = _silu(_conv3(kx_ref[0], cwk, cbk)) * scale
    for c in range(nx):
        rows = slice(c * T, (c + 1) * T)
        vt_ref[:, rows] = vx_ref[0, rows, :].T
        csc_ref[rows, :], csr_ref[:, rows] = _mlstm_cumsums(gcx_ref[0, rows, :] + gbc,
                                                            grx_ref[0, :, rows] + gbr, tri_lo, tri_up)

    def body(c, carry):
        for rev in (False, True):
            di = int(rev)
            cidx = (nx - 1 - c) if rev else c
            r0 = pl.multiple_of(cidx * T, T)
            rows = pl.ds(r0, T)
            ht = _mlstm_chunk(rev, qs_ref[rows, :], ks_ref[rows, :], vt_ref[:, rows],
                              gcx_ref[0, rows, :] + gbc, grx_ref[0, :, rows] + gbr,
                              csc_ref[rows, :], csr_ref[:, rows],
                              ct_ref.at[di], n_ref.at[di], m_ref.at[di])
            if rev:
                hb_ref[:, rows] = ht
            else:
                hf_ref[:, rows] = ht
        return carry

    lax.fori_loop(0, nx, body, 0, unroll=4)
    for c in range(nx):
        rows = slice(c * T, (c + 1) * T)
        h = (hf_ref[:, rows] + hb_ref[:, rows]).T
        yx_ref[0, rows, :] = _group_rms_2x64(h * _sigmoid(ox_ref[0, rows, :])).astype(BF16)


def _mlstm(seg_c, seg_x, conv_w, conv_b, gate_bc, gate_br):
    B, lx, _ = seg_x[1].shape
    lc = seg_c[1].shape[1]
    assert lc == ML_T and lx % ML_T == 0

    def seg_specs(L):
        col = lambda off: pl.BlockSpec((1, L, LANES), lambda b, p, off=off: (b, 0, off + p))
        return [col(0), col(2), col(0), col(0), col(0),
                pl.BlockSpec((1, SUBLANES, L), lambda b, p: (b, p, 0))]

    def seg_args(s):
        qk, v, o, g, gt = s
        return [qk, qk, v, o, g, gt]

    wspec = lambda rows, off: pl.BlockSpec((rows, LANES), lambda b, p, off=off: (0, off + p))
    return pl.pallas_call(
        _mlstm_kernel,
        grid=(B, 2),
        in_specs=seg_specs(lc) + seg_specs(lx)
                 + [wspec(3, 0), wspec(1, 0), wspec(3, 2), wspec(1, 2), wspec(1, 0),
                    pl.BlockSpec((SUBLANES, 1), lambda b, p: (p, 0))],
        out_specs=[pl.BlockSpec((1, lc, LANES), lambda b, p: (b, 0, p)),
                   pl.BlockSpec((1, lx, LANES), lambda b, p: (b, 0, p))],
        out_shape=[jax.ShapeDtypeStruct((B, lc, 2 * LANES), BF16),
                   jax.ShapeDtypeStruct((B, lx, 2 * LANES), BF16)],
        scratch_shapes=[pltpu.VMEM((lx, LANES), F32), pltpu.VMEM((lx, LANES), F32),
                        pltpu.VMEM((LANES, lx), F32), pltpu.VMEM((LANES, lx), F32),
                        pltpu.VMEM((LANES, lx), F32),
                        pltpu.VMEM((lx, LANES), F32), pltpu.VMEM((SUBLANES, lx), F32),
                        pltpu.VMEM((2, LANES, LANES), F32), pltpu.VMEM((2, 1, LANES), F32),
                        pltpu.VMEM((2, 2, LANES), F32)],
        compiler_params=_cparams("parallel", "arbitrary"),
        name="mlstm",
    )(*seg_args(seg_c), *seg_args(seg_x), conv_w, conv_b, conv_w, conv_b, gate_bc, gate_br)


def _outproj_kernel(x_ref, g_ref, hy_ref, da_ref, ml_ref, gain_ref, w_ref, o_ref):
    c_hy = hy_ref.shape[2]
    c_da = da_ref.shape[2]
    gain = gain_ref[...]
    acc = jnp.dot((hy_ref[0] * gain[:, 0:c_hy]).astype(BF16), w_ref[0:c_hy, :], preferred_element_type=F32)
    acc += jnp.dot((da_ref[0] * gain[:, c_hy:c_hy + c_da]).astype(BF16), w_ref[c_hy:c_hy + c_da, :],
                   preferred_element_type=F32)
    acc += jnp.dot((ml_ref[0] * gain[:, c_hy + c_da:]).astype(BF16), w_ref[c_hy + c_da:, :],
                   preferred_element_type=F32)
    o_ref[0] = x_ref[0] + g_ref[0] * acc


def _outproj(x, gate, y_hy, y_da, y_ml, gain, w_out):
    B, L, D = x.shape
    tm = min(TM, L)
    tok = lambda w: pl.BlockSpec((1, tm, w), lambda b, i: (b, i, 0))
    return pl.pallas_call(
        _outproj_kernel,
        grid=(B, L // tm),
        in_specs=[tok(D), pl.BlockSpec((1, 1, D), lambda b, i: (b, 0, 0)),
                  tok(y_hy.shape[2]), tok(y_da.shape[2]), tok(y_ml.shape[2]),
                  pl.BlockSpec(gain.shape, lambda b, i: (0, 0)), _const_spec(w_out.shape)],
        out_specs=tok(D),
        out_shape=jax.ShapeDtypeStruct((B, L, D), F32),
        compiler_params=_cparams("parallel", "parallel"),
        name="outproj",
    )(x, gate, y_hy, y_da, y_ml, gain, w_out)


def _ffn_kernel(final, x_ref, prev_ref, next_ref, sh_ref, sc_ref, g_ref, up_ref, cw_ref, cb_ref,
                down_ref, fw_ref, o_ref, act_ref):
    i = pl.program_id(1)
    last = pl.num_programs(1) - 1
    tm = x_ref.shape[1]
    d_ff = down_ref.shape[0]
    x = x_ref[0]
    mod = lambda a: _rms(a) * (1.0 + sc_ref[0]) + sh_ref[0]
    hp = jnp.where(i == 0, 0.0, mod(prev_ref[0]))
    hn = jnp.where(i == last, 0.0, mod(next_ref[0]))
    h = jnp.concatenate([hp, mod(x), hn], axis=0).astype(BF16)
    ext = tm + 2 * SUBLANES

    def conv_cols(lo, hi):
        u = jnp.dot(h, up_ref[:, lo:hi], preferred_element_type=F32)
        w = cw_ref[:, lo:hi]
        c = (pltpu.roll(u, 1, axis=0) * w[0:1] + u * w[1:2] + pltpu.roll(u, ext - 1, axis=0) * w[2:3]
             + cb_ref[:, lo:hi])
        return c[SUBLANES:SUBLANES + tm]

    for j in range(d_ff // FFN_TN):
        lo = j * FFN_TN
        a = conv_cols(lo, lo + FFN_TN)
        g = conv_cols(d_ff + lo, d_ff + lo + FFN_TN)
        act_ref[:, lo:lo + FFN_TN] = (_silu(g) * a).astype(BF16)
    y = x + g_ref[0] * jnp.dot(act_ref[...], down_ref[...], preferred_element_type=F32)
    if final:
        y = _rms(y) * fw_ref[...]
    o_ref[0] = y


def _ffn(x, shift, scale, gate, up, conv_w, conv_b, down, final_w, final):
    B, L, D = x.shape
    tm = min(TM, L)
    nb = tm // SUBLANES
    nrow = L // SUBLANES
    tok = pl.BlockSpec((1, tm, D), lambda b, i: (b, i, 0))
    mod = pl.BlockSpec((1, 1, D), lambda b, i: (b, 0, 0))
    full = lambda a: pl.BlockSpec(a.shape, lambda b, i: (0,) * a.ndim)
    return pl.pallas_call(
        functools.partial(_ffn_kernel, final),
        grid=(B, L // tm),
        in_specs=[tok,
                  pl.BlockSpec((1, SUBLANES, D), lambda b, i: (b, jnp.maximum(i * nb - 1, 0), 0)),
                  pl.BlockSpec((1, SUBLANES, D), lambda b, i: (b, jnp.minimum((i + 1) * nb, nrow - 1), 0)),
                  mod, mod, mod, _const_spec(up.shape), full(conv_w), full(conv_b),
                  _const_spec(down.shape), full(final_w)],
        out_specs=tok,
        out_shape=jax.ShapeDtypeStruct((B, L, D), F32),
        scratch_shapes=[pltpu.VMEM((tm, down.shape[0]), BF16)],
        compiler_params=_cparams("parallel", "parallel"),
        name="ffn_final" if final else "ffn",
    )(x, x, x, shift, scale, gate, up, conv_w, conv_b, down, final_w)


def _rope_tables(L):
    rows_n = L // GRID_W
    rows = jnp.repeat(jnp.arange(rows_n, dtype=F32), GRID_W)
    cols = jnp.tile(jnp.arange(GRID_W, dtype=F32), rows_n)
    nf = HEAD_DIM // 4
    inv = ROPE_BASE ** (-jnp.arange(nf, dtype=F32) / nf)
    cr, sr = jnp.cos(rows[:, None] * inv), jnp.sin(rows[:, None] * inv)
    cc, sc = jnp.cos(cols[:, None] * inv), jnp.sin(cols[:, None] * inv)
    cos64 = jnp.concatenate([cr, cr, cc, cc], axis=-1)
    sin64 = jnp.concatenate([-sr, sr, -sc, sc], axis=-1)
    return jnp.tile(cos64, (1, 2)), jnp.tile(sin64, (1, 2))


def _hy_features(L):
    t = jnp.linspace(0.0, 1.0, L, dtype=F32)
    pos = jnp.arange(L, dtype=F32)
    f = jnp.linspace(1e-4, HY_BANDS - 1, HY_BANDS, dtype=F32)
    ang = (2.0 * math.pi / L) * pos[:, None] * f
    z = jnp.concatenate([t[:, None], jnp.cos(ang), jnp.sin(ang)], axis=-1)
    return jnp.pad(z, ((0, 0), (0, HY_POS_PAD - z.shape[1])))


def _gate_layout(a):
    g = a.reshape(a.shape[:-1] + (4, 2, 2))
    g = jnp.moveaxis(g, -2, -3).reshape(a.shape[:-1] + (2, 8))
    g = jnp.pad(g, [(0, 0)] * (g.ndim - 1) + [(0, LANES - 8)])
    return g.reshape(a.shape[:-1] + (2 * LANES,))


def _gates_t(g):
    return jnp.swapaxes(jnp.concatenate([g[..., 0:8], g[..., LANES:LANES + 8]], axis=-1), 1, 2)


def kernel(x, c, ctx, c_ctx, ada_w, ada_b, w_in, w_out, hy_conv_w, hy_conv_b, hy_w1, hy_b1, hy_w2, hy_b2,
           hy_w3, hy_b3, hy_skip, da_lambda, ml_conv_w, ml_conv_b, ml_gate_b, mix_norm_w, ffn_up,
           ffn_conv_w, ffn_conv_b, ffn_down, final_norm_w):
    B, L, D = x.shape
    lc = ctx.shape[1]
    depth = ada_w.shape[0]
    hy_w = hy_skip.shape[2]
    n_hy = 3 * hy_w
    da_w = 2 * hy_w
    ml_w = hy_w

    rows = ((B + 1 + SUBLANES - 1) // SUBLANES) * SUBLANES
    cc = jnp.zeros((rows, D), F32).at[:B].set(c).at[B].set(c_ctx)
    mods = _adaln(cc, ada_w, ada_b)

    cos_x, sin_x = _rope_tables(L)
    cos_c, sin_c = jnp.ones((lc, LANES), F32), jnp.zeros((lc, LANES), F32)
    tables = _dft_tables(L)
    fd_c, fi_c = _dense_dft_tables(lc)
    zf_x, zf_c = _hy_features(L), _hy_features(lc)
    delta = jnp.abs(jnp.linspace(math.log(HY_DECAY_TARGET) / HY_FAST_DECAY,
                                 math.log(HY_DECAY_TARGET) / HY_SLOW_DECAY, hy_w, dtype=F32))[None, :]
    fw = final_norm_w[None, :]

    for l in range(depth):
        lam_init = 0.8 - 0.6 * math.exp(-0.3 * l)
        update_ctx = l < depth - 1
        mx = [m[:, None, :] for m in jnp.split(mods[l, :B], 6, axis=-1)]
        mc = [jnp.broadcast_to(m[None], (B, 1, D)) for m in jnp.split(mods[l, B:B + 1], 6, axis=-1)]

        w = w_in[l]
        n_main = n_hy + 3 * da_w + 4 * ml_w
        w_ext = jnp.concatenate([w[:, :n_main], _gate_layout(w[:, n_main:])], axis=1).astype(BF16)
        gate_b = _gate_layout(ml_gate_b[l])
        gate_bc = gate_b[None, :]
        gate_br = _gates_t(gate_b[None, None, :])[0]

        px = _inproj(x, mx[0], mx[1], w_ext, cos_x, sin_x)
        pc = _inproj(ctx, mc[0], mc[1], w_ext, cos_c, sin_c)
        hy_x, q_x, k_x, v_x, mqk_x, mv_x, mo_x, g_x = px
        hy_c, q_c, k_c, v_c, mqk_c, mv_c, mo_c, g_c = pc

        w1p = jnp.pad(hy_w1[l], ((0, HY_POS_PAD - hy_w1.shape[1]), (0, 0)))
        b1, b2, b3 = hy_b1[l][None, :], hy_b2[l][None, :], hy_b3[l][None, :]
        cbias = hy_conv_b[l][None, :]
        kf = _hy_filter(_hy_hidden(zf_x, w1p, b1, hy_w2[l], b2), hy_w3[l], b3, delta, tables[0], tables[1])
        z1 = _hy_conv(0, hy_x, hy_x, hy_conv_w[l], cbias, hy_skip[l], kf, tables)
        y_hy_x = _hy_conv(1, z1, hy_x, hy_conv_w[l], cbias, hy_skip[l], kf, tables)

        y_da_x = _diff_attn(q_x, [k_c, k_x], [v_c, v_x], da_lambda[l], lam_init)

        seg_c = (mqk_c, mv_c, mo_c, g_c, _gates_t(g_c))
        seg_x = (mqk_x, mv_x, mo_x, g_x, _gates_t(g_x))
        y_ml_c, y_ml_x = _mlstm(seg_c, seg_x, ml_conv_w[l], ml_conv_b[l][None, :], gate_bc, gate_br)

        gain = mix_norm_w[l][None, :]
        wo = w_out[l].astype(BF16)
        up = ffn_up[l].astype(BF16)
        down = ffn_down[l].astype(BF16)
        fcb = ffn_conv_b[l][None, :]
        x = _outproj(x, mx[2], y_hy_x, y_da_x, y_ml_x, gain, wo)
        x = _ffn(x, mx[3], mx[4], mx[5], up, ffn_conv_w[l], fcb, down, fw, final=not update_ctx)

        if update_ctx:
            y_hy_c = _hy_ctx(hy_c, _hy_hidden(zf_c, w1p, b1, hy_w2[l], b2), hy_w3[l], b3, delta,
                             hy_conv_w[l], cbias, hy_skip[l], fd_c, fi_c)
            y_da_c = _diff_attn(q_c, [k_c], [v_c], da_lambda[l], lam_init)
            ctx = _outproj(ctx, mc[2], y_hy_c, y_da_c, y_ml_c, gain, wo)
            ctx = _ffn(ctx, mc[3], mc[4], mc[5], up, ffn_conv_w[l], fcb, down, fw, final=False)
    return x
```
